```python
import jax, jax.numpy as jnp
from jax import lax
import numpy as np

D_MODEL = 2048
BATCH = 16
SEQ = 2048
DEPTH = 4

D_MIX = D_MODEL
CONV_CH = D_MIX // 2
CONV_GROUPS = 8
CONV_WIDTH = 31
DN_HEADS = 8
DN_HEAD_DIM = (D_MIX - CONV_CH) // DN_HEADS
DN_DIM = DN_HEADS * DN_HEAD_DIM
SHORT_CONV = 4
CHUNK = 64
D_FF = 4 * D_MODEL
N_MOD = 6
EPS = 1e-6
IN_COLS = 2 * CONV_CH + 4 * DN_DIM + 2 * DN_HEADS

kernel_name = "hymba_style_conformer_gdn_hybrid"


def rmsnorm(x, w):
    xf = x.astype(jnp.float32)
    y = xf * lax.rsqrt(jnp.mean(xf * xf, axis=-1, keepdims=True) + EPS)
    return (y * w.astype(jnp.float32)).astype(x.dtype)


def layernorm(x, w, b):
    xf = x.astype(jnp.float32)
    mu = jnp.mean(xf, axis=-1, keepdims=True)
    xc = xf - mu
    y = xc * lax.rsqrt(jnp.mean(xc * xc, axis=-1, keepdims=True) + EPS)
    return (y * w.astype(jnp.float32) + b.astype(jnp.float32)).astype(x.dtype)


def l2norm(x):
    return x * lax.rsqrt(jnp.sum(x * x, axis=-1, keepdims=True) + EPS)


def causal_depthwise_conv(x, w):
    K, C = w.shape
    return lax.conv_general_dilated(
        x, w[:, None, :].astype(x.dtype), window_strides=(1,), padding=[(K - 1, 0)],
        dimension_numbers=("NWC", "WIO", "NWC"), feature_group_count=C)


def gated_delta_rule_chunked(q, k, v, g, beta):
    B, S, H, Dk = q.shape
    Dv = v.shape[-1]
    NC = S // CHUNK

    def to_chunks(t):
        return t.reshape(B, NC, CHUNK, H, t.shape[-1]).transpose(0, 3, 1, 2, 4)

    q = to_chunks(q) * (Dk ** -0.5)
    k = to_chunks(k)
    v = to_chunks(v)
    beta = beta.reshape(B, NC, CHUNK, H).transpose(0, 3, 1, 2)
    g = jnp.cumsum(g.reshape(B, NC, CHUNK, H).transpose(0, 3, 1, 2), axis=-1)

    k_beta = k * beta[..., None]
    v_beta = v * beta[..., None]
    causal = jnp.tril(jnp.ones((CHUNK, CHUNK), dtype=bool))
    strict = jnp.tril(jnp.ones((CHUNK, CHUNK), dtype=bool), -1)
    decay = jnp.exp(jnp.where(causal, g[..., :, None] - g[..., None, :], -jnp.inf))

    L = jnp.where(strict, jnp.einsum("bhncd,bhnmd->bhncm", k_beta, k) * decay, 0.0)
    A = L + jnp.eye(CHUNK, dtype=jnp.float32)
    rhs = jnp.concatenate([v_beta, k_beta * jnp.exp(g)[..., None]], axis=-1)
    sol = lax.linalg.triangular_solve(A, rhs, left_side=True, lower=True, unit_diagonal=True)
    u = sol[..., :Dv]
    w = sol[..., Dv:]

    attn = jnp.where(causal, jnp.einsum("bhncd,bhnmd->bhncm", q, k) * decay, 0.0)
    q_dec = q * jnp.exp(g)[..., None]
    k_dec = k * jnp.exp(g[..., -1:] - g)[..., None]
    g_last = jnp.exp(g[..., -1])

    xs = tuple(jnp.moveaxis(t, 2, 0) for t in (q_dec, k_dec, u, w, attn, g_last))

    def step(state, inp):
        qd, kd, u_c, w_c, a_c, gl = inp
        v_new = u_c - jnp.einsum("bhck,bhkv->bhcv", w_c, state)
        o = (jnp.einsum("bhck,bhkv->bhcv", qd, state)
             + jnp.einsum("bhcm,bhmv->bhcv", a_c, v_new))
        state = state * gl[..., None, None] + jnp.einsum("bhck,bhcv->bhkv", kd, v_new)
        return state, o

    s0 = jnp.zeros((B, H, Dk, Dv), jnp.float32)
    _, o = lax.scan(step, s0, xs)
    return o.transpose(1, 0, 3, 2, 4).reshape(B, S, H, Dv)


def modulate(h, shift, scale):
    return h * (1.0 + scale[:, None, :]) + shift[:, None, :]


def hybrid_layer(x, mod, norm1_w, w_in, conv_dw_w, conv_dw_b, conv_ln_w, conv_ln_b, w_pw2,
                 conv_out_norm_w, qkv_conv_w, a_log, dt_bias, dn_norm_w, w_out,
                 norm2_w, w_up, w_down):
    B, S, _ = x.shape
    shift1, scale1, gate1, shift2, scale2, gate2 = jnp.split(mod, N_MOD, axis=-1)

    h = modulate(rmsnorm(x, norm1_w), shift1, scale1)
    proj = h @ w_in
    cuts = np.cumsum([CONV_CH, CONV_CH, DN_DIM, DN_DIM, DN_DIM, DN_DIM, DN_HEADS])
    c_val, c_gate, q, k, v, z, b_logit, a_logit = jnp.split(proj, cuts, axis=-1)

    u = c_val * jax.nn.sigmoid(c_gate)
    u = causal_depthwise_conv(u, conv_dw_w) + conv_dw_b
    u = jax.nn.silu(layernorm(u, conv_ln_w, conv_ln_b))
    u = u @ w_pw2
    y_conv = rmsnorm(u, conv_out_norm_w)

    qkv = jax.nn.silu(causal_depthwise_conv(jnp.concatenate([q, k, v], axis=-1), qkv_conv_w))
    q, k, v = jnp.split(qkv.astype(jnp.float32), 3, axis=-1)
    q = l2norm(q.reshape(B, S, DN_HEADS, DN_HEAD_DIM))
    k = l2norm(k.reshape(B, S, DN_HEADS, DN_HEAD_DIM))
    v = v.reshape(B, S, DN_HEADS, DN_HEAD_DIM)
    beta = jax.nn.sigmoid(b_logit.astype(jnp.float32))
    g = -jnp.exp(a_log.astype(jnp.float32)) * jax.nn.softplus(
        a_logit.astype(jnp.float32) + dt_bias.astype(jnp.float32))
    o = gated_delta_rule_chunked(q, k, v, g, beta)
    o = rmsnorm(o, dn_norm_w) * jax.nn.silu(
        z.astype(jnp.float32).reshape(B, S, DN_HEADS, DN_HEAD_DIM))
    y_dn = o.reshape(B, S, DN_DIM).astype(x.dtype)

    y = jnp.concatenate([y_conv, y_dn], axis=-1) @ w_out
    x = x + gate1[:, None, :] * y

    h2 = modulate(rmsnorm(x, norm2_w), shift2, scale2)
    m = jnp.square(jax.nn.relu(h2 @ w_up)) @ w_down
    return x + gate2[:, None, :] * m


def _fwd_setup_inputs(seed: int = 0) -> dict:
    key = jax.random.key(seed)
    ks = jax.random.split(key, 24)
    f32 = jnp.float32
    nrm = lambda k, shape, s: jax.random.normal(k, shape, f32) * s
    dt = jnp.exp(jax.random.uniform(ks[15], (DEPTH, DN_HEADS), f32, np.log(1e-3), np.log(1e-1)))
    return {
        "x": nrm(ks[0], (BATCH, SEQ, D_MODEL), 1.0),
        "c": nrm(ks[1], (BATCH, D_MODEL), 1.0),
        "w_ada": nrm(ks[2], (DEPTH, D_MODEL, N_MOD * D_MODEL), 0.5 * D_MODEL ** -0.5),
        "b_ada": nrm(ks[3], (DEPTH, N_MOD * D_MODEL), 0.01),
        "norm1_w": 1.0 + nrm(ks[4], (DEPTH, D_MODEL), 0.02),
        "w_in": nrm(ks[5], (DEPTH, D_MODEL, IN_COLS), D_MODEL ** -0.5),
        "conv_dw_w": nrm(ks[6], (DEPTH, CONV_WIDTH, CONV_CH), CONV_WIDTH ** -0.5),
        "conv_dw_b": nrm(ks[7], (DEPTH, CONV_CH), 0.02),
        "conv_ln_w": 1.0 + nrm(ks[8], (DEPTH, CONV_CH), 0.02),
        "conv_ln_b": nrm(ks[9], (DEPTH, CONV_CH), 0.02),
        "w_pw2": nrm(ks[10], (DEPTH, CONV_CH, CONV_CH), CONV_CH ** -0.5),
        "conv_out_norm_w": 1.0 + nrm(ks[11], (DEPTH, CONV_CH), 0.02),
        "qkv_conv_w": nrm(ks[12], (DEPTH, SHORT_CONV, 3 * DN_DIM), SHORT_CONV ** -0.5),
        "a_log": jnp.log(jax.random.uniform(ks[13], (DEPTH, DN_HEADS), f32, 1.0, 16.0)),
        "dt_bias": dt + jnp.log(-jnp.expm1(-dt)),
        "dn_norm_w": 1.0 + nrm(ks[14], (DEPTH, DN_HEAD_DIM), 0.02),
        "w_out": nrm(ks[16], (DEPTH, D_MIX, D_MODEL), D_MIX ** -0.5),
        "norm2_w": 1.0 + nrm(ks[17], (DEPTH, D_MODEL), 0.02),
        "w_up": nrm(ks[18], (DEPTH, D_MODEL, D_FF), D_MODEL ** -0.5),
        "w_down": nrm(ks[19], (DEPTH, D_FF, D_MODEL), D_FF ** -0.5),
        "final_ada_w": nrm(ks[20], (D_MODEL, 2 * D_MODEL), 0.5 * D_MODEL ** -0.5),
        "final_ada_b": nrm(ks[21], (2 * D_MODEL,), 0.01),
        "final_norm_w": 1.0 + nrm(ks[22], (D_MODEL,), 0.02),
    }


def _fwd_reference(x, c, w_ada, b_ada, norm1_w, w_in, conv_dw_w, conv_dw_b, conv_ln_w, conv_ln_b,
              w_pw2, conv_out_norm_w, qkv_conv_w, a_log, dt_bias, dn_norm_w, w_out,
              norm2_w, w_up, w_down, final_ada_w, final_ada_b, final_norm_w):
    c_act = jax.nn.silu(c)
    for l in range(DEPTH):
        mod = c_act @ w_ada[l] + b_ada[l]
        x = hybrid_layer(x, mod, norm1_w[l], w_in[l], conv_dw_w[l], conv_dw_b[l],
                         conv_ln_w[l], conv_ln_b[l], w_pw2[l], conv_out_norm_w[l],
                         qkv_conv_w[l], a_log[l], dt_bias[l], dn_norm_w[l], w_out[l],
                         norm2_w[l], w_up[l], w_down[l])
    shift_f, scale_f = jnp.split(c_act @ final_ada_w + final_ada_b, 2, axis=-1)
    return modulate(rmsnorm(x, final_norm_w), shift_f, scale_f)


import jax as _jax
import jax.numpy as _jnp

TWIN_FORMAT = 'train_step'
FWD_PARAMS = ['x', 'c', 'w_ada', 'b_ada', 'norm1_w', 'w_in', 'conv_dw_w', 'conv_dw_b', 'conv_ln_w', 'conv_ln_b', 'w_pw2', 'conv_out_norm_w', 'qkv_conv_w', 'a_log', 'dt_bias', 'dn_norm_w', 'w_out', 'norm2_w', 'w_up', 'w_down', 'final_ada_w', 'final_ada_b', 'final_norm_w']
TWIN_WEIGHTS = ['w_ada', 'b_ada', 'norm1_w', 'w_in', 'conv_dw_w', 'conv_dw_b', 'conv_ln_w', 'conv_ln_b', 'w_pw2', 'conv_out_norm_w', 'qkv_conv_w', 'a_log', 'dt_bias', 'dn_norm_w', 'w_out', 'norm2_w', 'w_up', 'w_down', 'final_ada_w', 'final_ada_b', 'final_norm_w']
TWIN_DIFF_INPUT = 'x'
TWIN_INPUTS = ['x', 'c', 'w_ada', 'b_ada', 'norm1_w', 'w_in', 'conv_dw_w', 'conv_dw_b', 'conv_ln_w', 'conv_ln_b', 'w_pw2', 'conv_out_norm_w', 'qkv_conv_w', 'a_log', 'dt_bias', 'dn_norm_w', 'w_out', 'norm2_w', 'w_up', 'w_down', 'final_ada_w', 'final_ada_b', 'final_norm_w', 'loss_target', 'm_w_ada', 'm_b_ada', 'm_norm1_w', 'm_w_in', 'm_conv_dw_w', 'm_conv_dw_b', 'm_conv_ln_w', 'm_conv_ln_b', 'm_w_pw2', 'm_conv_out_norm_w', 'm_qkv_conv_w', 'm_a_log', 'm_dt_bias', 'm_dn_norm_w', 'm_w_out', 'm_norm2_w', 'm_w_up', 'm_w_down', 'm_final_ada_w', 'm_final_ada_b', 'm_final_norm_w', 'v_w_ada', 'v_b_ada', 'v_norm1_w', 'v_w_in', 'v_conv_dw_w', 'v_conv_dw_b', 'v_conv_ln_w', 'v_conv_ln_b', 'v_w_pw2', 'v_conv_out_norm_w', 'v_qkv_conv_w', 'v_a_log', 'v_dt_bias', 'v_dn_norm_w', 'v_w_out', 'v_norm2_w', 'v_w_up', 'v_w_down', 'v_final_ada_w', 'v_final_ada_b', 'v_final_norm_w']
TWIN_OUTPUTS = ['loss', 'grad_x', 'grad_w_ada', 'grad_b_ada', 'grad_norm1_w', 'grad_w_in', 'grad_conv_dw_w', 'grad_conv_dw_b', 'grad_conv_ln_w', 'grad_conv_ln_b', 'grad_w_pw2', 'grad_conv_out_norm_w', 'grad_qkv_conv_w', 'grad_a_log', 'grad_dt_bias', 'grad_dn_norm_w', 'grad_w_out', 'grad_norm2_w', 'grad_w_up', 'grad_w_down', 'grad_final_ada_w', 'grad_final_ada_b', 'grad_final_norm_w', 'delta_w_ada', 'delta_b_ada', 'delta_norm1_w', 'delta_w_in', 'delta_conv_dw_w', 'delta_conv_dw_b', 'delta_conv_ln_w', 'delta_conv_ln_b', 'delta_w_pw2', 'delta_conv_out_norm_w', 'delta_qkv_conv_w', 'delta_a_log', 'delta_dt_bias', 'delta_dn_norm_w', 'delta_w_out', 'delta_norm2_w', 'delta_w_up', 'delta_w_down', 'delta_final_ada_w', 'delta_final_ada_b', 'delta_final_norm_w', 'new_m_w_ada', 'new_m_b_ada', 'new_m_norm1_w', 'new_m_w_in', 'new_m_conv_dw_w', 'new_m_conv_dw_b', 'new_m_conv_ln_w', 'new_m_conv_ln_b', 'new_m_w_pw2', 'new_m_conv_out_norm_w', 'new_m_qkv_conv_w', 'new_m_a_log', 'new_m_dt_bias', 'new_m_dn_norm_w', 'new_m_w_out', 'new_m_norm2_w', 'new_m_w_up', 'new_m_w_down', 'new_m_final_ada_w', 'new_m_final_ada_b', 'new_m_final_norm_w', 'new_v_w_ada', 'new_v_b_ada', 'new_v_norm1_w', 'new_v_w_in', 'new_v_conv_dw_w', 'new_v_conv_dw_b', 'new_v_conv_ln_w', 'new_v_conv_ln_b', 'new_v_w_pw2', 'new_v_conv_out_norm_w', 'new_v_qkv_conv_w', 'new_v_a_log', 'new_v_dt_bias', 'new_v_dn_norm_w', 'new_v_w_out', 'new_v_norm2_w', 'new_v_w_up', 'new_v_w_down', 'new_v_final_ada_w', 'new_v_final_ada_b', 'new_v_final_norm_w']
TWIN_LEAF_KINDS = {'loss': 'loss', 'grad_x': 'grad_x', 'grad_w_ada': 'grad_w', 'grad_b_ada': 'grad_w', 'grad_norm1_w': 'grad_w', 'grad_w_in': 'grad_w', 'grad_conv_dw_w': 'grad_w', 'grad_conv_dw_b': 'grad_w', 'grad_conv_ln_w': 'grad_w', 'grad_conv_ln_b': 'grad_w', 'grad_w_pw2': 'grad_w', 'grad_conv_out_norm_w': 'grad_w', 'grad_qkv_conv_w': 'grad_w', 'grad_a_log': 'grad_w', 'grad_dt_bias': 'grad_w', 'grad_dn_norm_w': 'grad_w', 'grad_w_out': 'grad_w', 'grad_norm2_w': 'grad_w', 'grad_w_up': 'grad_w', 'grad_w_down': 'grad_w', 'grad_final_ada_w': 'grad_w', 'grad_final_ada_b': 'grad_w', 'grad_final_norm_w': 'grad_w', 'delta_w_ada': 'delta_w', 'delta_b_ada': 'delta_w', 'delta_norm1_w': 'delta_w', 'delta_w_in': 'delta_w', 'delta_conv_dw_w': 'delta_w', 'delta_conv_dw_b': 'delta_w', 'delta_conv_ln_w': 'delta_w', 'delta_conv_ln_b': 'delta_w', 'delta_w_pw2': 'delta_w', 'delta_conv_out_norm_w': 'delta_w', 'delta_qkv_conv_w': 'delta_w', 'delta_a_log': 'delta_w', 'delta_dt_bias': 'delta_w', 'delta_dn_norm_w': 'delta_w', 'delta_w_out': 'delta_w', 'delta_norm2_w': 'delta_w', 'delta_w_up': 'delta_w', 'delta_w_down': 'delta_w', 'delta_final_ada_w': 'delta_w', 'delta_final_ada_b': 'delta_w', 'delta_final_norm_w': 'delta_w', 'new_m_w_ada': 'new_m', 'new_m_b_ada': 'new_m', 'new_m_norm1_w': 'new_m', 'new_m_w_in': 'new_m', 'new_m_conv_dw_w': 'new_m', 'new_m_conv_dw_b': 'new_m', 'new_m_conv_ln_w': 'new_m', 'new_m_conv_ln_b': 'new_m', 'new_m_w_pw2': 'new_m', 'new_m_conv_out_norm_w': 'new_m', 'new_m_qkv_conv_w': 'new_m', 'new_m_a_log': 'new_m', 'new_m_dt_bias': 'new_m', 'new_m_dn_norm_w': 'new_m', 'new_m_w_out': 'new_m', 'new_m_norm2_w': 'new_m', 'new_m_w_up': 'new_m', 'new_m_w_down': 'new_m', 'new_m_final_ada_w': 'new_m', 'new_m_final_ada_b': 'new_m', 'new_m_final_norm_w': 'new_m', 'new_v_w_ada': 'new_v', 'new_v_b_ada': 'new_v', 'new_v_norm1_w': 'new_v', 'new_v_w_in': 'new_v', 'new_v_conv_dw_w': 'new_v', 'new_v_conv_dw_b': 'new_v', 'new_v_conv_ln_w': 'new_v', 'new_v_conv_ln_b': 'new_v', 'new_v_w_pw2': 'new_v', 'new_v_conv_out_norm_w': 'new_v', 'new_v_qkv_conv_w': 'new_v', 'new_v_a_log': 'new_v', 'new_v_dt_bias': 'new_v', 'new_v_dn_norm_w': 'new_v', 'new_v_w_out': 'new_v', 'new_v_norm2_w': 'new_v', 'new_v_w_up': 'new_v', 'new_v_w_down': 'new_v', 'new_v_final_ada_w': 'new_v', 'new_v_final_ada_b': 'new_v', 'new_v_final_norm_w': 'new_v'}


def _forward(args):
    return _fwd_reference(*[args[k] for k in FWD_PARAMS])


def _output_shape():
    out = _jax.eval_shape(lambda: _forward(_fwd_setup_inputs(0)))
    return out.shape, out.dtype

N_MICROBATCH = 1
ADAM_LR = 0.001
ADAM_B1 = 0.9
ADAM_B2 = 0.999
ADAM_EPS = 1e-08
ADAM_WD = 0.01
ADAM_STEP = 10
PER_EXAMPLE_BATCH_AXIS = {'x': 0, 'c': 0, 'loss_target': 0}
SHARED_INPUTS = []
_WEIGHT_DTYPES = {'w_ada': _jnp.float32, 'b_ada': _jnp.float32, 'norm1_w': _jnp.float32, 'w_in': _jnp.float32, 'conv_dw_w': _jnp.float32, 'conv_dw_b': _jnp.float32, 'conv_ln_w': _jnp.float32, 'conv_ln_b': _jnp.float32, 'w_pw2': _jnp.float32, 'conv_out_norm_w': _jnp.float32, 'qkv_conv_w': _jnp.float32, 'a_log': _jnp.float32, 'dt_bias': _jnp.float32, 'dn_norm_w': _jnp.float32, 'w_out': _jnp.float32, 'norm2_w': _jnp.float32, 'w_up': _jnp.float32, 'w_down': _jnp.float32, 'final_ada_w': _jnp.float32, 'final_ada_b': _jnp.float32, 'final_norm_w': _jnp.float32}
MOMENT_SCALE = {'w_ada': 5.959875e-01, 'b_ada': 1.038948e+00, 'norm1_w': 7.259856e-02, 'w_in': 7.736422e-02, 'conv_dw_w': 1.787277e-01, 'conv_dw_b': 9.503217e-01, 'conv_ln_w': 4.494386e-01, 'conv_ln_b': 6.071418e-01, 'w_pw2': 2.965995e-01, 'conv_out_norm_w': 2.956992e-01, 'qkv_conv_w': 5.061403e-02, 'a_log': 1.697477e-01, 'dt_bias': 1.655165e-01, 'dn_norm_w': 3.757634e-01, 'w_out': 2.253245e-01, 'norm2_w': 1.202025e-01, 'w_up': 1.043597e-01, 'w_down': 4.560479e-01, 'final_ada_w': 4.478593e+00, 'final_ada_b': 1.193114e+01, 'final_norm_w': 1.841064e+01}


def _to_microbatches(a, axis):
    t = _jnp.moveaxis(a, axis, 0)
    t = t.reshape((N_MICROBATCH, t.shape[0] // N_MICROBATCH) + t.shape[1:])
    return _jnp.moveaxis(t, 1, axis + 1)


def setup_inputs(seed: int = 0) -> dict:
    inp = _fwd_setup_inputs(seed)
    key = _jax.random.fold_in(_jax.random.key(seed), 7919)
    shape, _ = _output_shape()
    out = dict(inp)
    out["loss_target"] = _jax.random.normal(_jax.random.fold_in(key, 0), shape, _jnp.float32)
    for i, name in enumerate(TWIN_WEIGHTS):
        w = inp[name].astype(_jnp.float32)
        if MOMENT_SCALE is None:
            s = _jnp.sqrt(_jnp.mean(_jnp.square(w)) + 1e-30)
        else:
            s = MOMENT_SCALE[name]
        km, kv = _jax.random.split(_jax.random.fold_in(key, i + 1))
        out[name] = w
        out["m_" + name] = s * _jax.random.normal(km, w.shape, _jnp.float32)
        out["v_" + name] = (s * s) * _jax.random.uniform(kv, w.shape, _jnp.float32, 0.5, 1.5)
    if N_MICROBATCH > 1:
        for name, axis in PER_EXAMPLE_BATCH_AXIS.items():
            out[name] = _to_microbatches(out[name], axis)
    return {'x': out['x'], 'c': out['c'], 'w_ada': out['w_ada'], 'b_ada': out['b_ada'], 'norm1_w': out['norm1_w'], 'w_in': out['w_in'], 'conv_dw_w': out['conv_dw_w'], 'conv_dw_b': out['conv_dw_b'], 'conv_ln_w': out['conv_ln_w'], 'conv_ln_b': out['conv_ln_b'], 'w_pw2': out['w_pw2'], 'conv_out_norm_w': out['conv_out_norm_w'], 'qkv_conv_w': out['qkv_conv_w'], 'a_log': out['a_log'], 'dt_bias': out['dt_bias'], 'dn_norm_w': out['dn_norm_w'], 'w_out': out['w_out'], 'norm2_w': out['norm2_w'], 'w_up': out['w_up'], 'w_down': out['w_down'], 'final_ada_w': out['final_ada_w'], 'final_ada_b': out['final_ada_b'], 'final_norm_w': out['final_norm_w'], 'loss_target': out['loss_target'], 'm_w_ada': out['m_w_ada'], 'm_b_ada': out['m_b_ada'], 'm_norm1_w': out['m_norm1_w'], 'm_w_in': out['m_w_in'], 'm_conv_dw_w': out['m_conv_dw_w'], 'm_conv_dw_b': out['m_conv_dw_b'], 'm_conv_ln_w': out['m_conv_ln_w'], 'm_conv_ln_b': out['m_conv_ln_b'], 'm_w_pw2': out['m_w_pw2'], 'm_conv_out_norm_w': out['m_conv_out_norm_w'], 'm_qkv_conv_w': out['m_qkv_conv_w'], 'm_a_log': out['m_a_log'], 'm_dt_bias': out['m_dt_bias'], 'm_dn_norm_w': out['m_dn_norm_w'], 'm_w_out': out['m_w_out'], 'm_norm2_w': out['m_norm2_w'], 'm_w_up': out['m_w_up'], 'm_w_down': out['m_w_down'], 'm_final_ada_w': out['m_final_ada_w'], 'm_final_ada_b': out['m_final_ada_b'], 'm_final_norm_w': out['m_final_norm_w'], 'v_w_ada': out['v_w_ada'], 'v_b_ada': out['v_b_ada'], 'v_norm1_w': out['v_norm1_w'], 'v_w_in': out['v_w_in'], 'v_conv_dw_w': out['v_conv_dw_w'], 'v_conv_dw_b': out['v_conv_dw_b'], 'v_conv_ln_w': out['v_conv_ln_w'], 'v_conv_ln_b': out['v_conv_ln_b'], 'v_w_pw2': out['v_w_pw2'], 'v_conv_out_norm_w': out['v_conv_out_norm_w'], 'v_qkv_conv_w': out['v_qkv_conv_w'], 'v_a_log': out['v_a_log'], 'v_dt_bias': out['v_dt_bias'], 'v_dn_norm_w': out['v_dn_norm_w'], 'v_w_out': out['v_w_out'], 'v_norm2_w': out['v_norm2_w'], 'v_w_up': out['v_w_up'], 'v_w_down': out['v_w_down'], 'v_final_ada_w': out['v_final_ada_w'], 'v_final_ada_b': out['v_final_ada_b'], 'v_final_norm_w': out['v_final_norm_w']}


def _loss(weights, diff, rest, loss_target):
    with _jax.named_scope("forward"):
        args = {**rest, TWIN_DIFF_INPUT: diff, **{k: w.astype(_WEIGHT_DTYPES[k]) for k, w in weights.items()}}
        y = _forward(args)
    with _jax.named_scope("loss_head"):
        err = _jnp.square(y.astype(_jnp.float32) - loss_target)
        return 0.5 * _jnp.sum(_jnp.mean(err, axis=-1)) if err.ndim else 0.5 * err


def _adamw(w, g, m, v):
    m = ADAM_B1 * m + (1.0 - ADAM_B1) * g
    v = ADAM_B2 * v + (1.0 - ADAM_B2) * _jnp.square(g)
    m_hat = m / (1.0 - ADAM_B1 ** ADAM_STEP)
    v_hat = v / (1.0 - ADAM_B2 ** ADAM_STEP)
    delta = -ADAM_LR * (m_hat / (_jnp.sqrt(v_hat) + ADAM_EPS) + ADAM_WD * w)
    return delta, m, v


def reference(x, c, w_ada, b_ada, norm1_w, w_in, conv_dw_w, conv_dw_b, conv_ln_w, conv_ln_b, w_pw2, conv_out_norm_w, qkv_conv_w, a_log, dt_bias, dn_norm_w, w_out, norm2_w, w_up, w_down, final_ada_w, final_ada_b, final_norm_w, loss_target, m_w_ada, m_b_ada, m_norm1_w, m_w_in, m_conv_dw_w, m_conv_dw_b, m_conv_ln_w, m_conv_ln_b, m_w_pw2, m_conv_out_norm_w, m_qkv_conv_w, m_a_log, m_dt_bias, m_dn_norm_w, m_w_out, m_norm2_w, m_w_up, m_w_down, m_final_ada_w, m_final_ada_b, m_final_norm_w, v_w_ada, v_b_ada, v_norm1_w, v_w_in, v_conv_dw_w, v_conv_dw_b, v_conv_ln_w, v_conv_ln_b, v_w_pw2, v_conv_out_norm_w, v_qkv_conv_w, v_a_log, v_dt_bias, v_dn_norm_w, v_w_out, v_norm2_w, v_w_up, v_w_down, v_final_ada_w, v_final_ada_b, v_final_norm_w):
    given = dict(x=x, c=c, w_ada=w_ada, b_ada=b_ada, norm1_w=norm1_w, w_in=w_in, conv_dw_w=conv_dw_w, conv_dw_b=conv_dw_b, conv_ln_w=conv_ln_w, conv_ln_b=conv_ln_b, w_pw2=w_pw2, conv_out_norm_w=conv_out_norm_w, qkv_conv_w=qkv_conv_w, a_log=a_log, dt_bias=dt_bias, dn_norm_w=dn_norm_w, w_out=w_out, norm2_w=norm2_w, w_up=w_up, w_down=w_down, final_ada_w=final_ada_w, final_ada_b=final_ada_b, final_norm_w=final_norm_w, loss_target=loss_target, m_w_ada=m_w_ada, m_b_ada=m_b_ada, m_norm1_w=m_norm1_w, m_w_in=m_w_in, m_conv_dw_w=m_conv_dw_w, m_conv_dw_b=m_conv_dw_b, m_conv_ln_w=m_conv_ln_w, m_conv_ln_b=m_conv_ln_b, m_w_pw2=m_w_pw2, m_conv_out_norm_w=m_conv_out_norm_w, m_qkv_conv_w=m_qkv_conv_w, m_a_log=m_a_log, m_dt_bias=m_dt_bias, m_dn_norm_w=m_dn_norm_w, m_w_out=m_w_out, m_norm2_w=m_norm2_w, m_w_up=m_w_up, m_w_down=m_w_down, m_final_ada_w=m_final_ada_w, m_final_ada_b=m_final_ada_b, m_final_norm_w=m_final_norm_w, v_w_ada=v_w_ada, v_b_ada=v_b_ada, v_norm1_w=v_norm1_w, v_w_in=v_w_in, v_conv_dw_w=v_conv_dw_w, v_conv_dw_b=v_conv_dw_b, v_conv_ln_w=v_conv_ln_w, v_conv_ln_b=v_conv_ln_b, v_w_pw2=v_w_pw2, v_conv_out_norm_w=v_conv_out_norm_w, v_qkv_conv_w=v_qkv_conv_w, v_a_log=v_a_log, v_dt_bias=v_dt_bias, v_dn_norm_w=v_dn_norm_w, v_w_out=v_w_out, v_norm2_w=v_norm2_w, v_w_up=v_w_up, v_w_down=v_w_down, v_final_ada_w=v_final_ada_w, v_final_ada_b=v_final_ada_b, v_final_norm_w=v_final_norm_w)
    weights = {n: given[n] for n in TWIN_WEIGHTS}
    shared = {n: given[n] for n in SHARED_INPUTS}
    per_example = {n: given[n] for n in ['x', 'c']}
    grad_fn = _jax.value_and_grad(_loss, argnums=(0, 1))

    def one_microbatch(ex, loss_target):
        ex = dict(ex)
        diff = ex.pop(TWIN_DIFF_INPUT)
        return grad_fn(weights, diff, {**shared, **ex}, loss_target)

    if N_MICROBATCH == 1:
        loss, (grad_w, grad_x) = one_microbatch(per_example, given["loss_target"])
    else:
        def body(carry, xs):
            loss_sum, grad_sum = carry
            l_k, (gw_k, gx_k) = one_microbatch(xs[0], xs[1])
            with _jax.named_scope("update"):
                return (loss_sum + l_k, _jax.tree.map(_jnp.add, grad_sum, gw_k)), gx_k

        init = (_jnp.zeros((), _jnp.float32), _jax.tree.map(_jnp.zeros_like, weights))
        (loss, grad_w), grad_x = _jax.lax.scan(body, init, (per_example, given["loss_target"]))
    with _jax.named_scope("update"):
        delta_w, new_m, new_v = {}, {}, {}
        for n in TWIN_WEIGHTS:
            delta_w[n], new_m[n], new_v[n] = _adamw(weights[n], grad_w[n], given["m_" + n], given["v_" + n])
    return (loss, grad_x, *[grad_w[n] for n in TWIN_WEIGHTS], *[delta_w[n] for n in TWIN_WEIGHTS],
            *[new_m[n] for n in TWIN_WEIGHTS], *[new_v[n] for n in TWIN_WEIGHTS])
```

```python
import functools
import math

import jax
import jax.numpy as jnp
from jax import lax
from jax.experimental import pallas as pl
from jax.experimental.pallas import tpu as pltpu

F32 = jnp.float32
MXU_DTYPE = jnp.bfloat16
HI = lax.Precision.HIGHEST
CHUNK = 64
EPS = 1e-6
LANE = 128
SUB = 8
PACK_LANES = 1024
VMEM_LIMIT = 56 * 1024 * 1024
ADAM_LR, ADAM_B1, ADAM_B2, ADAM_EPS, ADAM_WD, ADAM_STEP = 0.001, 0.9, 0.999, 1e-08, 0.01, 10
MESH = pl.DeviceIdType.MESH
ANY = pl.BlockSpec(memory_space=pl.ANY)
NN, NT, TN = ((1,), (0,)), ((1,), (1,)), ((0,), (0,))


def _pick(dim, pref, mult):
    for t in range(min(pref, dim), 0, -1):
        if dim % t == 0 and t % mult == 0:
            return t
    return dim


def _params(sem):
    return pltpu.CompilerParams(dimension_semantics=sem, vmem_limit_bytes=VMEM_LIMIT)


def _sigmoid(v):
    return 1.0 / (1.0 + jnp.exp(-v))


def _silu(v):
    return v * _sigmoid(v)


def _row_specs(rows, exps, gls, tm, tpe):
    specs = [pl.BlockSpec((tm, w), functools.partial(lambda i, j: (i, j), j=cb)) for _, w, cb in rows]
    specs += [pl.BlockSpec((1, 1, e.shape[-1]), lambda i: (i // tpe, 0, 0)) for e in exps]
    specs += [pl.BlockSpec((1, g.shape[-1]), lambda i: (0, 0)) for g in gls]
    return specs


def _rows_fwd(name, fn, rows, exps, gls, outs, *, tm, tpe=1, ngroups=1):
    T = rows[0][0].shape[0]
    nr, ne, ng = len(rows), len(exps), len(gls)

    def body(*refs):
        r, e, g, o = refs[:nr], refs[nr:nr + ne], refs[nr + ne:nr + ne + ng], refs[nr + ne + ng:]
        ev = [t[0].astype(F32) for t in e]
        gv = [t[...].astype(F32) for t in g]
        for k in range(ngroups):
            rv = [t[:, k * (w // ngroups):(k + 1) * (w // ngroups)].astype(F32) for t, (_, w, _) in zip(r, rows)]
            res = fn(*rv, *ev, *gv)
            for oref, val, (w, dt) in zip(o, res, outs):
                gw = w // ngroups
                oref[:, k * gw:(k + 1) * gw] = val.astype(dt)

    return pl.pallas_call(
        body, name=name, grid=(T // tm,),
        in_specs=_row_specs(rows, exps, gls, tm, tpe),
        out_specs=[pl.BlockSpec((tm, w), lambda i: (i, 0)) for w, _ in outs],
        out_shape=[jax.ShapeDtypeStruct((T, w), dt) for w, dt in outs],
        compiler_params=_params(("arbitrary",)),
    )(*[a for a, _, _ in rows], *exps, *gls)


def _rows_vjp(name, fn, rows, exps, gls, cts, row_dtypes, *, tm, tpe=1, ngroups=1, primal=None):
    T = rows[0][0].shape[0]
    nr, ne, ng = len(rows), len(exps), len(gls)
    nc = 0 if cts is None else len(cts)
    keep = [k for k, dt in enumerate(row_dtypes) if dt is not None]
    npr = 0 if primal is None else len(primal)

    def body(*refs):
        r, e, g = refs[:nr], refs[nr:nr + ne], refs[nr + ne:nr + ne + ng]
        c = refs[nr + ne + ng:nr + ne + ng + nc]
        o = refs[nr + ne + ng + nc:]
        po, ro, eo, go = o[:npr], o[npr:npr + len(keep)], o[npr + len(keep):npr + len(keep) + ne], o[npr + len(keep) + ne:]
        i = pl.program_id(0)
        ev = [t[0].astype(F32) for t in e]
        gv = [t[...].astype(F32) for t in g]
        esum = [jnp.zeros_like(v) for v in ev]
        gsum = [jnp.zeros_like(v) for v in gv]
        for k in range(ngroups):
            rv = [t[:, k * (w // ngroups):(k + 1) * (w // ngroups)].astype(F32) for t, (_, w, _) in zip(r, rows)]
            res, pull = jax.vjp(fn, *rv, *ev, *gv)
            if cts is None:
                ct = tuple(jnp.ones_like(v) for v in res)
            else:
                ct = tuple(t[:, k * (w // ngroups):(k + 1) * (w // ngroups)].astype(F32) for t, (_, w, _) in zip(c, cts))
            grads = pull(ct)
            for oref, val, (w, dt) in zip(po, res, primal or ()):
                gw = w // ngroups
                oref[:, k * gw:(k + 1) * gw] = val.astype(dt)
            for oref, idx in zip(ro, keep):
                gw = rows[idx][1] // ngroups
                oref[:, k * gw:(k + 1) * gw] = grads[idx].astype(row_dtypes[idx])
            esum = [s + d for s, d in zip(esum, grads[nr:nr + ne])]
            gsum = [s + d for s, d in zip(gsum, grads[nr + ne:])]

        if ne:
            @pl.when(i % tpe == 0)
            def _():
                for oref in eo:
                    oref[...] = jnp.zeros_like(oref)
            for oref, s in zip(eo, esum):
                oref[0] += s
        if ng:
            @pl.when(i == 0)
            def _():
                for oref in go:
                    oref[...] = jnp.zeros_like(oref)
            for oref, s in zip(go, gsum):
                oref[...] += s

    out_specs = [pl.BlockSpec((tm, w), lambda i: (i, 0)) for w, _ in (primal or ())]
    out_shape = [jax.ShapeDtypeStruct((T, w), dt) for w, dt in (primal or ())]
    out_specs += [pl.BlockSpec((tm, rows[k][1]), lambda i: (i, 0)) for k in keep]
    out_shape += [jax.ShapeDtypeStruct((T, rows[k][1]), row_dtypes[k]) for k in keep]
    out_specs += [pl.BlockSpec((1, 1, e.shape[-1]), lambda i: (i // tpe, 0, 0)) for e in exps]
    out_shape += [jax.ShapeDtypeStruct(e.shape, F32) for e in exps]
    out_specs += [pl.BlockSpec((1, g.shape[-1]), lambda i: (0, 0)) for g in gls]
    out_shape += [jax.ShapeDtypeStruct(g.shape, F32) for g in gls]
    ct_specs = [] if cts is None else [pl.BlockSpec((tm, w), functools.partial(lambda i, j: (i, j), j=cb)) for _, w, cb in cts]
    ct_arrs = [] if cts is None else [a for a, _, _ in cts]
    return pl.pallas_call(
        body, name=name, grid=(T // tm,),
        in_specs=_row_specs(rows, exps, gls, tm, tpe) + ct_specs,
        out_specs=out_specs, out_shape=out_shape,
        compiler_params=_params(("arbitrary",)),
    )(*[a for a, _, _ in rows], *exps, *gls, *ct_arrs)


def _whole(a):
    return (a, a.shape[-1], 0)


def _junction(x, y, gate, shift, scale, w):
    xn = x + gate * y
    r = lax.rsqrt(jnp.mean(xn * xn, axis=-1, keepdims=True) + EPS)
    return xn, (xn * r * w) * (1.0 + scale) + shift


def _final_loss(x, y, tgt, gate, shift, scale, w):
    _, out = _junction(x, y, gate, shift, scale, w)
    err = out - tgt
    return (0.5 * jnp.mean(err * err, axis=-1, keepdims=True),)


def _glu(val, gate):
    return (val * _sigmoid(gate),)


def _ln_silu(u, w, b):
    xc = u - jnp.mean(u, axis=-1, keepdims=True)
    y = xc * lax.rsqrt(jnp.mean(xc * xc, axis=-1, keepdims=True) + EPS) * w + b
    return (_silu(y),)


def _rms(u, w):
    return (u * lax.rsqrt(jnp.mean(u * u, axis=-1, keepdims=True) + EPS) * w,)


def _gated_rms(o, z, w):
    return (o * lax.rsqrt(jnp.mean(o * o, axis=-1, keepdims=True) + EPS) * w * _silu(z),)


def _relu2(u):
    r = jnp.maximum(u, 0.0)
    return (r * r,)


def _ident(u):
    return (u,)


def _silu_row(u):
    return (_silu(u),)


def _mm(name, a, b, mode, out_dtype=F32, bias=None, tm=1024, tn=1024, tk=512):
    if mode == "nn":
        (M, K), N = a.shape, b.shape[1]
    elif mode == "nt":
        (M, K), N = a.shape, b.shape[0]
    else:
        (K, M), N = a.shape, b.shape[1]
    tm, tn, tk = _pick(M, tm, LANE), _pick(N, tn, LANE), _pick(K, tk, LANE)
    nk = K // tk
    dn = {"nn": NN, "nt": NT, "tn": TN}[mode]
    a_spec = pl.BlockSpec((tk, tm), lambda i, j, k: (k, i)) if mode == "tn" else pl.BlockSpec((tm, tk), lambda i, j, k: (i, k))
    b_spec = pl.BlockSpec((tn, tk), lambda i, j, k: (j, k)) if mode == "nt" else pl.BlockSpec((tk, tn), lambda i, j, k: (k, j))
    specs, args = [a_spec, b_spec], [a, b]
    if bias is not None:
        specs.append(pl.BlockSpec((1, tn), lambda i, j, k: (0, j)))
        args.append(bias)

    def body(*refs):
        a_ref, b_ref = refs[0], refs[1]
        o_ref, acc = refs[-2], refs[-1]
        k = pl.program_id(2)

        @pl.when(k == 0)
        def _():
            acc[...] = jnp.zeros_like(acc)

        acc[...] += lax.dot_general(a_ref[...].astype(MXU_DTYPE), b_ref[...].astype(MXU_DTYPE), (dn, ((), ())),
                                    preferred_element_type=F32)

        @pl.when(k == nk - 1)
        def _():
            res = acc[...]
            if bias is not None:
                res = res + refs[2][...]
            o_ref[...] = res.astype(out_dtype)

    return pl.pallas_call(
        body, name=name, grid=(M // tm, N // tn, nk),
        in_specs=specs, out_specs=pl.BlockSpec((tm, tn), lambda i, j, k: (i, j)),
        out_shape=jax.ShapeDtypeStruct((M, N), out_dtype),
        scratch_shapes=[pltpu.VMEM((tm, tn), F32)],
        compiler_params=_params(("parallel", "parallel", "arbitrary")),
    )(*args)


def _halo(K):
    return SUB * -(-(K - 1) // SUB)


def _conv_fwd(name, x, col0, w, bias, *, ts, tpe):
    T = x.shape[0]
    K, C = w.shape
    H = _halo(K)
    cb = _pick(C, 256, LANE)
    off = col0 // cb
    specs = [pl.BlockSpec((ts, cb), lambda j, i: (i, off + j)),
             pl.BlockSpec((H, cb), lambda j, i: (jnp.maximum(i * (ts // H) - 1, 0), off + j)),
             pl.BlockSpec((K, cb), lambda j, i: (0, j))]
    args = [x, x, w]
    if bias is not None:
        specs.append(pl.BlockSpec((1, cb), lambda j, i: (0, j)))
        args.append(bias)

    def body(*refs):
        cur, halo, w_ref = refs[:3]
        o_ref, xp = refs[-2], refs[-1]
        i = pl.program_id(1)
        xp[0:H, :] = jnp.where(i % tpe == 0, 0.0, halo[...].astype(F32))
        xp[H:H + ts, :] = cur[...].astype(F32)
        acc = jnp.zeros((ts, cb), F32) if bias is None else jnp.zeros((ts, cb), F32) + refs[3][...]
        for j in range(K):
            acc = acc + w_ref[j:j + 1, :] * xp[H - (K - 1) + j:H - (K - 1) + j + ts, :]
        o_ref[...] = acc

    return pl.pallas_call(
        body, name=name, grid=(C // cb, T // ts), in_specs=specs,
        out_specs=pl.BlockSpec((ts, cb), lambda j, i: (i, j)),
        out_shape=jax.ShapeDtypeStruct((T, C), F32),
        scratch_shapes=[pltpu.VMEM((H + ts, cb), F32)],
        compiler_params=_params(("parallel", "arbitrary")),
    )(*args)


def _conv_bwd(name, x, col0, dy, w, out_dtype, *, ts, tpe):
    T = x.shape[0]
    K, C = w.shape
    H = _halo(K)
    cb = _pick(C, 256, LANE)
    off = col0 // cb
    nt = T // ts

    def body(cur, halo, dyc, dyn, w_ref, dx_ref, dw_ref, db_ref, xp, dyp):
        i = pl.program_id(1)
        xp[0:H, :] = jnp.where(i % tpe == 0, 0.0, halo[...].astype(F32))
        xp[H:H + ts, :] = cur[...].astype(F32)
        d = dyc[...]
        dyp[0:ts, :] = d
        dyp[ts:ts + H, :] = jnp.where(i % tpe == tpe - 1, 0.0, dyn[...])
        acc = jnp.zeros((ts, cb), F32)
        for j in range(K):
            acc = acc + w_ref[j:j + 1, :] * dyp[K - 1 - j:K - 1 - j + ts, :]
        dx_ref[...] = acc.astype(out_dtype)

        @pl.when(i == 0)
        def _():
            dw_ref[...] = jnp.zeros_like(dw_ref)
            db_ref[...] = jnp.zeros_like(db_ref)

        for j in range(K):
            dw_ref[j:j + 1, :] += jnp.sum(d * xp[H - (K - 1) + j:H - (K - 1) + j + ts, :], axis=0, keepdims=True)
        db_ref[...] += jnp.sum(d, axis=0, keepdims=True)

    return pl.pallas_call(
        body, name=name, grid=(C // cb, nt),
        in_specs=[pl.BlockSpec((ts, cb), lambda j, i: (i, off + j)),
                  pl.BlockSpec((H, cb), lambda j, i: (jnp.maximum(i * (ts // H) - 1, 0), off + j)),
                  pl.BlockSpec((ts, cb), lambda j, i: (i, j)),
                  pl.BlockSpec((H, cb), lambda j, i: (jnp.minimum((i + 1) * (ts // H), T // H - 1), j)),
                  pl.BlockSpec((K, cb), lambda j, i: (0, j))],
        out_specs=[pl.BlockSpec((ts, cb), lambda j, i: (i, j)),
                   pl.BlockSpec((K, cb), lambda j, i: (0, j)),
                   pl.BlockSpec((1, cb), lambda j, i: (0, j))],
        out_shape=[jax.ShapeDtypeStruct((T, C), out_dtype), jax.ShapeDtypeStruct((K, C), F32),
                   jax.ShapeDtypeStruct((1, C), F32)],
        scratch_shapes=[pltpu.VMEM((H + ts, cb), F32), pltpu.VMEM((ts + H, cb), F32)],
        compiler_params=_params(("parallel", "arbitrary")),
    )(x, x, dy, dy, w)


def _hdot(a, b, dn):
    return lax.dot_general(a, b, (dn, ((), ())), precision=HI, preferred_element_type=F32)


def _bdot(a, b, dn):
    return lax.dot_general(a.astype(MXU_DTYPE), b.astype(MXU_DTYPE), (dn, ((), ())), preferred_element_type=F32)


def _dn_prep(qc, kc, vc, ba, alog, dtb, *, h, nh):
    C, Dk = qc.shape
    lane = lax.broadcasted_iota(jnp.int32, (1, ba.shape[1]), 1)
    selb = (lane == h).astype(F32)
    sela = (lane == nh + h).astype(F32)
    bl = jnp.sum(ba * selb, axis=1, keepdims=True)
    al = jnp.sum(ba * sela, axis=1, keepdims=True)
    a_h = jnp.sum(alog * selb, axis=1, keepdims=True)
    d_h = jnp.sum(dtb * selb, axis=1, keepdims=True)
    q, k, v = _silu(qc), _silu(kc), _silu(vc)
    q = q * lax.rsqrt(jnp.sum(q * q, axis=-1, keepdims=True) + EPS) * (Dk ** -0.5)
    k = k * lax.rsqrt(jnp.sum(k * k, axis=-1, keepdims=True) + EPS)
    beta = _sigmoid(bl)
    z = al + d_h
    g = -jnp.exp(a_h) * (jnp.maximum(z, 0.0) + jnp.log(1.0 + jnp.exp(-jnp.abs(z))))
    gb = g * jnp.ones((1, Dk), F32)
    ri = lax.broadcasted_iota(jnp.int32, (C, C), 0)
    cj = lax.broadcasted_iota(jnp.int32, (C, C), 1)
    causal, strict = ri >= cj, ri > cj
    gc = _hdot(causal.astype(F32), gb, NN)
    gl = _hdot(jnp.ones((C, C), F32), gb, NN)
    e0 = (lax.broadcasted_iota(jnp.int32, (C, Dk), 1) == 0).astype(F32)
    diff = _hdot(gc, e0, NT) - _hdot(e0, gc, NT)
    decay = jnp.where(causal, jnp.exp(jnp.where(causal, diff, 0.0)), 0.0)
    kb, vb = k * beta, v * beta
    n = -jnp.where(strict, _hdot(kb, k, NT) * decay, 0.0)
    tm = (ri == cj).astype(F32) + n
    p = n
    for _ in range(int(math.log2(C)) - 1):
        p = _hdot(p, p, NN)
        tm = tm + _hdot(tm, p, NN)
    egc = jnp.exp(gc)
    u = _hdot(tm, vb, NN)
    w = _hdot(tm, kb * egc, NN)
    attn = jnp.where(causal, _hdot(q, k, NT) * decay, 0.0)
    return u, w, q * egc, k * jnp.exp(gl - gc), attn, jnp.exp(gl)


def _dn_prep_specs(C, DN, NH, col_q, wp_cols):
    qs = [pl.BlockSpec((C, DN), functools.partial(lambda i, j: (i, j), j=j)) for j in range(3)]
    return qs + [pl.BlockSpec((C, LANE), lambda i: (i, col_q)), pl.BlockSpec((1, LANE), lambda i: (0, 0)),
                 pl.BlockSpec((1, LANE), lambda i: (0, 0))]


def _dn_prep_fwd(name, qkv, proj, ba_col, alog, dtb, *, NH):
    T = qkv.shape[0]
    DN = qkv.shape[1] // 3
    Dk = DN // NH
    C = CHUNK

    def body(q_ref, k_ref, v_ref, ba_ref, al_ref, dt_ref, u_ref, w_ref, qd_ref, kd_ref, gl_ref, at_ref):
        for h in range(NH):
            sl = slice(h * Dk, (h + 1) * Dk)
            res = _dn_prep(q_ref[:, sl], k_ref[:, sl], v_ref[:, sl], ba_ref[...], al_ref[...], dt_ref[...], h=h, nh=NH)
            u_ref[:, sl], w_ref[:, sl], qd_ref[:, sl], kd_ref[:, sl] = res[0], res[1], res[2], res[3]
            at_ref[h] = res[4]
            gl_ref[:, sl] = res[5]

    big = pl.BlockSpec((C, DN), lambda i: (i, 0))
    return pl.pallas_call(
        body, name=name, grid=(T // C,),
        in_specs=_dn_prep_specs(C, DN, NH, ba_col // LANE, proj.shape[1]),
        out_specs=[big] * 5 + [pl.BlockSpec((NH, C, C), lambda i: (0, i, 0))],
        out_shape=[jax.ShapeDtypeStruct((T, DN), F32)] * 5 + [jax.ShapeDtypeStruct((NH, T, C), F32)],
        compiler_params=_params(("arbitrary",)),
    )(qkv, qkv, qkv, proj, alog, dtb)


def _dn_prep_bwd(name, qkv, proj, ba_col, alog, dtb, cts, out_dtype, *, NH):
    T = qkv.shape[0]
    DN = qkv.shape[1] // 3
    Dk = DN // NH
    C = CHUNK

    def body(q_ref, k_ref, v_ref, ba_ref, al_ref, dt_ref, du, dw, dqd, dkd, dgl, dat,
             dq_ref, dk_ref, dv_ref, dba_ref, dal_ref, ddt_ref):
        i = pl.program_id(0)
        dba = jnp.zeros((C, LANE), F32)
        dal = jnp.zeros((1, LANE), F32)
        ddt = jnp.zeros((1, LANE), F32)
        for h in range(NH):
            sl = slice(h * Dk, (h + 1) * Dk)
            _, pull = jax.vjp(functools.partial(_dn_prep, h=h, nh=NH), q_ref[:, sl], k_ref[:, sl], v_ref[:, sl],
                              ba_ref[...], al_ref[...], dt_ref[...])
            gq, gk, gv, gba, gal, gdt = pull((du[:, sl], dw[:, sl], dqd[:, sl], dkd[:, sl], dat[h], dgl[:, sl]))
            dq_ref[:, sl], dk_ref[:, sl], dv_ref[:, sl] = gq.astype(out_dtype), gk.astype(out_dtype), gv.astype(out_dtype)
            dba, dal, ddt = dba + gba, dal + gal, ddt + gdt
        dba_ref[...] = dba.astype(out_dtype)

        @pl.when(i == 0)
        def _():
            dal_ref[...] = jnp.zeros_like(dal_ref)
            ddt_ref[...] = jnp.zeros_like(ddt_ref)

        dal_ref[...] += dal
        ddt_ref[...] += ddt

    big = pl.BlockSpec((C, DN), lambda i: (i, 0))
    row = pl.BlockSpec((1, LANE), lambda i: (0, 0))
    return pl.pallas_call(
        body, name=name, grid=(T // C,),
        in_specs=_dn_prep_specs(C, DN, NH, ba_col // LANE, proj.shape[1]) + [big] * 5
        + [pl.BlockSpec((NH, C, C), lambda i: (0, i, 0))],
        out_specs=[big] * 3 + [pl.BlockSpec((C, LANE), lambda i: (i, 0)), row, row],
        out_shape=[jax.ShapeDtypeStruct((T, DN), out_dtype)] * 3 + [jax.ShapeDtypeStruct((T, LANE), out_dtype),
                                                                     jax.ShapeDtypeStruct((1, LANE), F32),
                                                                     jax.ShapeDtypeStruct((1, LANE), F32)],
        compiler_params=_params(("arbitrary",)),
    )(qkv, qkv, qkv, proj, alog, dtb, *cts)


def _dn_scan_fwd(name, u, w, qd, kd, gl, attn, *, NH, B):
    T, DN = u.shape
    Dk = DN // NH
    C = CHUNK
    NC = T // (B * C)

    def body(u_ref, w_ref, qd_ref, kd_ref, gl_ref, at_ref, o_ref, st_ref, s_ref):
        @pl.when(pl.program_id(1) == 0)
        def _():
            s_ref[...] = jnp.zeros_like(s_ref)

        for h in range(NH):
            sl = slice(h * Dk, (h + 1) * Dk)
            s = s_ref[h]
            st_ref[0, h] = s
            vnew = u_ref[:, sl] - _bdot(w_ref[:, sl], s, NN)
            o_ref[:, sl] = _bdot(qd_ref[:, sl], s, NN) + _bdot(at_ref[h], vnew, NN)
            s_ref[h] = s * gl_ref[0:1, sl] + _bdot(kd_ref[:, sl], vnew, TN)

    big = pl.BlockSpec((C, DN), lambda b, n: (b * NC + n, 0))
    return pl.pallas_call(
        body, name=name, grid=(B, NC),
        in_specs=[big] * 5 + [pl.BlockSpec((NH, C, C), lambda b, n: (0, b * NC + n, 0))],
        out_specs=[big, pl.BlockSpec((1, NH, Dk, Dk), lambda b, n: (b * NC + n, 0, 0, 0))],
        out_shape=[jax.ShapeDtypeStruct((T, DN), F32), jax.ShapeDtypeStruct((T // C, NH, Dk, Dk), F32)],
        scratch_shapes=[pltpu.VMEM((NH, Dk, Dk), F32)],
        compiler_params=_params(("arbitrary", "arbitrary")),
    )(u, w, qd, kd, gl, attn)


def _dn_scan_bwd(name, do, u, w, qd, kd, gl, attn, st, *, NH, B):
    T, DN = u.shape
    Dk = DN // NH
    C = CHUNK
    NC = T // (B * C)

    def body(do_ref, u_ref, w_ref, qd_ref, kd_ref, gl_ref, at_ref, st_ref,
             du_ref, dw_ref, dqd_ref, dkd_ref, dgl_ref, dat_ref, ds_ref):
        @pl.when(pl.program_id(1) == 0)
        def _():
            ds_ref[...] = jnp.zeros_like(ds_ref)

        row0 = lax.broadcasted_iota(jnp.int32, (C, Dk), 0) == 0
        for h in range(NH):
            sl = slice(h * Dk, (h + 1) * Dk)
            s, ds, g = st_ref[0, h], ds_ref[h], do_ref[:, sl]
            wv, at, kdv = w_ref[:, sl], at_ref[h], kd_ref[:, sl]
            vnew = u_ref[:, sl] - _bdot(wv, s, NN)
            dvnew = _bdot(at, g, TN) + _bdot(kdv, ds, NN)
            dat_ref[h] = _bdot(g, vnew, NT)
            dqd_ref[:, sl] = _bdot(g, s, NT)
            dkd_ref[:, sl] = _bdot(vnew, ds, NT)
            dgl_ref[:, sl] = jnp.where(row0, jnp.sum(s * ds, axis=0, keepdims=True), 0.0)
            du_ref[:, sl] = dvnew
            dw_ref[:, sl] = -_bdot(dvnew, s, NT)
            ds_ref[h] = _bdot(qd_ref[:, sl], g, TN) + ds * gl_ref[0:1, sl] - _bdot(wv, dvnew, TN)

    big = pl.BlockSpec((C, DN), lambda b, n: (b * NC + NC - 1 - n, 0))
    att = pl.BlockSpec((NH, C, C), lambda b, n: (0, b * NC + NC - 1 - n, 0))
    return pl.pallas_call(
        body, name=name, grid=(B, NC),
        in_specs=[big] * 6 + [att, pl.BlockSpec((1, NH, Dk, Dk), lambda b, n: (b * NC + NC - 1 - n, 0, 0, 0))],
        out_specs=[big] * 5 + [att],
        out_shape=[jax.ShapeDtypeStruct((T, DN), F32)] * 5 + [jax.ShapeDtypeStruct((NH, T, C), F32)],
        scratch_shapes=[pltpu.VMEM((NH, Dk, Dk), F32)],
        compiler_params=_params(("arbitrary", "arbitrary")),
    )(do, u, w, qd, kd, gl, attn, st)


def _sum0(name, a):
    n, r, ln = a.shape
    tr = _pick(r, 64, SUB)

    def body(a_ref, o_ref):
        acc = a_ref[0]
        for k in range(1, n):
            acc = acc + a_ref[k]
        o_ref[...] = acc

    return pl.pallas_call(
        body, name=name, grid=(r // tr,),
        in_specs=[pl.BlockSpec((n, tr, ln), lambda i: (0, i, 0))],
        out_specs=pl.BlockSpec((tr, ln), lambda i: (i, 0)),
        out_shape=jax.ShapeDtypeStruct((r, ln), F32),
        compiler_params=_params(("arbitrary",)),
    )(a)


def _adamw(name, w, g, m, v):
    R, Cc = w.shape
    tr = _pick(R, max(SUB, (1 << 18) // Cc // SUB * SUB), SUB)
    c1 = 1.0 - ADAM_B1 ** ADAM_STEP
    c2 = 1.0 - ADAM_B2 ** ADAM_STEP

    def body(w_ref, g_ref, m_ref, v_ref, d_ref, mo_ref, vo_ref):
        gv = g_ref[...]
        mn = ADAM_B1 * m_ref[...] + (1.0 - ADAM_B1) * gv
        vn = ADAM_B2 * v_ref[...] + (1.0 - ADAM_B2) * (gv * gv)
        mo_ref[...] = mn
        vo_ref[...] = vn
        d_ref[...] = -ADAM_LR * ((mn / c1) / (jnp.sqrt(vn / c2) + ADAM_EPS) + ADAM_WD * w_ref[...])

    spec = pl.BlockSpec((tr, Cc), lambda i: (i, 0))
    return pl.pallas_call(
        body, name=name, grid=(R // tr,), in_specs=[spec] * 4, out_specs=[spec] * 3,
        out_shape=[jax.ShapeDtypeStruct((R, Cc), F32)] * 3,
        compiler_params=_params(("arbitrary",)),
    )(w, g, m, v)


def _pack(arrs):
    flat = jnp.concatenate([a.reshape(-1).astype(F32) for a in arrs])
    pad = (-flat.shape[0]) % (SUB * PACK_LANES)
    return jnp.pad(flat, (0, pad)).reshape(-1, PACK_LANES)


def _unpack(flat, shapes):
    out, pos = [], 0
    for shp in shapes:
        n = math.prod(shp)
        out.append(flat[..., pos:pos + n].reshape(flat.shape[:-1] + tuple(shp)))
        pos += n
    return out


def _remote(src, dst, ssem, rsem, dev):
    return pltpu.make_async_remote_copy(src_ref=src, dst_ref=dst, send_sem=ssem, recv_sem=rsem, device_id=dev,
                                        device_id_type=MESH)


def _place():
    return lax.axis_index("x"), lax.axis_index("y"), lax.axis_index("c")


def _all_gather8(name, a):
    m, n = a.shape

    def body(x_ref, out_ref, send_sems, recv_sems, local_sem):
        x, y, c = _place()
        me, sibling = (x, y, c), (x, y, 1 - c)
        chips = [(1 - x, y), (x, 1 - y), (1 - x, 1 - y)]

        def rows(px, py, pc):
            return out_ref.at[pl.ds((4 * px + 2 * py + pc) * m, m), :]

        def copy(k, block, to, src=None):
            return _remote(rows(*block) if src is None else src, rows(*block), send_sems.at[k], recv_sems.at[k], to)

        mine = pltpu.make_async_copy(x_ref, rows(*me), local_sem)
        mine.start()
        first = [copy(0, me, sibling, src=x_ref)]
        first += [copy(1 + j, me, (*chip, c), src=x_ref) for j, chip in enumerate(chips)]
        for cp in first:
            cp.start()
        passed = [copy(4 + j, (*chip, c), sibling) for j, chip in enumerate(chips)]
        for j, chip in enumerate(chips):
            copy(1 + j, (*chip, c), me).wait_recv()
            passed[j].start()
        copy(0, sibling, me).wait_recv()
        for j, chip in enumerate(chips):
            copy(4 + j, (*chip, 1 - c), me).wait_recv()
        for cp in first + passed:
            cp.wait_send()
        mine.wait()

    return pl.pallas_call(
        body, name=name,
        out_shape=jax.ShapeDtypeStruct((8 * m, n), a.dtype),
        in_specs=[pl.BlockSpec(memory_space=pltpu.VMEM)],
        out_specs=pl.BlockSpec(memory_space=pltpu.VMEM),
        scratch_shapes=[pltpu.SemaphoreType.DMA((7,)), pltpu.SemaphoreType.DMA((7,)), pltpu.SemaphoreType.DMA],
        compiler_params=pltpu.CompilerParams(vmem_limit_bytes=VMEM_LIMIT),
    )(a)


def _chip_peers(x, y):
    return [(1 - x, y), (x, 1 - y), (1 - x, 1 - y)]


def _gather_chips(name, shards):
    n = len(shards)

    def body(*refs):
        ins, outs = refs[:n], refs[n:2 * n]
        ssem, rsem, lsem = refs[2 * n:]
        x, y, c = _place()
        sends, recvs, locs = [], [], []
        for a in range(n):
            loc = pltpu.make_async_copy(ins[a], outs[a].at[2 * x + y], lsem.at[a])
            loc.start()
            locs.append(loc)
            for k, (px, py) in enumerate(_chip_peers(x, y)):
                s = _remote(ins[a], outs[a].at[2 * x + y], ssem.at[3 * a + k], rsem.at[3 * a + k], (px, py, c))
                s.start()
                sends.append(s)
                recvs.append(_remote(ins[a], outs[a].at[2 * px + py], ssem.at[3 * a + k], rsem.at[3 * a + k], (px, py, c)))
        for s, r in zip(sends, recvs):
            s.wait_send()
            r.wait_recv()
        for loc in locs:
            loc.wait()

    return pl.pallas_call(
        body, name=name, in_specs=[ANY] * n, out_specs=[ANY] * n,
        out_shape=[jax.ShapeDtypeStruct((4,) + s.shape, s.dtype) for s in shards],
        scratch_shapes=[pltpu.SemaphoreType.DMA((3 * n,)), pltpu.SemaphoreType.DMA((3 * n,)), pltpu.SemaphoreType.DMA((n,))],
    )(*shards)


def _swap_halves(name, grads):
    n = len(grads)

    def body(*refs):
        ins, outs = refs[:n], refs[n:2 * n]
        ssem, rsem = refs[2 * n:]
        x, y, c = _place()
        cps = [_remote(ins[a].at[:, 1 - c], outs[a], ssem.at[a], rsem.at[a], (x, y, 1 - c)) for a in range(n)]
        for cp in cps:
            cp.start()
        for cp in cps:
            cp.wait_send()
            cp.wait_recv()

    return pl.pallas_call(
        body, name=name, in_specs=[ANY] * n, out_specs=[ANY] * n,
        out_shape=[jax.ShapeDtypeStruct((4,) + g.shape[2:], g.dtype) for g in grads],
        scratch_shapes=[pltpu.SemaphoreType.DMA((n,)), pltpu.SemaphoreType.DMA((n,))],
    )(*grads)


def _scatter_chips(name, halves):
    n = len(halves)

    def body(*refs):
        ins, outs = refs[:n], refs[n:2 * n]
        ssem, rsem = refs[2 * n:]
        x, y, c = _place()
        cps = []
        for a in range(n):
            for k, (px, py) in enumerate(_chip_peers(x, y)):
                cps.append(_remote(ins[a].at[2 * px + py], outs[a].at[k], ssem.at[3 * a + k], rsem.at[3 * a + k], (px, py, c)))
        for cp in cps:
            cp.start()
        for cp in cps:
            cp.wait_send()
            cp.wait_recv()

    return pl.pallas_call(
        body, name=name, in_specs=[ANY] * n, out_specs=[ANY] * n,
        out_shape=[jax.ShapeDtypeStruct((3,) + h.shape[1:], h.dtype) for h in halves],
        scratch_shapes=[pltpu.SemaphoreType.DMA((3 * n,)), pltpu.SemaphoreType.DMA((3 * n,))],
    )(*halves)


def _join_halves(name, parts):
    n = len(parts)

    def body(*refs):
        ins, outs = refs[:n], refs[n:2 * n]
        ssem, rsem, lsem = refs[2 * n:]
        x, y, c = _place()
        locs = [pltpu.make_async_copy(ins[a], outs[a].at[c], lsem.at[a]) for a in range(n)]
        sends = [_remote(ins[a], outs[a].at[c], ssem.at[a], rsem.at[a], (x, y, 1 - c)) for a in range(n)]
        recvs = [_remote(ins[a], outs[a].at[1 - c], ssem.at[a], rsem.at[a], (x, y, 1 - c)) for a in range(n)]
        for cp in locs + sends:
            cp.start()
        for s, r in zip(sends, recvs):
            s.wait_send()
            r.wait_recv()
        for cp in locs:
            cp.wait()

    return pl.pallas_call(
        body, name=name, in_specs=[ANY] * n, out_specs=[ANY] * n,
        out_shape=[jax.ShapeDtypeStruct((2,) + p.shape, p.dtype) for p in parts],
        scratch_shapes=[pltpu.SemaphoreType.DMA((n,)), pltpu.SemaphoreType.DMA((n,)), pltpu.SemaphoreType.DMA((n,))],
    )(*parts)


def _add_half(name, idx, g, r):
    _, _, Rh, Cc = g.shape
    tr = _pick(Rh, 128, SUB)

    def body(i_ref, g_ref, r_ref, o_ref):
        o_ref[...] = g_ref[0] + r_ref[...]

    return pl.pallas_call(
        body, name=name,
        grid_spec=pltpu.PrefetchScalarGridSpec(
            num_scalar_prefetch=1, grid=(4, Rh // tr),
            in_specs=[pl.BlockSpec((1, 1, tr, Cc), lambda s, i, ix: (s, ix[0], i, 0)),
                      pl.BlockSpec((1, tr, Cc), lambda s, i, ix: (s, i, 0))],
            out_specs=pl.BlockSpec((1, tr, Cc), lambda s, i, ix: (s, i, 0))),
        out_shape=jax.ShapeDtypeStruct((4, Rh, Cc), F32),
        compiler_params=_params(("arbitrary", "arbitrary")),
    )(idx, g, r)


def _add_chips(name, idx, h, r):
    _, Rh, Cc = h.shape
    tr = _pick(Rh, 128, SUB)

    def body(i_ref, h_ref, r0, r1, r2, o_ref):
        o_ref[...] = ((h_ref[0] + r0[0]) + r1[0]) + r2[0]

    return pl.pallas_call(
        body, name=name,
        grid_spec=pltpu.PrefetchScalarGridSpec(
            num_scalar_prefetch=1, grid=(Rh // tr,),
            in_specs=[pl.BlockSpec((1, tr, Cc), lambda i, ix: (ix[0], i, 0))]
            + [pl.BlockSpec((1, tr, Cc), functools.partial(lambda i, ix, k: (k, i, 0), k=k)) for k in range(3)],
            out_specs=pl.BlockSpec((tr, Cc), lambda i, ix: (i, 0))),
        out_shape=jax.ShapeDtypeStruct((Rh, Cc), F32),
        compiler_params=_params(("arbitrary",)),
    )(idx, h, r, r, r)


def _reduce_scatter(tag, grads, cidx, sidx):
    g4 = [g.reshape(4, 2, g.shape[1] // 2, g.shape[2]) for g in grads]
    got = _swap_halves(f"rs_swap_{tag}", g4)
    halves = [_add_half(f"rs_add_half_{tag}_{a}", cidx, g, r) for a, (g, r) in enumerate(zip(g4, got))]
    got = _scatter_chips(f"rs_scatter_{tag}", halves)
    parts = [_add_chips(f"rs_add_chips_{tag}_{a}", sidx, h, r) for a, (h, r) in enumerate(zip(halves, got))]
    full = _join_halves(f"rs_join_{tag}", parts)
    return [f.reshape(g.shape[1], g.shape[2]) for f, g in zip(full, grads)]


def kernel(x, c, w_ada, b_ada, norm1_w, w_in, conv_dw_w, conv_dw_b, conv_ln_w, conv_ln_b, w_pw2, conv_out_norm_w, qkv_conv_w, a_log, dt_bias, dn_norm_w, w_out, norm2_w, w_up, w_down, final_ada_w, final_ada_b, final_norm_w, loss_target, m_w_ada, m_b_ada, m_norm1_w, m_w_in, m_conv_dw_w, m_conv_dw_b, m_conv_ln_w, m_conv_ln_b, m_w_pw2, m_conv_out_norm_w, m_qkv_conv_w, m_a_log, m_dt_bias, m_dn_norm_w, m_w_out, m_norm2_w, m_w_up, m_w_down, m_final_ada_w, m_final_ada_b, m_final_norm_w, v_w_ada, v_b_ada, v_norm1_w, v_w_in, v_conv_dw_w, v_conv_dw_b, v_conv_ln_w, v_conv_ln_b, v_w_pw2, v_conv_out_norm_w, v_qkv_conv_w, v_a_log, v_dt_bias, v_dn_norm_w, v_w_out, v_norm2_w, v_w_up, v_w_down, v_final_ada_w, v_final_ada_b, v_final_norm_w):
    names = ["w_ada", "b_ada", "norm1_w", "w_in", "conv_dw_w", "conv_dw_b", "conv_ln_w", "conv_ln_b", "w_pw2",
             "conv_out_norm_w", "qkv_conv_w", "a_log", "dt_bias", "dn_norm_w", "w_out", "norm2_w", "w_up", "w_down",
             "final_ada_w", "final_ada_b", "final_norm_w"]
    weights = dict(zip(names, [w_ada, b_ada, norm1_w, w_in, conv_dw_w, conv_dw_b, conv_ln_w, conv_ln_b, w_pw2,
                               conv_out_norm_w, qkv_conv_w, a_log, dt_bias, dn_norm_w, w_out, norm2_w, w_up, w_down,
                               final_ada_w, final_ada_b, final_norm_w]))
    mom1 = dict(zip(names, [m_w_ada, m_b_ada, m_norm1_w, m_w_in, m_conv_dw_w, m_conv_dw_b, m_conv_ln_w, m_conv_ln_b,
                            m_w_pw2, m_conv_out_norm_w, m_qkv_conv_w, m_a_log, m_dt_bias, m_dn_norm_w, m_w_out,
                            m_norm2_w, m_w_up, m_w_down, m_final_ada_w, m_final_ada_b, m_final_norm_w]))
    mom2 = dict(zip(names, [v_w_ada, v_b_ada, v_norm1_w, v_w_in, v_conv_dw_w, v_conv_dw_b, v_conv_ln_w, v_conv_ln_b,
                            v_w_pw2, v_conv_out_norm_w, v_qkv_conv_w, v_a_log, v_dt_bias, v_dn_norm_w, v_w_out,
                            v_norm2_w, v_w_up, v_w_down, v_final_ada_w, v_final_ada_b, v_final_norm_w]))

    B, S, D = x.shape
    T = B * S
    L = w_in.shape[0]
    C1 = conv_ln_w.shape[-1]
    NH, DH = a_log.shape[-1], dn_norm_w.shape[-1]
    DN = NH * DH
    FF = w_down.shape[1] * 4
    IN = w_in.shape[-1] * 4
    INP = 6 * C1 + LANE
    KC, KQ = conv_dw_w.shape[1], qkv_conv_w.shape[1]
    NW, NF = w_ada.shape[-1], final_ada_w.shape[-1]
    assert DH == LANE and DN == C1 and IN == 6 * C1 + 2 * NH and S % CHUNK == 0
    xi, yi, ci = _place()
    s_me, me = 2 * xi + yi, 4 * xi + 2 * yi + ci
    cidx = jnp.reshape(ci, (1,)).astype(jnp.int32)
    sidx = jnp.reshape(s_me, (1,)).astype(jnp.int32)
    tmf, tmb = _pick(S, 256, SUB), _pick(S, 128, SUB)
    ts = _pick(S, 128, 32)

    shapes1 = [(B, D), conv_dw_w.shape, qkv_conv_w.shape]
    g1 = _pack([c, conv_dw_w, qkv_conv_w])
    g1 = _all_gather8("gather_cond", g1).reshape(8, -1)
    c_all, cw_all, qw_all = _unpack(g1, shapes1)
    c_all = c_all.reshape(8 * B, D)
    conv_w_full = jnp.moveaxis(cw_all[0::2], 0, 2).reshape(L, KC, C1)
    qkv_w_full = jnp.moveaxis(qw_all[0::2], 0, 2).reshape(L, KQ, 3 * DN)
    (c_act,) = _rows_fwd("cond_silu", _silu_row, [_whole(c_all)], [], [], [(D, F32)], tm=8 * B)

    mods = []
    for l in range(L):
        bsh = lax.dynamic_slice(b_ada[l], (s_me * NW,), (NW,)).reshape(1, NW)
        mods.append(_mm(f"mod_{l}", c_act, w_ada[l], "nn", bias=bsh))
    bsh = lax.dynamic_slice(final_ada_b, (s_me * NF,), (NF,)).reshape(1, NF)
    mods.append(_mm("mod_final", c_act, final_ada_w, "nn", bias=bsh))
    shapes2 = [(8 * B, NW)] * L + [(8 * B, NF)]
    g2 = _all_gather8("gather_mod", _pack(mods)).reshape(8, -1)[0::2]
    mod_all = [jnp.moveaxis(t, 0, 1).reshape(8 * B, -1) for t in _unpack(g2, shapes2)]
    mod_me = [lax.dynamic_slice(t, (B * me, 0), (B, t.shape[1])) for t in mod_all]

    def split_mod(t, n):
        return [t[:, k * D:(k + 1) * D].reshape(B, 1, D) for k in range(n)]

    wfull = []
    for l in range(L):
        sh = [w_in[l], w_pw2[l], w_out[l], w_up[l], w_down[l]]
        cast = [_rows_fwd(f"cast_{l}_{k}", _ident, [_whole(a)], [], [], [(a.shape[1], MXU_DTYPE)],
                          tm=_pick(a.shape[0], 256, 16))[0] for k, a in enumerate(sh)]
        gi, gp, go, gu, gd = _gather_chips(f"gather_w_{l}", cast)
        wi = jnp.pad(jnp.moveaxis(gi, 0, 1).reshape(D, IN), ((0, 0), (0, INP - IN)))
        wfull.append(dict(w_in=wi, w_pw2=gp.reshape(C1, C1), w_out=go.reshape(D, D),
                          w_up=jnp.moveaxis(gu, 0, 1).reshape(D, FF), w_down=gd.reshape(FF, D)))

    pad_row = lambda v: jnp.pad(v.reshape(1, -1), ((0, 0), (0, LANE - v.shape[-1])))
    row = lambda v: v.reshape(1, -1)

    saved = []
    xcur = x.reshape(T, D)
    ycur = jnp.zeros((T, D), F32)
    gate_prev = jnp.zeros((B, 1, D), F32)
    for l in range(L):
        W = wfull[l]
        sh1, sc1, g1_, sh2, sc2, g2_ = split_mod(mod_me[l], 6)
        sv = dict(x_in=xcur, y_in=ycur, gate_in=gate_prev, sh1=sh1, sc1=sc1, g1=g1_, sh2=sh2, sc2=sc2, g2=g2_)
        x0, h1 = _rows_fwd(f"junction1_{l}", _junction, [_whole(xcur), _whole(ycur)], [gate_prev, sh1, sc1],
                           [row(norm1_w[l])], [(D, F32), (D, MXU_DTYPE)], tm=tmf, tpe=S // tmf)
        proj = _mm(f"proj_{l}", h1, W["w_in"], "nn", tn=896)
        (u0,) = _rows_fwd(f"glu_{l}", _glu, [(proj, C1, 0), (proj, C1, 1)], [], [], [(C1, F32)], tm=tmf)
        u1 = _conv_fwd(f"conv_{l}", u0, 0, conv_w_full[l], row(conv_dw_b[l]), ts=ts, tpe=S // ts)
        (u2,) = _rows_fwd(f"ln_silu_{l}", _ln_silu, [_whole(u1)], [], [row(conv_ln_w[l]), row(conv_ln_b[l])],
                          [(C1, MXU_DTYPE)], tm=tmf)
        u3 = _mm(f"pw2_{l}", u2, W["w_pw2"], "nn")
        (y_conv,) = _rows_fwd(f"conv_out_norm_{l}", _rms, [_whole(u3)], [], [row(conv_out_norm_w[l])],
                              [(C1, MXU_DTYPE)], tm=tmf)
        qkv = _conv_fwd(f"qkv_conv_{l}", proj, 2 * C1, qkv_w_full[l], None, ts=ts, tpe=S // ts)
        al, dtb = pad_row(a_log[l]), pad_row(dt_bias[l])
        u, w, qd, kd, gl, attn = _dn_prep_fwd(f"dn_prep_{l}", qkv, proj, 6 * C1, al, dtb, NH=NH)
        o, st = _dn_scan_fwd(f"dn_scan_{l}", u, w, qd, kd, gl, attn, NH=NH, B=B)
        (y_dn,) = _rows_fwd(f"dn_out_norm_{l}", _gated_rms, [_whole(o), (proj, DN, 5)], [], [row(dn_norm_w[l])],
                            [(DN, MXU_DTYPE)], tm=tmf, ngroups=NH)
        ycat = jnp.concatenate([y_conv, y_dn], axis=1)
        y = _mm(f"out_{l}", ycat, W["w_out"], "nn")
        x1, h2 = _rows_fwd(f"junction2_{l}", _junction, [_whole(x0), _whole(y)], [g1_, sh2, sc2],
                           [row(norm2_w[l])], [(D, F32), (D, MXU_DTYPE)], tm=tmf, tpe=S // tmf)
        up = _mm(f"up_{l}", h2, W["w_up"], "nn")
        (act,) = _rows_fwd(f"relu2_{l}", _relu2, [_whole(up)], [], [], [(FF, MXU_DTYPE)], tm=tmb)
        mlp = _mm(f"down_{l}", act, W["w_down"], "nn")
        sv.update(x0=x0, h1=h1, proj=proj, u0=u0, u1=u1, u2=u2, u3=u3, qkv=qkv, al=al, dtb=dtb, u=u, w=w, qd=qd, kd=kd,
                  gl=gl, attn=attn, o=o, st=st, ycat=ycat, y=y, x1=x1, h2=h2, up=up, act=act)
        saved.append(sv)
        xcur, ycur, gate_prev = x1, mlp, g2_

    shf, scf = split_mod(mod_me[L], 2)
    tgt = loss_target.reshape(T, D)
    rowloss, dx, dy, dgate, dshf, dscf, dfinal_norm = _rows_vjp(
        "loss_head", _final_loss, [_whole(xcur), _whole(ycur), _whole(tgt)], [gate_prev, shf, scf], [row(final_norm_w)],
        None, [F32, MXU_DTYPE, None], tm=tmb, tpe=S // tmb, primal=[(1, F32)])
    loss = lax.psum(jnp.sum(rowloss), ("x", "y", "c"))

    dmods = [None] * L
    small = [None] * L
    big_grads = {}
    for l in reversed(range(L)):
        W, sv = wfull[l], saved[l]
        dact = _mm(f"d_down_x_{l}", dy, W["w_down"], "nt")
        gw_down = _mm(f"d_down_w_{l}", sv["act"], dy, "tn")
        (dup,) = _rows_vjp(f"d_relu2_{l}", _relu2, [_whole(sv["up"])], [], [], [_whole(dact)], [MXU_DTYPE], tm=tmb)
        dh2 = _mm(f"d_up_x_{l}", dup, W["w_up"], "nt")
        gw_up = _mm(f"d_up_w_{l}", sv["h2"], dup, "tn")
        dx0, dyo, dg1, dsh2, dsc2, dn2 = _rows_vjp(
            f"d_junction2_{l}", _junction, [_whole(sv["x0"]), _whole(sv["y"])], [sv["g1"], sv["sh2"], sv["sc2"]],
            [row(norm2_w[l])], [_whole(dx), _whole(dh2)], [F32, MXU_DTYPE], tm=tmb, tpe=S // tmb)
        dycat = _mm(f"d_out_x_{l}", dyo, W["w_out"], "nt")
        gw_out = _mm(f"d_out_w_{l}", sv["ycat"], dyo, "tn")
        proj = sv["proj"]
        do, dz, ddn = _rows_vjp(f"d_dn_out_norm_{l}", _gated_rms, [_whole(sv["o"]), (proj, DN, 5)], [],
                                [row(dn_norm_w[l])], [(dycat, DN, 1)], [F32, MXU_DTYPE], tm=tmb, ngroups=NH)
        cts = _dn_scan_bwd(f"d_dn_scan_{l}", do, sv["u"], sv["w"], sv["qd"], sv["kd"], sv["gl"], sv["attn"], sv["st"],
                           NH=NH, B=B)
        dq, dk, dv, dba, dal, ddt = _dn_prep_bwd(f"d_dn_prep_{l}", sv["qkv"], proj, 6 * C1, sv["al"], sv["dtb"],
                                                 [cts[0], cts[1], cts[2], cts[3], cts[4], cts[5]], F32, NH=NH)
        dqkv, gqw = [], []
        for k, dpart in enumerate((dq, dk, dv)):
            dxp, dwp, _ = _conv_bwd(f"d_qkv_conv_{l}_{k}", proj, (2 + k) * C1, dpart,
                                    qkv_w_full[l][:, k * DN:(k + 1) * DN], MXU_DTYPE, ts=ts, tpe=S // ts)
            dqkv.append(dxp)
            gqw.append(dwp)
        (du3, dcon) = _rows_vjp(f"d_conv_out_norm_{l}", _rms, [_whole(sv["u3"])], [], [row(conv_out_norm_w[l])],
                                [(dycat, C1, 0)], [MXU_DTYPE], tm=tmb)
        du2 = _mm(f"d_pw2_x_{l}", du3, W["w_pw2"], "nt")
        gw_pw2 = _mm(f"d_pw2_w_{l}", sv["u2"], du3, "tn")
        du1, dlnw, dlnb = _rows_vjp(f"d_ln_silu_{l}", _ln_silu, [_whole(sv["u1"])], [],
                                    [row(conv_ln_w[l]), row(conv_ln_b[l])], [_whole(du2)], [F32], tm=tmb)
        du0, gcw, gcb = _conv_bwd(f"d_conv_{l}", sv["u0"], 0, du1, conv_w_full[l], F32, ts=ts, tpe=S // ts)
        dval, dgate_c = _rows_vjp(f"d_glu_{l}", _glu, [(proj, C1, 0), (proj, C1, 1)], [], [], [_whole(du0)],
                                  [MXU_DTYPE, MXU_DTYPE], tm=tmb)
        dproj = jnp.concatenate([dval, dgate_c] + dqkv + [dz, dba.astype(MXU_DTYPE)], axis=1)
        dh1 = _mm(f"d_proj_x_{l}", dproj, W["w_in"], "nt", tk=896)
        gw_in = _mm(f"d_proj_w_{l}", sv["h1"], dproj, "tn", tn=896)[:, :IN]
        dxn, dyn, dg2p, dsh1, dsc1, dn1 = _rows_vjp(
            f"d_junction1_{l}", _junction, [_whole(sv["x_in"]), _whole(sv["y_in"])], [sv["gate_in"], sv["sh1"], sv["sc1"]],
            [row(norm1_w[l])], [_whole(dx0), _whole(dh1)], [F32, MXU_DTYPE], tm=tmb, tpe=S // tmb)
        dmods[l] = [dsh1, dsc1, dg1, dsh2, dsc2, dgate]
        small[l] = dict(norm1_w=dn1, conv_dw_w=gcw, conv_dw_b=gcb, conv_ln_w=dlnw, conv_ln_b=dlnb, conv_out_norm_w=dcon,
                        qkv_conv_w=jnp.concatenate(gqw, axis=1), a_log=dal[:, :NH], dt_bias=ddt[:, :NH], dn_norm_w=ddn,
                        norm2_w=dn2)
        shard_major = [jnp.moveaxis(gw_in.reshape(D, 4, IN // 4), 1, 0), gw_pw2.reshape(4, C1 // 4, C1),
                       gw_out.reshape(4, D // 4, D), jnp.moveaxis(gw_up.reshape(D, 4, FF // 4), 1, 0),
                       gw_down.reshape(4, FF // 4, D)]
        big_grads[l] = _reduce_scatter(str(l), shard_major, cidx, sidx)
        dx, dy, dgate = dxn, dyn, dg2p

    grad_x = dx.reshape(B, S, D)

    small_names = ["norm1_w", "conv_dw_w", "conv_dw_b", "conv_ln_w", "conv_ln_b", "conv_out_norm_w", "qkv_conv_w", "a_log",
                   "dt_bias", "dn_norm_w", "norm2_w"]
    dmod_flat = jnp.concatenate([jnp.concatenate([t.reshape(B, D) for t in dmods[l]], axis=1) for l in range(L)]
                                + [dshf.reshape(B, D), dscf.reshape(B, D)], axis=1)
    small_list = [small[l][n] for l in range(L) for n in small_names] + [dfinal_norm]
    shapes3 = [dmod_flat.shape] + [t.shape for t in small_list]
    g3 = _all_gather8("gather_grads", _pack([dmod_flat] + small_list))
    rows3 = g3.shape[0] // 8
    g3 = g3.reshape(8, rows3, PACK_LANES)
    summed = _unpack(_sum0("sum_small_grads", g3).reshape(-1), shapes3)[1:]
    dmod_all = _unpack(g3.reshape(8, -1), shapes3[:1])[0].reshape(8 * B, -1)
    nm = dmod_all.shape[1]
    grad_b_all = _sum0("sum_mod_grads", dmod_all.reshape(8 * B, nm // PACK_LANES, PACK_LANES)).reshape(-1)
    grads = {}
    cact_t = c_act
    gwa = []
    for l in range(L):
        dm = lax.dynamic_slice(dmod_all, (0, l * 6 * D + s_me * NW), (8 * B, NW))
        gwa.append(_mm(f"d_ada_w_{l}", cact_t, dm, "tn"))
    grads["w_ada"] = jnp.stack(gwa)
    grads["b_ada"] = grad_b_all[:L * 6 * D].reshape(L, 6 * D)
    dm = lax.dynamic_slice(dmod_all, (0, L * 6 * D + s_me * NF), (8 * B, NF))
    grads["final_ada_w"] = _mm("d_final_ada_w", cact_t, dm, "tn")
    grads["final_ada_b"] = grad_b_all[L * 6 * D:]
    per_layer = {n: [] for n in small_names}
    for l in range(L):
        for k, n in enumerate(small_names):
            per_layer[n].append(summed[l * len(small_names) + k])
    for n in small_names:
        t = jnp.stack(per_layer[n])
        if n == "conv_dw_w":
            t = lax.dynamic_slice(t, (0, 0, s_me * (C1 // 4)), (L, KC, C1 // 4))
        elif n == "qkv_conv_w":
            t = lax.dynamic_slice(t, (0, 0, s_me * (3 * DN // 4)), (L, KQ, 3 * DN // 4))
        grads[n] = t.reshape(weights[n].shape)
    grads["final_norm_w"] = summed[-1].reshape(final_norm_w.shape)
    for k, n in enumerate(["w_in", "w_pw2", "w_out", "w_up", "w_down"]):
        grads[n] = jnp.stack([big_grads[l][k] for l in range(L)])

    delta, new_m, new_v = {}, {}, {}
    big_names = ["w_ada", "w_in", "w_pw2", "w_out", "w_up", "w_down", "final_ada_w"]
    for n in big_names:
        shp = weights[n].shape
        two = lambda t: t.reshape(-1, shp[-1])
        d_, m_, v_ = _adamw(f"adamw_{n}", two(weights[n]), two(grads[n]), two(mom1[n]), two(mom2[n]))
        delta[n], new_m[n], new_v[n] = d_.reshape(shp), m_.reshape(shp), v_.reshape(shp)
    rest = [n for n in names if n not in big_names]
    rshapes = [weights[n].shape for n in rest]
    packed = [_pack([d[n] for n in rest]) for d in (weights, grads, mom1, mom2)]
    outs = _adamw("adamw_small", *packed)
    for dst, arr in zip((delta, new_m, new_v), outs):
        for n, t in zip(rest, _unpack(arr.reshape(-1), rshapes)):
            dst[n] = t

    return (loss, grad_x, *[grads[n] for n in names], *[delta[n] for n in names], *[new_m[n] for n in names],
            *[new_v[n] for n in names])
```

```python
import functools
import math

import jax
import jax.numpy as jnp
from jax import lax
from jax.experimental import pallas as pl
from jax.experimental.pallas import tpu as pltpu

F32 = jnp.float32
MXU_DTYPE = jnp.bfloat16
COMM_DTYPE = jnp.bfloat16
HI = lax.Precision.HIGHEST
CHUNK = 64
EPS = 1e-6
LANE = 128
SUB = 8
PACK_LANES = 1024
VMEM_LIMIT = 56 * 1024 * 1024
ADAM_LR, ADAM_B1, ADAM_B2, ADAM_EPS, ADAM_WD, ADAM_STEP = 0.001, 0.9, 0.999, 1e-08, 0.01, 10
MESH = pl.DeviceIdType.MESH
ANY = pl.BlockSpec(memory_space=pl.ANY)
NN, NT, TN = ((1,), (0,)), ((1,), (1,)), ((0,), (0,))


def _pick(dim, pref, mult):
    for t in range(min(pref, dim), 0, -1):
        if dim % t == 0 and t % mult == 0:
            return t
    return dim


def _params(sem):
    return pltpu.CompilerParams(dimension_semantics=sem, vmem_limit_bytes=VMEM_LIMIT)


def _sigmoid(v):
    return 1.0 / (1.0 + jnp.exp(-v))


def _silu(v):
    return v * _sigmoid(v)


def _row_specs(rows, exps, gls, tm, tpe):
    specs = [pl.BlockSpec((tm, w), functools.partial(lambda i, j: (i, j), j=cb)) for _, w, cb in rows]
    specs += [pl.BlockSpec((1, 1, e.shape[-1]), lambda i: (i // tpe, 0, 0)) for e in exps]
    specs += [pl.BlockSpec((1, g.shape[-1]), lambda i: (0, 0)) for g in gls]
    return specs


def _rows_fwd(name, fn, rows, exps, gls, outs, *, tm, tpe=1, ngroups=1):
    T = rows[0][0].shape[0]
    nr, ne, ng = len(rows), len(exps), len(gls)

    def body(*refs):
        r, e, g, o = refs[:nr], refs[nr:nr + ne], refs[nr + ne:nr + ne + ng], refs[nr + ne + ng:]
        ev = [t[0].astype(F32) for t in e]
        gv = [t[...].astype(F32) for t in g]
        for k in range(ngroups):
            rv = [t[:, k * (w // ngroups):(k + 1) * (w // ngroups)].astype(F32) for t, (_, w, _) in zip(r, rows)]
            res = fn(*rv, *ev, *gv)
            for oref, val, (w, dt) in zip(o, res, outs):
                gw = w // ngroups
                oref[:, k * gw:(k + 1) * gw] = val.astype(dt)

    return pl.pallas_call(
        body, name=name, grid=(T // tm,),
        in_specs=_row_specs(rows, exps, gls, tm, tpe),
        out_specs=[pl.BlockSpec((tm, w), lambda i: (i, 0)) for w, _ in outs],
        out_shape=[jax.ShapeDtypeStruct((T, w), dt) for w, dt in outs],
        compiler_params=_params(("arbitrary",)),
    )(*[a for a, _, _ in rows], *exps, *gls)


def _rows_vjp(name, fn, rows, exps, gls, cts, row_dtypes, *, tm, tpe=1, ngroups=1, primal=None):
    T = rows[0][0].shape[0]
    nr, ne, ng = len(rows), len(exps), len(gls)
    nc = 0 if cts is None else len(cts)
    keep = [k for k, dt in enumerate(row_dtypes) if dt is not None]
    npr = 0 if primal is None else len(primal)

    def body(*refs):
        r, e, g = refs[:nr], refs[nr:nr + ne], refs[nr + ne:nr + ne + ng]
        c = refs[nr + ne + ng:nr + ne + ng + nc]
        o = refs[nr + ne + ng + nc:]
        po, ro, eo, go = o[:npr], o[npr:npr + len(keep)], o[npr + len(keep):npr + len(keep) + ne], o[npr + len(keep) + ne:]
        i = pl.program_id(0)
        ev = [t[0].astype(F32) for t in e]
        gv = [t[...].astype(F32) for t in g]
        esum = [jnp.zeros_like(v) for v in ev]
        gsum = [jnp.zeros_like(v) for v in gv]
        for k in range(ngroups):
            rv = [t[:, k * (w // ngroups):(k + 1) * (w // ngroups)].astype(F32) for t, (_, w, _) in zip(r, rows)]
            res, pull = jax.vjp(fn, *rv, *ev, *gv)
            if cts is None:
                ct = tuple(jnp.ones_like(v) for v in res)
            else:
                ct = tuple(t[:, k * (w // ngroups):(k + 1) * (w // ngroups)].astype(F32) for t, (_, w, _) in zip(c, cts))
            grads = pull(ct)
            for oref, val, (w, dt) in zip(po, res, primal or ()):
                gw = w // ngroups
                oref[:, k * gw:(k + 1) * gw] = val.astype(dt)
            for oref, idx in zip(ro, keep):
                gw = rows[idx][1] // ngroups
                oref[:, k * gw:(k + 1) * gw] = grads[idx].astype(row_dtypes[idx])
            esum = [s + d for s, d in zip(esum, grads[nr:nr + ne])]
            gsum = [s + d for s, d in zip(gsum, grads[nr + ne:])]

        if ne:
            @pl.when(i % tpe == 0)
            def _():
                for oref in eo:
                    oref[...] = jnp.zeros_like(oref)
            for oref, s in zip(eo, esum):
                oref[0] += s
        if ng:
            @pl.when(i == 0)
            def _():
                for oref in go:
                    oref[...] = jnp.zeros_like(oref)
            for oref, s in zip(go, gsum):
                oref[...] += s

    out_specs = [pl.BlockSpec((tm, w), lambda i: (i, 0)) for w, _ in (primal or ())]
    out_shape = [jax.ShapeDtypeStruct((T, w), dt) for w, dt in (primal or ())]
    out_specs += [pl.BlockSpec((tm, rows[k][1]), lambda i: (i, 0)) for k in keep]
    out_shape += [jax.ShapeDtypeStruct((T, rows[k][1]), row_dtypes[k]) for k in keep]
    out_specs += [pl.BlockSpec((1, 1, e.shape[-1]), lambda i: (i // tpe, 0, 0)) for e in exps]
    out_shape += [jax.ShapeDtypeStruct(e.shape, F32) for e in exps]
    out_specs += [pl.BlockSpec((1, g.shape[-1]), lambda i: (0, 0)) for g in gls]
    out_shape += [jax.ShapeDtypeStruct(g.shape, F32) for g in gls]
    ct_specs = [] if cts is None else [pl.BlockSpec((tm, w), functools.partial(lambda i, j: (i, j), j=cb)) for _, w, cb in cts]
    ct_arrs = [] if cts is None else [a for a, _, _ in cts]
    return pl.pallas_call(
        body, name=name, grid=(T // tm,),
        in_specs=_row_specs(rows, exps, gls, tm, tpe) + ct_specs,
        out_specs=out_specs, out_shape=out_shape,
        compiler_params=_params(("arbitrary",)),
    )(*[a for a, _, _ in rows], *exps, *gls, *ct_arrs)


def _whole(a):
    return (a, a.shape[-1], 0)


def _junction(x, y, gate, shift, scale, w):
    xn = x + gate * y
    r = lax.rsqrt(jnp.mean(xn * xn, axis=-1, keepdims=True) + EPS)
    return xn, (xn * r * w) * (1.0 + scale) + shift


def _final_loss(x, y, tgt, gate, shift, scale, w):
    _, out = _junction(x, y, gate, shift, scale, w)
    err = out - tgt
    return (0.5 * jnp.mean(err * err, axis=-1, keepdims=True),)


def _glu(val, gate):
    return (val * _sigmoid(gate),)


def _ln_silu(u, w, b):
    xc = u - jnp.mean(u, axis=-1, keepdims=True)
    y = xc * lax.rsqrt(jnp.mean(xc * xc, axis=-1, keepdims=True) + EPS) * w + b
    return (_silu(y),)


def _rms(u, w):
    return (u * lax.rsqrt(jnp.mean(u * u, axis=-1, keepdims=True) + EPS) * w,)


def _gated_rms(o, z, w):
    return (o * lax.rsqrt(jnp.mean(o * o, axis=-1, keepdims=True) + EPS) * w * _silu(z),)


def _relu2(u):
    r = jnp.maximum(u, 0.0)
    return (r * r,)


def _silu_row(u):
    return (_silu(u),)


def _mm(name, a, b, mode, out_dtype=F32, bias=None, tm=1024, tn=1024, tk=2048):
    if mode == "nn":
        (M, K), N = a.shape, b.shape[1]
    elif mode == "nt":
        (M, K), N = a.shape, b.shape[0]
    else:
        (K, M), N = a.shape, b.shape[1]
    tm, tn, tk = _pick(M, tm, LANE), _pick(N, tn, LANE), _pick(K, tk, LANE)
    nk = K // tk
    dn = {"nn": NN, "nt": NT, "tn": TN}[mode]
    a_spec = pl.BlockSpec((tk, tm), lambda i, j, k: (k, i)) if mode == "tn" else pl.BlockSpec((tm, tk), lambda i, j, k: (i, k))
    b_spec = pl.BlockSpec((tn, tk), lambda i, j, k: (j, k)) if mode == "nt" else pl.BlockSpec((tk, tn), lambda i, j, k: (k, j))
    specs, args = [a_spec, b_spec], [a, b]
    if bias is not None:
        specs.append(pl.BlockSpec((1, tn), lambda i, j, k: (0, j)))
        args.append(bias)

    def body(*refs):
        a_ref, b_ref = refs[0], refs[1]
        o_ref, acc = refs[-2], refs[-1]
        k = pl.program_id(2)

        @pl.when(k == 0)
        def _():
            acc[...] = jnp.zeros_like(acc)

        acc[...] += lax.dot_general(a_ref[...].astype(MXU_DTYPE), b_ref[...].astype(MXU_DTYPE), (dn, ((), ())),
                                    preferred_element_type=F32)

        @pl.when(k == nk - 1)
        def _():
            res = acc[...]
            if bias is not None:
                res = res + refs[2][...]
            o_ref[...] = res.astype(out_dtype)

    return pl.pallas_call(
        body, name=name, grid=(M // tm, N // tn, nk),
        in_specs=specs, out_specs=pl.BlockSpec((tm, tn), lambda i, j, k: (i, j)),
        out_shape=jax.ShapeDtypeStruct((M, N), out_dtype),
        scratch_shapes=[pltpu.VMEM((tm, tn), F32)],
        compiler_params=_params(("parallel", "parallel", "arbitrary")),
    )(*args)


def _halo(K):
    return SUB * -(-(K - 1) // SUB)


def _conv_fwd(name, x, col0, w, bias, *, ts, tpe):
    T = x.shape[0]
    K, C = w.shape
    H = _halo(K)
    cb = _pick(C, 256, LANE)
    rb = _pick(ts, 64, SUB)
    off = col0 // cb
    specs = [pl.BlockSpec((ts, cb), lambda j, i: (i, off + j)),
             pl.BlockSpec((H, cb), lambda j, i: (jnp.maximum(i * (ts // H) - 1, 0), off + j)),
             pl.BlockSpec((K, cb), lambda j, i: (0, j))]
    args = [x, x, w]
    if bias is not None:
        specs.append(pl.BlockSpec((1, cb), lambda j, i: (0, j)))
        args.append(bias)

    def body(*refs):
        cur, halo, w_ref = refs[:3]
        o_ref, xp = refs[-2], refs[-1]
        i = pl.program_id(1)
        xp[0:H, :] = jnp.where(i % tpe == 0, 0.0, halo[...].astype(F32))
        xp[H:H + ts, :] = cur[...].astype(F32)
        for r0 in range(0, ts, rb):
            acc = jnp.zeros((rb, cb), F32) if bias is None else jnp.zeros((rb, cb), F32) + refs[3][...]
            for j in range(K):
                lo = H - (K - 1) + j + r0
                acc = acc + w_ref[j:j + 1, :] * xp[lo:lo + rb, :]
            o_ref[r0:r0 + rb, :] = acc

    return pl.pallas_call(
        body, name=name, grid=(C // cb, T // ts), in_specs=specs,
        out_specs=pl.BlockSpec((ts, cb), lambda j, i: (i, j)),
        out_shape=jax.ShapeDtypeStruct((T, C), F32),
        scratch_shapes=[pltpu.VMEM((H + ts, cb), F32)],
        compiler_params=_params(("parallel", "arbitrary")),
    )(*args)


def _conv_bwd(name, x, col0, dy, w, out_dtype, *, ts, tpe):
    T = x.shape[0]
    K, C = w.shape
    H = _halo(K)
    cb = _pick(C, 256, LANE)
    rb = _pick(ts, 64, SUB)
    off = col0 // cb
    nt = T // ts

    def body(cur, halo, dyc, dyn, w_ref, dx_ref, dw_ref, db_ref, xp, dyp):
        i = pl.program_id(1)
        xp[0:H, :] = jnp.where(i % tpe == 0, 0.0, halo[...].astype(F32))
        xp[H:H + ts, :] = cur[...].astype(F32)
        dyp[0:ts, :] = dyc[...]
        dyp[ts:ts + H, :] = jnp.where(i % tpe == tpe - 1, 0.0, dyn[...])
        for r0 in range(0, ts, rb):
            acc = jnp.zeros((rb, cb), F32)
            for j in range(K):
                lo = K - 1 - j + r0
                acc = acc + w_ref[j:j + 1, :] * dyp[lo:lo + rb, :]
            dx_ref[r0:r0 + rb, :] = acc.astype(out_dtype)

        @pl.when(i == 0)
        def _():
            dw_ref[...] = jnp.zeros_like(dw_ref)
            db_ref[...] = jnp.zeros_like(db_ref)

        for j in range(K):
            part = jnp.zeros((1, cb), F32)
            for r0 in range(0, ts, rb):
                lo = H - (K - 1) + j + r0
                part = part + jnp.sum(dyp[r0:r0 + rb, :] * xp[lo:lo + rb, :], axis=0, keepdims=True)
            dw_ref[j:j + 1, :] += part
        db_ref[...] += jnp.sum(dyc[...], axis=0, keepdims=True)

    return pl.pallas_call(
        body, name=name, grid=(C // cb, nt),
        in_specs=[pl.BlockSpec((ts, cb), lambda j, i: (i, off + j)),
                  pl.BlockSpec((H, cb), lambda j, i: (jnp.maximum(i * (ts // H) - 1, 0), off + j)),
                  pl.BlockSpec((ts, cb), lambda j, i: (i, j)),
                  pl.BlockSpec((H, cb), lambda j, i: (jnp.minimum((i + 1) * (ts // H), T // H - 1), j)),
                  pl.BlockSpec((K, cb), lambda j, i: (0, j))],
        out_specs=[pl.BlockSpec((ts, cb), lambda j, i: (i, j)),
                   pl.BlockSpec((K, cb), lambda j, i: (0, j)),
                   pl.BlockSpec((1, cb), lambda j, i: (0, j))],
        out_shape=[jax.ShapeDtypeStruct((T, C), out_dtype), jax.ShapeDtypeStruct((K, C), F32),
                   jax.ShapeDtypeStruct((1, C), F32)],
        scratch_shapes=[pltpu.VMEM((H + ts, cb), F32), pltpu.VMEM((ts + H, cb), F32)],
        compiler_params=_params(("parallel", "arbitrary")),
    )(x, x, dy, dy, w)


def _hdot(a, b, dn):
    return lax.dot_general(a, b, (dn, ((), ())), precision=HI, preferred_element_type=F32)


def _bdot(a, b, dn):
    return lax.dot_general(a.astype(MXU_DTYPE), b.astype(MXU_DTYPE), (dn, ((), ())), preferred_element_type=F32)


def _split(a):
    hi = a.astype(MXU_DTYPE)
    return hi, (a - hi.astype(F32)).astype(MXU_DTYPE)


def _dot3_raw(a, b, dn):
    if MXU_DTYPE == F32:
        return _hdot(a, b, dn)
    (ah, al), (bh, bl) = _split(a), _split(b)
    d = lambda p, q: lax.dot_general(p, q, (dn, ((), ())), preferred_element_type=F32)
    return d(ah, bh) + (d(ah, bl) + d(al, bh))


def _with_vjp(raw):
    dn = {"nn": NN, "nt": NT, "tn": TN}

    @functools.partial(jax.custom_vjp, nondiff_argnums=(2,))
    def dot(a, b, mode):
        return raw(a, b, dn[mode])

    def fwd(a, b, mode):
        return raw(a, b, dn[mode]), (a, b)

    def bwd(mode, res, ct):
        a, b = res
        if mode == "nn":
            return raw(ct, b, NT), raw(a, ct, TN)
        if mode == "nt":
            return raw(ct, b, NN), raw(ct, a, TN)
        return raw(b, ct, NT), raw(a, ct, NN)

    dot.defvjp(fwd, bwd)
    return dot


_dot_exact = _with_vjp(_hdot)
_dot3 = _with_vjp(_dot3_raw)
_dot1 = _with_vjp(_bdot)


@jax.custom_vjp
def _tri_inv(n):
    C = n.shape[0]
    ri = lax.broadcasted_iota(jnp.int32, (C, C), 0)
    cj = lax.broadcasted_iota(jnp.int32, (C, C), 1)
    t = (ri == cj).astype(F32) + n
    p = n
    for _ in range(int(math.log2(C)) - 1):
        p = _dot3_raw(p, p, NN)
        t = t + _dot3_raw(t, p, NN)
    return t


def _tri_inv_fwd(n):
    t = _tri_inv(n)
    return t, t


def _tri_inv_bwd(t, ct):
    return (_dot3_raw(_dot3_raw(t, ct, TN), t, NT),)


_tri_inv.defvjp(_tri_inv_fwd, _tri_inv_bwd)


def _dn_prep(qs, ks, vs, ba, alog, dtb, *, nh):
    C, Dk = qs[0].shape
    z = ba + dtb
    g_all = -jnp.exp(alog) * (jnp.maximum(z, 0.0) + jnp.log(1.0 + jnp.exp(-jnp.abs(z))))
    ri = lax.broadcasted_iota(jnp.int32, (C, C), 0)
    cj = lax.broadcasted_iota(jnp.int32, (C, C), 1)
    causal, strict = ri >= cj, ri > cj
    gc_all = _dot_exact(causal.astype(F32), g_all, "nn")
    gc_rows = _dot_exact(g_all, (ri <= cj).astype(F32), "tn")
    lane = lax.broadcasted_iota(jnp.int32, (1, ba.shape[1]), 1)
    subl = lax.broadcasted_iota(jnp.int32, (ba.shape[1], 1), 0)
    out = []
    for h in range(nh):
        sel = (lane == nh + h).astype(F32)
        bl = jnp.sum(ba * (lane == h).astype(F32), axis=1, keepdims=True)
        gc = jnp.sum(gc_all * sel, axis=1, keepdims=True)
        gl = jnp.sum(jnp.sum(g_all * sel, axis=1, keepdims=True), axis=0, keepdims=True)
        gcr = jnp.sum(gc_rows * (subl == nh + h).astype(F32), axis=0, keepdims=True)
        decay = jnp.where(causal, jnp.exp(jnp.where(causal, gc - gcr, 0.0)), 0.0)
        q, k, v = _silu(qs[h]), _silu(ks[h]), _silu(vs[h])
        q = q * lax.rsqrt(jnp.sum(q * q, axis=-1, keepdims=True) + EPS) * (Dk ** -0.5)
        k = k * lax.rsqrt(jnp.sum(k * k, axis=-1, keepdims=True) + EPS)
        beta = _sigmoid(bl)
        kb, vb = k * beta, v * beta
        t = _tri_inv(-jnp.where(strict, _dot1(kb, k, "nt") * decay, 0.0))
        egc = jnp.exp(gc)
        u = _dot3(t, vb, "nn")
        w = _dot3(t, kb * egc, "nn")
        attn = jnp.where(causal, _dot1(q, k, "nt") * decay, 0.0)
        out.append((u, w, q * egc, k * jnp.exp(gl - gc), attn, jnp.exp(gl) * jnp.ones((C, Dk), F32)))
    return tuple(zip(*out))


def _dn_prep_specs(C, DN, col_q):
    qs = [pl.BlockSpec((C, DN), functools.partial(lambda i, j: (i, j), j=j)) for j in range(3)]
    return qs + [pl.BlockSpec((C, LANE), lambda i: (i, col_q)), pl.BlockSpec((1, LANE), lambda i: (0, 0)),
                 pl.BlockSpec((1, LANE), lambda i: (0, 0))]


def _heads(ref, NH, Dk):
    return tuple(ref[:, h * Dk:(h + 1) * Dk] for h in range(NH))


def _dn_prep_fwd(name, qkv, proj, ba_col, alog, dtb, *, NH):
    T = qkv.shape[0]
    DN = qkv.shape[1] // 3
    Dk = DN // NH
    C = CHUNK

    def body(q_ref, k_ref, v_ref, ba_ref, al_ref, dt_ref, u_ref, w_ref, qd_ref, kd_ref, gl_ref, at_ref):
        res = _dn_prep(_heads(q_ref, NH, Dk), _heads(k_ref, NH, Dk), _heads(v_ref, NH, Dk), ba_ref[...], al_ref[...],
                       dt_ref[...], nh=NH)
        for h in range(NH):
            sl = slice(h * Dk, (h + 1) * Dk)
            u_ref[:, sl], w_ref[:, sl], qd_ref[:, sl], kd_ref[:, sl] = res[0][h], res[1][h], res[2][h], res[3][h]
            at_ref[h] = res[4][h]
            gl_ref[:, sl] = res[5][h]

    big = pl.BlockSpec((C, DN), lambda i: (i, 0))
    return pl.pallas_call(
        body, name=name, grid=(T // C,),
        in_specs=_dn_prep_specs(C, DN, ba_col // LANE),
        out_specs=[big] * 5 + [pl.BlockSpec((NH, C, C), lambda i: (0, i, 0))],
        out_shape=[jax.ShapeDtypeStruct((T, DN), F32)] * 5 + [jax.ShapeDtypeStruct((NH, T, C), F32)],
        compiler_params=_params(("arbitrary",)),
    )(qkv, qkv, qkv, proj, alog, dtb)


def _dn_prep_bwd(name, qkv, proj, ba_col, alog, dtb, cts, out_dtype, *, NH):
    T = qkv.shape[0]
    DN = qkv.shape[1] // 3
    Dk = DN // NH
    C = CHUNK

    def body(q_ref, k_ref, v_ref, ba_ref, al_ref, dt_ref, du, dw, dqd, dkd, dgl, dat,
             dq_ref, dk_ref, dv_ref, dba_ref, dal_ref, ddt_ref):
        i = pl.program_id(0)
        _, pull = jax.vjp(functools.partial(_dn_prep, nh=NH), _heads(q_ref, NH, Dk), _heads(k_ref, NH, Dk),
                          _heads(v_ref, NH, Dk), ba_ref[...], al_ref[...], dt_ref[...])
        gq, gk, gv, gba, gal, gdt = pull((_heads(du, NH, Dk), _heads(dw, NH, Dk), _heads(dqd, NH, Dk),
                                          _heads(dkd, NH, Dk), tuple(dat[h] for h in range(NH)), _heads(dgl, NH, Dk)))
        for h in range(NH):
            sl = slice(h * Dk, (h + 1) * Dk)
            dq_ref[:, sl], dk_ref[:, sl], dv_ref[:, sl] = gq[h].astype(out_dtype), gk[h].astype(out_dtype), gv[h].astype(out_dtype)
        dba_ref[...] = gba.astype(out_dtype)

        @pl.when(i == 0)
        def _():
            dal_ref[...] = jnp.zeros_like(dal_ref)
            ddt_ref[...] = jnp.zeros_like(ddt_ref)

        dal_ref[...] += gal
        ddt_ref[...] += gdt

    big = pl.BlockSpec((C, DN), lambda i: (i, 0))
    row = pl.BlockSpec((1, LANE), lambda i: (0, 0))
    return pl.pallas_call(
        body, name=name, grid=(T // C,),
        in_specs=_dn_prep_specs(C, DN, ba_col // LANE) + [big] * 5 + [pl.BlockSpec((NH, C, C), lambda i: (0, i, 0))],
        out_specs=[big] * 3 + [pl.BlockSpec((C, LANE), lambda i: (i, 0)), row, row],
        out_shape=[jax.ShapeDtypeStruct((T, DN), out_dtype)] * 3 + [jax.ShapeDtypeStruct((T, LANE), out_dtype),
                                                                     jax.ShapeDtypeStruct((1, LANE), F32),
                                                                     jax.ShapeDtypeStruct((1, LANE), F32)],
        compiler_params=_params(("arbitrary",)),
    )(qkv, qkv, qkv, proj, alog, dtb, *cts)


def _dn_scan_fwd(name, u, w, qd, kd, gl, attn, *, NH, B):
    T, DN = u.shape
    Dk = DN // NH
    C = CHUNK
    NC = T // (B * C)

    def body(u_ref, w_ref, qd_ref, kd_ref, gl_ref, at_ref, o_ref, st_ref, s_ref):
        @pl.when(pl.program_id(1) == 0)
        def _():
            s_ref[...] = jnp.zeros_like(s_ref)

        for h in range(NH):
            sl = slice(h * Dk, (h + 1) * Dk)
            s = s_ref[h]
            st_ref[0, h] = s
            vnew = u_ref[:, sl] - _bdot(w_ref[:, sl], s, NN)
            o_ref[:, sl] = _bdot(qd_ref[:, sl], s, NN) + _bdot(at_ref[h], vnew, NN)
            s_ref[h] = s * gl_ref[0:1, sl] + _bdot(kd_ref[:, sl], vnew, TN)

    big = pl.BlockSpec((C, DN), lambda b, n: (b * NC + n, 0))
    return pl.pallas_call(
        body, name=name, grid=(B, NC),
        in_specs=[big] * 5 + [pl.BlockSpec((NH, C, C), lambda b, n: (0, b * NC + n, 0))],
        out_specs=[big, pl.BlockSpec((1, NH, Dk, Dk), lambda b, n: (b * NC + n, 0, 0, 0))],
        out_shape=[jax.ShapeDtypeStruct((T, DN), F32), jax.ShapeDtypeStruct((T // C, NH, Dk, Dk), F32)],
        scratch_shapes=[pltpu.VMEM((NH, Dk, Dk), F32)],
        compiler_params=_params(("arbitrary", "arbitrary")),
    )(u, w, qd, kd, gl, attn)


def _dn_scan_bwd(name, do, u, w, qd, kd, gl, attn, st, *, NH, B):
    T, DN = u.shape
    Dk = DN // NH
    C = CHUNK
    NC = T // (B * C)

    def body(do_ref, u_ref, w_ref, qd_ref, kd_ref, gl_ref, at_ref, st_ref,
             du_ref, dw_ref, dqd_ref, dkd_ref, dgl_ref, dat_ref, ds_ref):
        @pl.when(pl.program_id(1) == 0)
        def _():
            ds_ref[...] = jnp.zeros_like(ds_ref)

        row0 = lax.broadcasted_iota(jnp.int32, (C, Dk), 0) == 0
        for h in range(NH):
            sl = slice(h * Dk, (h + 1) * Dk)
            s, ds, g = st_ref[0, h], ds_ref[h], do_ref[:, sl]
            wv, at, kdv = w_ref[:, sl], at_ref[h], kd_ref[:, sl]
            vnew = u_ref[:, sl] - _bdot(wv, s, NN)
            dvnew = _bdot(at, g, TN) + _bdot(kdv, ds, NN)
            dat_ref[h] = _bdot(g, vnew, NT)
            dqd_ref[:, sl] = _bdot(g, s, NT)
            dkd_ref[:, sl] = _bdot(vnew, ds, NT)
            dgl_ref[:, sl] = jnp.where(row0, jnp.sum(s * ds, axis=0, keepdims=True), 0.0)
            du_ref[:, sl] = dvnew
            dw_ref[:, sl] = -_bdot(dvnew, s, NT)
            ds_ref[h] = _bdot(qd_ref[:, sl], g, TN) + ds * gl_ref[0:1, sl] - _bdot(wv, dvnew, TN)

    big = pl.BlockSpec((C, DN), lambda b, n: (b * NC + NC - 1 - n, 0))
    att = pl.BlockSpec((NH, C, C), lambda b, n: (0, b * NC + NC - 1 - n, 0))
    return pl.pallas_call(
        body, name=name, grid=(B, NC),
        in_specs=[big] * 6 + [att, pl.BlockSpec((1, NH, Dk, Dk), lambda b, n: (b * NC + NC - 1 - n, 0, 0, 0))],
        out_specs=[big] * 5 + [att],
        out_shape=[jax.ShapeDtypeStruct((T, DN), F32)] * 5 + [jax.ShapeDtypeStruct((NH, T, C), F32)],
        scratch_shapes=[pltpu.VMEM((NH, Dk, Dk), F32)],
        compiler_params=_params(("arbitrary", "arbitrary")),
    )(do, u, w, qd, kd, gl, attn, st)


def _sum0(name, a):
    n, r, ln = a.shape
    tr = _pick(r, 64, SUB)

    def body(a_ref, o_ref):
        acc = a_ref[0]
        for k in range(1, n):
            acc = acc + a_ref[k]
        o_ref[...] = acc

    return pl.pallas_call(
        body, name=name, grid=(r // tr,),
        in_specs=[pl.BlockSpec((n, tr, ln), lambda i: (0, i, 0))],
        out_specs=pl.BlockSpec((tr, ln), lambda i: (i, 0)),
        out_shape=jax.ShapeDtypeStruct((r, ln), F32),
        compiler_params=_params(("arbitrary",)),
    )(a)


def _adamw(name, w, g, m, v):
    R, Cc = w.shape
    tr = _pick(R, max(SUB, (1 << 18) // Cc // SUB * SUB), SUB)
    c1 = 1.0 - ADAM_B1 ** ADAM_STEP
    c2 = 1.0 - ADAM_B2 ** ADAM_STEP

    def body(w_ref, g_ref, m_ref, v_ref, d_ref, mo_ref, vo_ref):
        gv = g_ref[...]
        mn = ADAM_B1 * m_ref[...] + (1.0 - ADAM_B1) * gv
        vn = ADAM_B2 * v_ref[...] + (1.0 - ADAM_B2) * (gv * gv)
        mo_ref[...] = mn
        vo_ref[...] = vn
        d_ref[...] = -ADAM_LR * ((mn / c1) / (jnp.sqrt(vn / c2) + ADAM_EPS) + ADAM_WD * w_ref[...])

    spec = pl.BlockSpec((tr, Cc), lambda i: (i, 0))
    return pl.pallas_call(
        body, name=name, grid=(R // tr,), in_specs=[spec] * 4, out_specs=[spec] * 3,
        out_shape=[jax.ShapeDtypeStruct((R, Cc), F32)] * 3,
        compiler_params=_params(("arbitrary",)),
    )(w, g, m, v)


def _pack(arrs):
    flat = jnp.concatenate([a.reshape(-1).astype(F32) for a in arrs])
    pad = (-flat.shape[0]) % (SUB * PACK_LANES)
    return jnp.pad(flat, (0, pad)).reshape(-1, PACK_LANES)


def _unpack(flat, shapes):
    out, pos = [], 0
    for shp in shapes:
        n = math.prod(shp)
        out.append(flat[..., pos:pos + n].reshape(flat.shape[:-1] + tuple(shp)))
        pos += n
    return out


def _remote(src, dst, ssem, rsem, dev):
    return pltpu.make_async_remote_copy(src_ref=src, dst_ref=dst, send_sem=ssem, recv_sem=rsem, device_id=dev,
                                        device_id_type=MESH)


def _place():
    return lax.axis_index("x"), lax.axis_index("y"), lax.axis_index("c")


def _all_gather8(name, a):
    m, n = a.shape

    def body(x_ref, out_ref, send_sems, recv_sems, local_sem):
        x, y, c = _place()
        me, sibling = (x, y, c), (x, y, 1 - c)
        chips = [(1 - x, y), (x, 1 - y), (1 - x, 1 - y)]

        def rows(px, py, pc):
            return out_ref.at[pl.ds((4 * px + 2 * py + pc) * m, m), :]

        def copy(k, block, to, src=None):
            return _remote(rows(*block) if src is None else src, rows(*block), send_sems.at[k], recv_sems.at[k], to)

        mine = pltpu.make_async_copy(x_ref, rows(*me), local_sem)
        mine.start()
        first = [copy(0, me, sibling, src=x_ref)]
        first += [copy(1 + j, me, (*chip, c), src=x_ref) for j, chip in enumerate(chips)]
        for cp in first:
            cp.start()
        passed = [copy(4 + j, (*chip, c), sibling) for j, chip in enumerate(chips)]
        for j, chip in enumerate(chips):
            copy(1 + j, (*chip, c), me).wait_recv()
            passed[j].start()
        copy(0, sibling, me).wait_recv()
        for j, chip in enumerate(chips):
            copy(4 + j, (*chip, 1 - c), me).wait_recv()
        for cp in first + passed:
            cp.wait_send()
        mine.wait()

    return pl.pallas_call(
        body, name=name,
        out_shape=jax.ShapeDtypeStruct((8 * m, n), a.dtype),
        in_specs=[pl.BlockSpec(memory_space=pltpu.VMEM)],
        out_specs=pl.BlockSpec(memory_space=pltpu.VMEM),
        scratch_shapes=[pltpu.SemaphoreType.DMA((7,)), pltpu.SemaphoreType.DMA((7,)), pltpu.SemaphoreType.DMA],
        compiler_params=pltpu.CompilerParams(vmem_limit_bytes=VMEM_LIMIT),
    )(a)


def _chip_peers(x, y):
    return [(1 - x, y), (x, 1 - y), (1 - x, 1 - y)]


def _cast_into_slot(name, idx, a, dtype):
    R, Cc = a.shape
    tr = _pick(R, 256, 16)

    def body(i_ref, a_ref, o_ref):
        o_ref[0] = a_ref[...].astype(dtype)

    return pl.pallas_call(
        body, name=name,
        grid_spec=pltpu.PrefetchScalarGridSpec(
            num_scalar_prefetch=1, grid=(R // tr,),
            in_specs=[pl.BlockSpec((tr, Cc), lambda i, ix: (i, 0))],
            out_specs=pl.BlockSpec((1, tr, Cc), lambda i, ix: (ix[0], i, 0))),
        out_shape=jax.ShapeDtypeStruct((4, R, Cc), dtype),
        compiler_params=_params(("arbitrary",)),
    )(idx, a)


def _gather_chips(name, bufs):
    n = len(bufs)

    def body(*refs):
        bufs_ = refs[n:2 * n]
        ssem, rsem = refs[2 * n:]
        x, y, c = _place()
        sib = (x, y, 1 - c)
        peers = _chip_peers(x, y)
        sends, fwds = [], []
        for a in range(n):
            for k, (px, py) in enumerate(peers):
                mine = bufs_[a].at[2 * x + y, c]
                s = _remote(mine, mine, ssem.at[6 * a + k], rsem.at[6 * a + k], (px, py, c))
                s.start()
                sends.append(s)
        for a in range(n):
            for k, (px, py) in enumerate(peers):
                got = bufs_[a].at[2 * px + py, c]
                _remote(got, got, ssem.at[6 * a + k], rsem.at[6 * a + k], (px, py, c)).wait_recv()
                f = _remote(got, got, ssem.at[6 * a + 3 + k], rsem.at[6 * a + 3 + k], sib)
                f.start()
                fwds.append(f)
        for a in range(n):
            for k, (px, py) in enumerate(peers):
                got = bufs_[a].at[2 * px + py, 1 - c]
                _remote(got, got, ssem.at[6 * a + 3 + k], rsem.at[6 * a + 3 + k], sib).wait_recv()
        for cp in sends + fwds:
            cp.wait_send()

    return pl.pallas_call(
        body, name=name, in_specs=[ANY] * n, out_specs=[ANY] * n,
        out_shape=[jax.ShapeDtypeStruct(b.shape, b.dtype) for b in bufs],
        input_output_aliases={a: a for a in range(n)},
        scratch_shapes=[pltpu.SemaphoreType.DMA((6 * n,)), pltpu.SemaphoreType.DMA((6 * n,))],
    )(*bufs)


def _swap_halves(name, grads):
    n = len(grads)

    def body(*refs):
        ins, outs = refs[:n], refs[n:2 * n]
        ssem, rsem = refs[2 * n:]
        x, y, c = _place()
        cps = [_remote(ins[a].at[:, 1 - c], outs[a], ssem.at[a], rsem.at[a], (x, y, 1 - c)) for a in range(n)]
        for cp in cps:
            cp.start()
        for cp in cps:
            cp.wait_send()
            cp.wait_recv()

    return pl.pallas_call(
        body, name=name, in_specs=[ANY] * n, out_specs=[ANY] * n,
        out_shape=[jax.ShapeDtypeStruct((4,) + g.shape[2:], g.dtype) for g in grads],
        scratch_shapes=[pltpu.SemaphoreType.DMA((n,)), pltpu.SemaphoreType.DMA((n,))],
    )(*grads)


def _scatter_chips(name, halves):
    n = len(halves)

    def body(*refs):
        ins, outs = refs[:n], refs[n:2 * n]
        ssem, rsem = refs[2 * n:]
        x, y, c = _place()
        cps = []
        for a in range(n):
            for k, (px, py) in enumerate(_chip_peers(x, y)):
                cps.append(_remote(ins[a].at[2 * px + py], outs[a].at[k], ssem.at[3 * a + k], rsem.at[3 * a + k], (px, py, c)))
        for cp in cps:
            cp.start()
        for cp in cps:
            cp.wait_send()
            cp.wait_recv()

    return pl.pallas_call(
        body, name=name, in_specs=[ANY] * n, out_specs=[ANY] * n,
        out_shape=[jax.ShapeDtypeStruct((3,) + h.shape[1:], h.dtype) for h in halves],
        scratch_shapes=[pltpu.SemaphoreType.DMA((3 * n,)), pltpu.SemaphoreType.DMA((3 * n,))],
    )(*halves)


def _join_halves(name, parts):
    n = len(parts)

    def body(*refs):
        bufs_ = refs[n:2 * n]
        ssem, rsem = refs[2 * n:]
        x, y, c = _place()
        sends = [_remote(bufs_[a].at[c], bufs_[a].at[c], ssem.at[a], rsem.at[a], (x, y, 1 - c)) for a in range(n)]
        recvs = [_remote(bufs_[a].at[1 - c], bufs_[a].at[1 - c], ssem.at[a], rsem.at[a], (x, y, 1 - c)) for a in range(n)]
        for cp in sends:
            cp.start()
        for s, r in zip(sends, recvs):
            s.wait_send()
            r.wait_recv()

    return pl.pallas_call(
        body, name=name, in_specs=[ANY] * n, out_specs=[ANY] * n,
        out_shape=[jax.ShapeDtypeStruct(p.shape, p.dtype) for p in parts],
        input_output_aliases={a: a for a in range(n)},
        scratch_shapes=[pltpu.SemaphoreType.DMA((n,)), pltpu.SemaphoreType.DMA((n,))],
    )(*parts)


def _add_half(name, idx, g, r):
    _, _, Rh, Cc = g.shape
    tr = _pick(Rh, 128, 16)

    def body(i_ref, g_ref, r_ref, o_ref):
        o_ref[...] = (g_ref[0].astype(F32) + r_ref[...].astype(F32)).astype(o_ref.dtype)

    return pl.pallas_call(
        body, name=name,
        grid_spec=pltpu.PrefetchScalarGridSpec(
            num_scalar_prefetch=1, grid=(4, Rh // tr),
            in_specs=[pl.BlockSpec((1, 1, tr, Cc), lambda s, i, ix: (s, ix[1], i, 0)),
                      pl.BlockSpec((1, tr, Cc), lambda s, i, ix: (s, i, 0))],
            out_specs=pl.BlockSpec((1, tr, Cc), lambda s, i, ix: (s, i, 0))),
        out_shape=jax.ShapeDtypeStruct((4, Rh, Cc), g.dtype),
        compiler_params=_params(("arbitrary", "arbitrary")),
    )(idx, g, r)


def _add_chips(name, idx, h, r):
    _, Rh, Cc = h.shape
    tr = _pick(Rh, 128, 16)

    def body(i_ref, h_ref, r0, r1, r2, o_ref):
        o_ref[0] = ((h_ref[0].astype(F32) + r0[0].astype(F32)) + r1[0].astype(F32)) + r2[0].astype(F32)

    return pl.pallas_call(
        body, name=name,
        grid_spec=pltpu.PrefetchScalarGridSpec(
            num_scalar_prefetch=1, grid=(Rh // tr,),
            in_specs=[pl.BlockSpec((1, tr, Cc), lambda i, ix: (ix[0], i, 0))]
            + [pl.BlockSpec((1, tr, Cc), functools.partial(lambda i, ix, k: (k, i, 0), k=k)) for k in range(3)],
            out_specs=pl.BlockSpec((1, tr, Cc), lambda i, ix: (ix[1], i, 0))),
        out_shape=jax.ShapeDtypeStruct((2, Rh, Cc), F32),
        compiler_params=_params(("arbitrary",)),
    )(idx, h, r, r, r)


def _reduce_scatter(tag, grads, idx):
    g4 = [g.reshape(4, 2, g.shape[1] // 2, g.shape[2]) for g in grads]
    got = _swap_halves(f"rs_swap_{tag}", g4)
    halves = [_add_half(f"rs_add_half_{tag}_{a}", idx, g, r) for a, (g, r) in enumerate(zip(g4, got))]
    got = _scatter_chips(f"rs_scatter_{tag}", halves)
    parts = [_add_chips(f"rs_add_chips_{tag}_{a}", idx, h, r) for a, (h, r) in enumerate(zip(halves, got))]
    full = _join_halves(f"rs_join_{tag}", parts)
    return [f.reshape(g.shape[1], g.shape[2]) for f, g in zip(full, grads)]


def kernel(x, c, w_ada, b_ada, norm1_w, w_in, conv_dw_w, conv_dw_b, conv_ln_w, conv_ln_b, w_pw2, conv_out_norm_w, qkv_conv_w, a_log, dt_bias, dn_norm_w, w_out, norm2_w, w_up, w_down, final_ada_w, final_ada_b, final_norm_w, loss_target, m_w_ada, m_b_ada, m_norm1_w, m_w_in, m_conv_dw_w, m_conv_dw_b, m_conv_ln_w, m_conv_ln_b, m_w_pw2, m_conv_out_norm_w, m_qkv_conv_w, m_a_log, m_dt_bias, m_dn_norm_w, m_w_out, m_norm2_w, m_w_up, m_w_down, m_final_ada_w, m_final_ada_b, m_final_norm_w, v_w_ada, v_b_ada, v_norm1_w, v_w_in, v_conv_dw_w, v_conv_dw_b, v_conv_ln_w, v_conv_ln_b, v_w_pw2, v_conv_out_norm_w, v_qkv_conv_w, v_a_log, v_dt_bias, v_dn_norm_w, v_w_out, v_norm2_w, v_w_up, v_w_down, v_final_ada_w, v_final_ada_b, v_final_norm_w):
    names = ["w_ada", "b_ada", "norm1_w", "w_in", "conv_dw_w", "conv_dw_b", "conv_ln_w", "conv_ln_b", "w_pw2",
             "conv_out_norm_w", "qkv_conv_w", "a_log", "dt_bias", "dn_norm_w", "w_out", "norm2_w", "w_up", "w_down",
             "final_ada_w", "final_ada_b", "final_norm_w"]
    weights = dict(zip(names, [w_ada, b_ada, norm1_w, w_in, conv_dw_w, conv_dw_b, conv_ln_w, conv_ln_b, w_pw2,
                               conv_out_norm_w, qkv_conv_w, a_log, dt_bias, dn_norm_w, w_out, norm2_w, w_up, w_down,
                               final_ada_w, final_ada_b, final_norm_w]))
    mom1 = dict(zip(names, [m_w_ada, m_b_ada, m_norm1_w, m_w_in, m_conv_dw_w, m_conv_dw_b, m_conv_ln_w, m_conv_ln_b,
                            m_w_pw2, m_conv_out_norm_w, m_qkv_conv_w, m_a_log, m_dt_bias, m_dn_norm_w, m_w_out,
                            m_norm2_w, m_w_up, m_w_down, m_final_ada_w, m_final_ada_b, m_final_norm_w]))
    mom2 = dict(zip(names, [v_w_ada, v_b_ada, v_norm1_w, v_w_in, v_conv_dw_w, v_conv_dw_b, v_conv_ln_w, v_conv_ln_b,
                            v_w_pw2, v_conv_out_norm_w, v_qkv_conv_w, v_a_log, v_dt_bias, v_dn_norm_w, v_w_out,
                            v_norm2_w, v_w_up, v_w_down, v_final_ada_w, v_final_ada_b, v_final_norm_w]))

    B, S, D = x.shape
    T = B * S
    L = w_in.shape[0]
    C1 = conv_ln_w.shape[-1]
    NH, DH = a_log.shape[-1], dn_norm_w.shape[-1]
    DN = NH * DH
    FF = w_down.shape[1] * 4
    IN = w_in.shape[-1] * 4
    INP = 6 * C1 + LANE
    KC, KQ = conv_dw_w.shape[1], qkv_conv_w.shape[1]
    NW, NF = w_ada.shape[-1], final_ada_w.shape[-1]
    assert DH == LANE and DN == C1 and IN == 6 * C1 + 2 * NH and S % CHUNK == 0
    xi, yi, ci = _place()
    s_me, me = 2 * xi + yi, 4 * xi + 2 * yi + ci
    idx = jnp.stack([s_me, ci]).astype(jnp.int32)
    tmf, tmb = _pick(S, 256, SUB), _pick(S, 128, SUB)
    ts, tsq = _pick(S, 256, _halo(KC)), _pick(S, 512, _halo(KQ))

    shapes1 = [(B, D), conv_dw_w.shape, qkv_conv_w.shape]
    g1 = _pack([c, conv_dw_w, qkv_conv_w])
    g1 = _all_gather8("gather_cond", g1).reshape(8, -1)
    c_all, cw_all, qw_all = _unpack(g1, shapes1)
    c_all = c_all.reshape(8 * B, D)
    conv_w_full = jnp.moveaxis(cw_all[0::2], 0, 2).reshape(L, KC, C1)
    qkv_w_full = jnp.moveaxis(qw_all[0::2], 0, 2).reshape(L, KQ, 3 * DN)
    (c_act,) = _rows_fwd("cond_silu", _silu_row, [_whole(c_all)], [], [], [(D, F32)], tm=8 * B)

    mods = []
    for l in range(L):
        bsh = lax.dynamic_slice(b_ada[l], (s_me * NW,), (NW,)).reshape(1, NW)
        mods.append(_mm(f"mod_{l}", c_act, w_ada[l], "nn", bias=bsh))
    bsh = lax.dynamic_slice(final_ada_b, (s_me * NF,), (NF,)).reshape(1, NF)
    mods.append(_mm("mod_final", c_act, final_ada_w, "nn", bias=bsh))
    shapes2 = [(8 * B, NW)] * L + [(8 * B, NF)]
    g2 = _all_gather8("gather_mod", _pack(mods)).reshape(8, -1)[0::2]
    mod_all = [jnp.moveaxis(t, 0, 1).reshape(8 * B, -1) for t in _unpack(g2, shapes2)]
    mod_me = [lax.dynamic_slice(t, (B * me, 0), (B, t.shape[1])) for t in mod_all]

    def split_mod(t, n):
        return [t[:, k * D:(k + 1) * D].reshape(B, 1, D) for k in range(n)]

    wfull = []
    for l in range(L):
        sh = [w_in[l], w_pw2[l], w_out[l], w_up[l], w_down[l]]
        bufs = [_cast_into_slot(f"cast_{l}_{k}", idx, a, MXU_DTYPE) for k, a in enumerate(sh)]
        got = _gather_chips(f"gather_w_{l}", [b.reshape(4, 2, b.shape[1] // 2, b.shape[2]) for b in bufs])
        gi, gp, go, gu, gd = [g.reshape(b.shape) for g, b in zip(got, bufs)]
        wi = jnp.pad(jnp.moveaxis(gi, 0, 1).reshape(D, IN), ((0, 0), (0, INP - IN)))
        wfull.append(dict(w_in=wi, w_pw2=gp.reshape(C1, C1), w_out=go.reshape(D, D),
                          w_up=jnp.moveaxis(gu, 0, 1).reshape(D, FF), w_down=gd.reshape(FF, D)))

    pad_row = lambda v: jnp.pad(v.reshape(1, -1), ((0, 0), (NH, LANE - 2 * NH)))
    row = lambda v: v.reshape(1, -1)

    saved = []
    xcur = x.reshape(T, D)
    ycur = jnp.zeros((T, D), F32)
    gate_prev = jnp.zeros((B, 1, D), F32)
    for l in range(L):
        W = wfull[l]
        sh1, sc1, g1_, sh2, sc2, g2_ = split_mod(mod_me[l], 6)
        sv = dict(x_in=xcur, y_in=ycur, gate_in=gate_prev, sh1=sh1, sc1=sc1, g1=g1_, sh2=sh2, sc2=sc2, g2=g2_)
        x0, h1 = _rows_fwd(f"junction1_{l}", _junction, [_whole(xcur), _whole(ycur)], [gate_prev, sh1, sc1],
                           [row(norm1_w[l])], [(D, F32), (D, MXU_DTYPE)], tm=tmf, tpe=S // tmf)
        proj = _mm(f"proj_{l}", h1, W["w_in"], "nn", tn=896)
        (u0,) = _rows_fwd(f"glu_{l}", _glu, [(proj, C1, 0), (proj, C1, 1)], [], [], [(C1, F32)], tm=tmf)
        u1 = _conv_fwd(f"conv_{l}", u0, 0, conv_w_full[l], row(conv_dw_b[l]), ts=ts, tpe=S // ts)
        (u2,) = _rows_fwd(f"ln_silu_{l}", _ln_silu, [_whole(u1)], [], [row(conv_ln_w[l]), row(conv_ln_b[l])],
                          [(C1, MXU_DTYPE)], tm=tmf)
        u3 = _mm(f"pw2_{l}", u2, W["w_pw2"], "nn")
        (y_conv,) = _rows_fwd(f"conv_out_norm_{l}", _rms, [_whole(u3)], [], [row(conv_out_norm_w[l])],
                              [(C1, MXU_DTYPE)], tm=tmf)
        qkv = _conv_fwd(f"qkv_conv_{l}", proj, 2 * C1, qkv_w_full[l], None, ts=tsq, tpe=S // tsq)
        al, dtb = pad_row(a_log[l]), pad_row(dt_bias[l])
        u, w, qd, kd, gl, attn = _dn_prep_fwd(f"dn_prep_{l}", qkv, proj, 6 * C1, al, dtb, NH=NH)
        o, st = _dn_scan_fwd(f"dn_scan_{l}", u, w, qd, kd, gl, attn, NH=NH, B=B)
        (y_dn,) = _rows_fwd(f"dn_out_norm_{l}", _gated_rms, [_whole(o), (proj, DN, 5)], [], [row(dn_norm_w[l])],
                            [(DN, MXU_DTYPE)], tm=tmf, ngroups=NH)
        ycat = jnp.concatenate([y_conv, y_dn], axis=1)
        y = _mm(f"out_{l}", ycat, W["w_out"], "nn")
        x1, h2 = _rows_fwd(f"junction2_{l}", _junction, [_whole(x0), _whole(y)], [g1_, sh2, sc2],
                           [row(norm2_w[l])], [(D, F32), (D, MXU_DTYPE)], tm=tmf, tpe=S // tmf)
        up = _mm(f"up_{l}", h2, W["w_up"], "nn")
        (act,) = _rows_fwd(f"relu2_{l}", _relu2, [_whole(up)], [], [], [(FF, MXU_DTYPE)], tm=tmb)
        mlp = _mm(f"down_{l}", act, W["w_down"], "nn")
        sv.update(x0=x0, h1=h1, proj=proj, u0=u0, u1=u1, u2=u2, u3=u3, qkv=qkv, al=al, dtb=dtb, u=u, w=w, qd=qd, kd=kd,
                  gl=gl, attn=attn, o=o, st=st, ycat=ycat, y=y, x1=x1, h2=h2, up=up, act=act)
        saved.append(sv)
        xcur, ycur, gate_prev = x1, mlp, g2_

    shf, scf = split_mod(mod_me[L], 2)
    tgt = loss_target.reshape(T, D)
    rowloss, dx, dy, dgate, dshf, dscf, dfinal_norm = _rows_vjp(
        "loss_head", _final_loss, [_whole(xcur), _whole(ycur), _whole(tgt)], [gate_prev, shf, scf], [row(final_norm_w)],
        None, [F32, MXU_DTYPE, None], tm=tmb, tpe=S // tmb, primal=[(1, F32)])
    loss = lax.psum(jnp.sum(rowloss), ("x", "y", "c"))

    dmods = [None] * L
    small = [None] * L
    big_grads = {}
    for l in reversed(range(L)):
        W, sv = wfull[l], saved[l]
        dact = _mm(f"d_down_x_{l}", dy, W["w_down"], "nt")
        gw_down = _mm(f"d_down_w_{l}", sv["act"], dy, "tn", COMM_DTYPE)
        (dup,) = _rows_vjp(f"d_relu2_{l}", _relu2, [_whole(sv["up"])], [], [], [_whole(dact)], [MXU_DTYPE], tm=tmb)
        dh2 = _mm(f"d_up_x_{l}", dup, W["w_up"], "nt")
        gw_up = _mm(f"d_up_w_{l}", sv["h2"], dup, "tn", COMM_DTYPE)
        dx0, dyo, dg1, dsh2, dsc2, dn2 = _rows_vjp(
            f"d_junction2_{l}", _junction, [_whole(sv["x0"]), _whole(sv["y"])], [sv["g1"], sv["sh2"], sv["sc2"]],
            [row(norm2_w[l])], [_whole(dx), _whole(dh2)], [F32, MXU_DTYPE], tm=tmb, tpe=S // tmb)
        dycat = _mm(f"d_out_x_{l}", dyo, W["w_out"], "nt")
        gw_out = _mm(f"d_out_w_{l}", sv["ycat"], dyo, "tn", COMM_DTYPE)
        proj = sv["proj"]
        do, dz, ddn = _rows_vjp(f"d_dn_out_norm_{l}", _gated_rms, [_whole(sv["o"]), (proj, DN, 5)], [],
                                [row(dn_norm_w[l])], [(dycat, DN, 1)], [F32, MXU_DTYPE], tm=tmb, ngroups=NH)
        cts = _dn_scan_bwd(f"d_dn_scan_{l}", do, sv["u"], sv["w"], sv["qd"], sv["kd"], sv["gl"], sv["attn"], sv["st"],
                           NH=NH, B=B)
        dq, dk, dv, dba, dal, ddt = _dn_prep_bwd(f"d_dn_prep_{l}", sv["qkv"], proj, 6 * C1, sv["al"], sv["dtb"],
                                                 [cts[0], cts[1], cts[2], cts[3], cts[4], cts[5]], F32, NH=NH)
        dqkv, gqw = [], []
        for k, dpart in enumerate((dq, dk, dv)):
            dxp, dwp, _ = _conv_bwd(f"d_qkv_conv_{l}_{k}", proj, (2 + k) * C1, dpart,
                                    qkv_w_full[l][:, k * DN:(k + 1) * DN], MXU_DTYPE, ts=tsq, tpe=S // tsq)
            dqkv.append(dxp)
            gqw.append(dwp)
        (du3, dcon) = _rows_vjp(f"d_conv_out_norm_{l}", _rms, [_whole(sv["u3"])], [], [row(conv_out_norm_w[l])],
                                [(dycat, C1, 0)], [MXU_DTYPE], tm=tmb)
        du2 = _mm(f"d_pw2_x_{l}", du3, W["w_pw2"], "nt")
        gw_pw2 = _mm(f"d_pw2_w_{l}", sv["u2"], du3, "tn", COMM_DTYPE)
        du1, dlnw, dlnb = _rows_vjp(f"d_ln_silu_{l}", _ln_silu, [_whole(sv["u1"])], [],
                                    [row(conv_ln_w[l]), row(conv_ln_b[l])], [_whole(du2)], [F32], tm=tmb)
        du0, gcw, gcb = _conv_bwd(f"d_conv_{l}", sv["u0"], 0, du1, conv_w_full[l], F32, ts=ts, tpe=S // ts)
        dval, dgate_c = _rows_vjp(f"d_glu_{l}", _glu, [(proj, C1, 0), (proj, C1, 1)], [], [], [_whole(du0)],
                                  [MXU_DTYPE, MXU_DTYPE], tm=tmb)
        dproj = jnp.concatenate([dval, dgate_c] + dqkv + [dz, dba.astype(MXU_DTYPE)], axis=1)
        dh1 = _mm(f"d_proj_x_{l}", dproj, W["w_in"], "nt", tk=896)
        gw_in = _mm(f"d_proj_w_{l}", sv["h1"], dproj, "tn", COMM_DTYPE, tn=896)[:, :IN]
        dxn, dyn, dg2p, dsh1, dsc1, dn1 = _rows_vjp(
            f"d_junction1_{l}", _junction, [_whole(sv["x_in"]), _whole(sv["y_in"])], [sv["gate_in"], sv["sh1"], sv["sc1"]],
            [row(norm1_w[l])], [_whole(dx0), _whole(dh1)], [F32, MXU_DTYPE], tm=tmb, tpe=S // tmb)
        dmods[l] = [dsh1, dsc1, dg1, dsh2, dsc2, dgate]
        small[l] = dict(norm1_w=dn1, conv_dw_w=gcw, conv_dw_b=gcb, conv_ln_w=dlnw, conv_ln_b=dlnb, conv_out_norm_w=dcon,
                        qkv_conv_w=jnp.concatenate(gqw, axis=1), a_log=dal[:, NH:2 * NH], dt_bias=ddt[:, NH:2 * NH],
                        dn_norm_w=ddn, norm2_w=dn2)
        shard_major = [jnp.moveaxis(gw_in.reshape(D, 4, IN // 4), 1, 0), gw_pw2.reshape(4, C1 // 4, C1),
                       gw_out.reshape(4, D // 4, D), jnp.moveaxis(gw_up.reshape(D, 4, FF // 4), 1, 0),
                       gw_down.reshape(4, FF // 4, D)]
        big_grads[l] = _reduce_scatter(str(l), shard_major, idx)
        dx, dy, dgate = dxn, dyn, dg2p

    grad_x = dx.reshape(B, S, D)

    small_names = ["norm1_w", "conv_dw_w", "conv_dw_b", "conv_ln_w", "conv_ln_b", "conv_out_norm_w", "qkv_conv_w", "a_log",
                   "dt_bias", "dn_norm_w", "norm2_w"]
    dmod_flat = jnp.concatenate([jnp.concatenate([t.reshape(B, D) for t in dmods[l]], axis=1) for l in range(L)]
                                + [dshf.reshape(B, D), dscf.reshape(B, D)], axis=1)
    small_list = [small[l][n] for l in range(L) for n in small_names] + [dfinal_norm]
    shapes3 = [dmod_flat.shape] + [t.shape for t in small_list]
    g3 = _all_gather8("gather_grads", _pack([dmod_flat] + small_list))
    rows3 = g3.shape[0] // 8
    g3 = g3.reshape(8, rows3, PACK_LANES)
    summed = _unpack(_sum0("sum_small_grads", g3).reshape(-1), shapes3)[1:]
    dmod_all = _unpack(g3.reshape(8, -1), shapes3[:1])[0].reshape(8 * B, -1)
    nm = dmod_all.shape[1]
    grad_b_all = _sum0("sum_mod_grads", dmod_all.reshape(8 * B, nm // PACK_LANES, PACK_LANES)).reshape(-1)
    grads = {}
    gwa = []
    for l in range(L):
        dm = lax.dynamic_slice(dmod_all, (0, l * 6 * D + s_me * NW), (8 * B, NW))
        gwa.append(_mm(f"d_ada_w_{l}", c_act, dm, "tn"))
    grads["w_ada"] = jnp.stack(gwa)
    grads["b_ada"] = grad_b_all[:L * 6 * D].reshape(L, 6 * D)
    dm = lax.dynamic_slice(dmod_all, (0, L * 6 * D + s_me * NF), (8 * B, NF))
    grads["final_ada_w"] = _mm("d_final_ada_w", c_act, dm, "tn")
    grads["final_ada_b"] = grad_b_all[L * 6 * D:]
    per_layer = {n: [] for n in small_names}
    for l in range(L):
        for k, n in enumerate(small_names):
            per_layer[n].append(summed[l * len(small_names) + k])
    for n in small_names:
        t = jnp.stack(per_layer[n])
        if n == "conv_dw_w":
            t = lax.dynamic_slice(t, (0, 0, s_me * (C1 // 4)), (L, KC, C1 // 4))
        elif n == "qkv_conv_w":
            t = lax.dynamic_slice(t, (0, 0, s_me * (3 * DN // 4)), (L, KQ, 3 * DN // 4))
        grads[n] = t.reshape(weights[n].shape)
    grads["final_norm_w"] = summed[-1].reshape(final_norm_w.shape)
    for k, n in enumerate(["w_in", "w_pw2", "w_out", "w_up", "w_down"]):
        grads[n] = jnp.stack([big_grads[l][k] for l in range(L)])

    delta, new_m, new_v = {}, {}, {}
    big_names = ["w_ada", "w_in", "w_pw2", "w_out", "w_up", "w_down", "final_ada_w"]
    for n in big_names:
        shp = weights[n].shape
        two = lambda t: t.reshape(-1, shp[-1])
        d_, m_, v_ = _adamw(f"adamw_{n}", two(weights[n]), two(grads[n]), two(mom1[n]), two(mom2[n]))
        delta[n], new_m[n], new_v[n] = d_.reshape(shp), m_.reshape(shp), v_.reshape(shp)
    rest = [n for n in names if n not in big_names]
    rshapes = [weights[n].shape for n in rest]
    packed = [_pack([d[n] for n in rest]) for d in (weights, grads, mom1, mom2)]
    outs = _adamw("adamw_small", *packed)
    for dst, arr in zip((delta, new_m, new_v), outs):
        for n, t in zip(rest, _unpack(arr.reshape(-1), rshapes)):
            dst[n] = t

    return (loss, grad_x, *[grads[n] for n in names], *[delta[n] for n in names], *[new_m[n] for n in names],
            *[new_v[n] for n in names])
```

```python
import functools
import math
import typing

import jax
import jax.numpy as jnp
from jax import lax
from jax.experimental import pallas as pl
from jax.experimental.pallas import tpu as pltpu

F32 = jnp.float32
MXU_DTYPE = jnp.bfloat16
COMM_DTYPE = jnp.bfloat16
HI = lax.Precision.HIGHEST
CHUNK = 64
PAIR = 2 * CHUNK
EPS = 1e-6
LANE = 128
SUB = 8
PACK_LANES = 1024
VMEM_LIMIT = 56 * 1024 * 1024
ADAM_LR, ADAM_B1, ADAM_B2, ADAM_EPS, ADAM_WD, ADAM_STEP = 0.001, 0.9, 0.999, 1e-08, 0.01, 10
MESH = pl.DeviceIdType.MESH
ANY = pl.BlockSpec(memory_space=pl.ANY)
NN, NT, TN = ((1,), (0,)), ((1,), (1,)), ((0,), (0,))


def _pick(dim, pref, mult):
    for t in range(min(pref, dim), 0, -1):
        if dim % t == 0 and t % mult == 0:
            return t
    return dim


def _params(sem):
    return pltpu.CompilerParams(dimension_semantics=sem, vmem_limit_bytes=VMEM_LIMIT)


def _sigmoid(v):
    return 1.0 / (1.0 + jnp.exp(-v))


def _silu(v):
    return v * _sigmoid(v)


def _row_specs(rows, exps, gls, tm, tpe):
    specs = [pl.BlockSpec((tm, w), functools.partial(lambda i, j: (i, j), j=cb)) for _, w, cb in rows]
    specs += [pl.BlockSpec((1, 1, e.shape[-1]), lambda i: (i // tpe, 0, 0)) for e in exps]
    specs += [pl.BlockSpec((1, g.shape[-1]), lambda i: (0, 0)) for g in gls]
    return specs


def _rows_fwd(name, fn, rows, exps, gls, outs, *, tm, tpe=1, ngroups=1):
    T = rows[0][0].shape[0]
    nr, ne, ng = len(rows), len(exps), len(gls)

    def body(*refs):
        r, e, g, o = refs[:nr], refs[nr:nr + ne], refs[nr + ne:nr + ne + ng], refs[nr + ne + ng:]
        ev = [t[0].astype(F32) for t in e]
        gv = [t[...].astype(F32) for t in g]
        for k in range(ngroups):
            rv = [t[:, k * (w // ngroups):(k + 1) * (w // ngroups)].astype(F32) for t, (_, w, _) in zip(r, rows)]
            res = fn(*rv, *ev, *gv)
            for oref, val, (w, dt) in zip(o, res, outs):
                gw = w // ngroups
                oref[:, k * gw:(k + 1) * gw] = val.astype(dt)

    return pl.pallas_call(
        body, name=name, grid=(T // tm,),
        in_specs=_row_specs(rows, exps, gls, tm, tpe),
        out_specs=[pl.BlockSpec((tm, w), lambda i: (i, 0)) for w, _ in outs],
        out_shape=[jax.ShapeDtypeStruct((T, w), dt) for w, dt in outs],
        compiler_params=_params(("arbitrary",)),
    )(*[a for a, _, _ in rows], *exps, *gls)


def _rows_vjp(name, fn, rows, exps, gls, cts, row_dtypes, *, tm, tpe=1, ngroups=1, primal=None):
    T = rows[0][0].shape[0]
    nr, ne, ng = len(rows), len(exps), len(gls)
    nc = 0 if cts is None else len(cts)
    keep = [k for k, dt in enumerate(row_dtypes) if dt is not None]
    npr = 0 if primal is None else len(primal)

    def body(*refs):
        r, e, g = refs[:nr], refs[nr:nr + ne], refs[nr + ne:nr + ne + ng]
        c = refs[nr + ne + ng:nr + ne + ng + nc]
        o = refs[nr + ne + ng + nc:]
        po, ro, eo, go = o[:npr], o[npr:npr + len(keep)], o[npr + len(keep):npr + len(keep) + ne], o[npr + len(keep) + ne:]
        i = pl.program_id(0)
        ev = [t[0].astype(F32) for t in e]
        gv = [t[...].astype(F32) for t in g]
        esum = [jnp.zeros_like(v) for v in ev]
        gsum = [jnp.zeros_like(v) for v in gv]
        for k in range(ngroups):
            rv = [t[:, k * (w // ngroups):(k + 1) * (w // ngroups)].astype(F32) for t, (_, w, _) in zip(r, rows)]
            res, pull = jax.vjp(fn, *rv, *ev, *gv)
            if cts is None:
                ct = tuple(jnp.ones_like(v) for v in res)
            else:
                ct = tuple(t[:, k * (w // ngroups):(k + 1) * (w // ngroups)].astype(F32) for t, (_, w, _) in zip(c, cts))
            grads = pull(ct)
            for oref, val, (w, dt) in zip(po, res, primal or ()):
                gw = w // ngroups
                oref[:, k * gw:(k + 1) * gw] = val.astype(dt)
            for oref, idx in zip(ro, keep):
                gw = rows[idx][1] // ngroups
                oref[:, k * gw:(k + 1) * gw] = grads[idx].astype(row_dtypes[idx])
            esum = [s + d for s, d in zip(esum, grads[nr:nr + ne])]
            gsum = [s + d for s, d in zip(gsum, grads[nr + ne:])]

        if ne:
            @pl.when(i % tpe == 0)
            def _():
                for oref in eo:
                    oref[...] = jnp.zeros_like(oref)
            for oref, s in zip(eo, esum):
                oref[0] += s
        if ng:
            @pl.when(i == 0)
            def _():
                for oref in go:
                    oref[...] = jnp.zeros_like(oref)
            for oref, s in zip(go, gsum):
                oref[...] += s

    out_specs = [pl.BlockSpec((tm, w), lambda i: (i, 0)) for w, _ in (primal or ())]
    out_shape = [jax.ShapeDtypeStruct((T, w), dt) for w, dt in (primal or ())]
    out_specs += [pl.BlockSpec((tm, rows[k][1]), lambda i: (i, 0)) for k in keep]
    out_shape += [jax.ShapeDtypeStruct((T, rows[k][1]), row_dtypes[k]) for k in keep]
    out_specs += [pl.BlockSpec((1, 1, e.shape[-1]), lambda i: (i // tpe, 0, 0)) for e in exps]
    out_shape += [jax.ShapeDtypeStruct(e.shape, F32) for e in exps]
    out_specs += [pl.BlockSpec((1, g.shape[-1]), lambda i: (0, 0)) for g in gls]
    out_shape += [jax.ShapeDtypeStruct(g.shape, F32) for g in gls]
    ct_specs = [] if cts is None else [pl.BlockSpec((tm, w), functools.partial(lambda i, j: (i, j), j=cb)) for _, w, cb in cts]
    ct_arrs = [] if cts is None else [a for a, _, _ in cts]
    return pl.pallas_call(
        body, name=name, grid=(T // tm,),
        in_specs=_row_specs(rows, exps, gls, tm, tpe) + ct_specs,
        out_specs=out_specs, out_shape=out_shape,
        compiler_params=_params(("arbitrary",)),
    )(*[a for a, _, _ in rows], *exps, *gls, *ct_arrs)


def _whole(a):
    return (a, a.shape[-1], 0)


def _junction(x, y, gate, shift, scale, w):
    xn = x + gate * y
    r = lax.rsqrt(jnp.mean(xn * xn, axis=-1, keepdims=True) + EPS)
    return xn, (xn * r * w) * (1.0 + scale) + shift


def _final_loss(x, y, tgt, gate, shift, scale, w):
    _, out = _junction(x, y, gate, shift, scale, w)
    err = out - tgt
    return (0.5 * jnp.mean(err * err, axis=-1, keepdims=True),)


def _glu(val, gate):
    return (val * _sigmoid(gate),)


def _ln_silu(u, w, b):
    xc = u - jnp.mean(u, axis=-1, keepdims=True)
    y = xc * lax.rsqrt(jnp.mean(xc * xc, axis=-1, keepdims=True) + EPS) * w + b
    return (_silu(y),)


def _rms(u, w):
    return (u * lax.rsqrt(jnp.mean(u * u, axis=-1, keepdims=True) + EPS) * w,)


def _gated_rms(o, z, w):
    return (o * lax.rsqrt(jnp.mean(o * o, axis=-1, keepdims=True) + EPS) * w * _silu(z),)


def _relu2(u):
    r = jnp.maximum(u, 0.0)
    return (r * r,)


def _silu_row(u):
    return (_silu(u),)


class _Comm(typing.NamedTuple):
    ins: list
    outs: list
    aliases: dict
    nsem: int
    plan: typing.Callable


def _attach(comm, body, kw, args, first_last):
    if comm is None:
        return body, kw, args
    ni0, no0, ns0 = len(kw["in_specs"]), len(kw["out_specs"]), len(kw["scratch_shapes"])
    ni, no = len(comm.ins), len(comm.outs)
    kw = dict(kw, in_specs=kw["in_specs"] + [ANY] * ni, out_specs=kw["out_specs"] + [ANY] * no,
              out_shape=kw["out_shape"] + comm.outs,
              scratch_shapes=kw["scratch_shapes"] + [pltpu.SemaphoreType.DMA((comm.nsem,)), pltpu.SemaphoreType.DMA((comm.nsem,))],
              input_output_aliases={ni0 + k: no0 + v for k, v in comm.aliases.items()})

    def carrying(*refs):
        own_in, c_in = refs[:ni0], refs[ni0:ni0 + ni]
        own_out, c_out = refs[ni0 + ni:ni0 + ni + no0], refs[ni0 + ni + no0:ni0 + ni + no0 + no]
        scratch = refs[ni0 + ni + no0 + no:]
        ssem, rsem = scratch[ns0], scratch[ns0 + 1]
        first, last = first_last()

        @pl.when(first)
        def _():
            for cp in comm.plan(c_in, c_out, ssem, rsem, True):
                cp.start()

        body(*own_in, *own_out, *scratch[:ns0])

        @pl.when(last)
        def _():
            sends, recvs = comm.plan(c_in, c_out, ssem, rsem, False)
            for cp in sends:
                cp.wait_send()
            for cp in recvs:
                cp.wait_recv()

    return carrying, kw, args + list(comm.ins)


def _run_comm(name, comm):
    ni, no = len(comm.ins), len(comm.outs)

    def body(*refs):
        for cp in comm.plan(refs[:ni], refs[ni:ni + no], refs[-2], refs[-1], True):
            cp.start()
        sends, recvs = comm.plan(refs[:ni], refs[ni:ni + no], refs[-2], refs[-1], False)
        for cp in sends:
            cp.wait_send()
        for cp in recvs:
            cp.wait_recv()

    return pl.pallas_call(
        body, name=name, in_specs=[ANY] * ni, out_specs=[ANY] * no, out_shape=comm.outs,
        input_output_aliases=comm.aliases,
        scratch_shapes=[pltpu.SemaphoreType.DMA((comm.nsem,)), pltpu.SemaphoreType.DMA((comm.nsem,))],
    )(*comm.ins)


def _mm(name, a, b, mode, out_dtype=F32, bias=None, tm=1024, tn=1024, tk=2048, comm=None, post=None):
    if mode == "nn":
        (M, K), N = a.shape, b.shape[1]
    elif mode == "nt":
        (M, K), N = a.shape, b.shape[0]
    else:
        (K, M), N = a.shape, b.shape[1]
    tm, tn, tk = _pick(M, tm, LANE), _pick(N, tn, LANE), _pick(K, tk, LANE)
    nk = K // tk
    grid = (M // tm, N // tn, nk)
    dn = {"nn": NN, "nt": NT, "tn": TN}[mode]
    a_spec = pl.BlockSpec((tk, tm), lambda i, j, k: (k, i)) if mode == "tn" else pl.BlockSpec((tm, tk), lambda i, j, k: (i, k))
    b_spec = pl.BlockSpec((tn, tk), lambda i, j, k: (j, k)) if mode == "nt" else pl.BlockSpec((tk, tn), lambda i, j, k: (k, j))
    specs, args = [a_spec, b_spec], [a, b]
    if bias is not None:
        specs.append(pl.BlockSpec((1, tn), lambda i, j, k: (0, j)))
        args.append(bias)

    o_spec = pl.BlockSpec((tm, tn), lambda i, j, k: (i, j))
    fn, extra, out_dtypes = post if post is not None else (lambda t: (t,), [], [out_dtype])
    specs += [o_spec] * len(extra)
    args += list(extra)
    nin, nout = len(args), len(out_dtypes)

    def body(*refs):
        a_ref, b_ref = refs[0], refs[1]
        outs, acc = refs[nin:nin + nout], refs[-1]
        k = pl.program_id(2)

        @pl.when(k == 0)
        def _():
            acc[...] = jnp.zeros_like(acc)

        acc[...] += lax.dot_general(a_ref[...].astype(MXU_DTYPE), b_ref[...].astype(MXU_DTYPE), (dn, ((), ())),
                                    preferred_element_type=F32)

        @pl.when(k == nk - 1)
        def _():
            res = acc[...]
            if bias is not None:
                res = res + refs[2][...]
            for o_ref, val in zip(outs, fn(res, *[r[...] for r in refs[nin - len(extra):nin]])):
                o_ref[...] = val.astype(o_ref.dtype)

    def first_last():
        at = [pl.program_id(d) for d in range(3)]
        first = jnp.logical_and(jnp.logical_and(at[0] == 0, at[1] == 0), at[2] == 0)
        last = jnp.logical_and(jnp.logical_and(at[0] == grid[0] - 1, at[1] == grid[1] - 1), at[2] == grid[2] - 1)
        return first, last

    kw = dict(in_specs=specs, out_specs=[o_spec] * nout, out_shape=[jax.ShapeDtypeStruct((M, N), dt) for dt in out_dtypes],
              scratch_shapes=[pltpu.VMEM((tm, tn), F32)])
    body, kw, args = _attach(comm, body, kw, args, first_last)
    sem = ("arbitrary",) * 3 if comm is not None else ("parallel", "parallel", "arbitrary")
    res = pl.pallas_call(body, name=name, grid=grid, compiler_params=_params(sem), **kw)(*args)
    main = res[0] if nout == 1 else res[:nout]
    return main if comm is None else (main, res[nout:])


def _halo(K):
    return SUB * -(-(K - 1) // SUB)


def _conv_fwd(name, x, col0, w, bias, *, ts, tpe):
    T = x.shape[0]
    K, C = w.shape
    H = _halo(K)
    cb = _pick(C, 256, LANE)
    rb = _pick(ts, 64, SUB)
    off = col0 // cb
    specs = [pl.BlockSpec((ts, cb), lambda j, i: (i, off + j)),
             pl.BlockSpec((H, cb), lambda j, i: (jnp.maximum(i * (ts // H) - 1, 0), off + j)),
             pl.BlockSpec((K, cb), lambda j, i: (0, j))]
    args = [x, x, w]
    if bias is not None:
        specs.append(pl.BlockSpec((1, cb), lambda j, i: (0, j)))
        args.append(bias)

    def body(*refs):
        cur, halo, w_ref = refs[:3]
        o_ref, xp = refs[-2], refs[-1]
        i = pl.program_id(1)
        xp[0:H, :] = jnp.where(i % tpe == 0, 0.0, halo[...].astype(F32))
        xp[H:H + ts, :] = cur[...].astype(F32)
        for r0 in range(0, ts, rb):
            acc = jnp.zeros((rb, cb), F32) if bias is None else jnp.zeros((rb, cb), F32) + refs[3][...]
            for j in range(K):
                lo = H - (K - 1) + j + r0
                acc = acc + w_ref[j:j + 1, :] * xp[lo:lo + rb, :]
            o_ref[r0:r0 + rb, :] = acc

    return pl.pallas_call(
        body, name=name, grid=(C // cb, T // ts), in_specs=specs,
        out_specs=pl.BlockSpec((ts, cb), lambda j, i: (i, j)),
        out_shape=jax.ShapeDtypeStruct((T, C), F32),
        scratch_shapes=[pltpu.VMEM((H + ts, cb), F32)],
        compiler_params=_params(("parallel", "arbitrary")),
    )(*args)


def _conv_bwd(name, x, col0, dy, w, out_dtype, *, ts, tpe):
    T = x.shape[0]
    K, C = w.shape
    H = _halo(K)
    cb = _pick(C, 256, LANE)
    rb = _pick(ts, 64, SUB)
    off = col0 // cb
    nt = T // ts

    def body(cur, halo, dyc, dyn, w_ref, dx_ref, dw_ref, db_ref, xp, dyp):
        i = pl.program_id(1)
        xp[0:H, :] = jnp.where(i % tpe == 0, 0.0, halo[...].astype(F32))
        xp[H:H + ts, :] = cur[...].astype(F32)
        dyp[0:ts, :] = dyc[...]
        dyp[ts:ts + H, :] = jnp.where(i % tpe == tpe - 1, 0.0, dyn[...])
        for r0 in range(0, ts, rb):
            acc = jnp.zeros((rb, cb), F32)
            for j in range(K):
                lo = K - 1 - j + r0
                acc = acc + w_ref[j:j + 1, :] * dyp[lo:lo + rb, :]
            dx_ref[r0:r0 + rb, :] = acc.astype(out_dtype)

        @pl.when(i == 0)
        def _():
            dw_ref[...] = jnp.zeros_like(dw_ref)
            db_ref[...] = jnp.zeros_like(db_ref)

        for j in range(K):
            part = jnp.zeros((1, cb), F32)
            for r0 in range(0, ts, rb):
                lo = H - (K - 1) + j + r0
                part = part + jnp.sum(dyp[r0:r0 + rb, :] * xp[lo:lo + rb, :], axis=0, keepdims=True)
            dw_ref[j:j + 1, :] += part
        db_ref[...] += jnp.sum(dyc[...], axis=0, keepdims=True)

    return pl.pallas_call(
        body, name=name, grid=(C // cb, nt),
        in_specs=[pl.BlockSpec((ts, cb), lambda j, i: (i, off + j)),
                  pl.BlockSpec((H, cb), lambda j, i: (jnp.maximum(i * (ts // H) - 1, 0), off + j)),
                  pl.BlockSpec((ts, cb), lambda j, i: (i, j)),
                  pl.BlockSpec((H, cb), lambda j, i: (jnp.minimum((i + 1) * (ts // H), T // H - 1), j)),
                  pl.BlockSpec((K, cb), lambda j, i: (0, j))],
        out_specs=[pl.BlockSpec((ts, cb), lambda j, i: (i, j)),
                   pl.BlockSpec((K, cb), lambda j, i: (0, j)),
                   pl.BlockSpec((1, cb), lambda j, i: (0, j))],
        out_shape=[jax.ShapeDtypeStruct((T, C), out_dtype), jax.ShapeDtypeStruct((K, C), F32),
                   jax.ShapeDtypeStruct((1, C), F32)],
        scratch_shapes=[pltpu.VMEM((H + ts, cb), F32), pltpu.VMEM((ts + H, cb), F32)],
        compiler_params=_params(("parallel", "arbitrary")),
    )(x, x, dy, dy, w)


def _hdot(a, b, dn):
    return lax.dot_general(a, b, (dn, ((), ())), precision=HI, preferred_element_type=F32)


def _bdot(a, b, dn):
    return lax.dot_general(a.astype(MXU_DTYPE), b.astype(MXU_DTYPE), (dn, ((), ())), preferred_element_type=F32)


def _split(a):
    hi = a.astype(MXU_DTYPE)
    return hi, (a - hi.astype(F32)).astype(MXU_DTYPE)


def _dot3_raw(a, b, dn):
    if MXU_DTYPE == F32:
        return _hdot(a, b, dn)
    (ah, al), (bh, bl) = _split(a), _split(b)
    d = lambda p, q: lax.dot_general(p, q, (dn, ((), ())), preferred_element_type=F32)
    return d(ah, bh) + (d(ah, bl) + d(al, bh))


def _with_vjp(raw):
    dn = {"nn": NN, "nt": NT, "tn": TN}

    @functools.partial(jax.custom_vjp, nondiff_argnums=(2,))
    def dot(a, b, mode):
        return raw(a, b, dn[mode])

    def fwd(a, b, mode):
        return raw(a, b, dn[mode]), (a, b)

    def bwd(mode, res, ct):
        a, b = res
        if mode == "nn":
            return raw(ct, b, NT), raw(a, ct, TN)
        if mode == "nt":
            return raw(ct, b, NN), raw(ct, a, TN)
        return raw(b, ct, NT), raw(a, ct, NN)

    dot.defvjp(fwd, bwd)
    return dot


_dot_exact = _with_vjp(_hdot)
_dot3 = _with_vjp(_dot3_raw)
_dot1 = _with_vjp(_bdot)


@jax.custom_vjp
def _tri_inv(ns):
    C = ns[0].shape[0]
    eye = (lax.broadcasted_iota(jnp.int32, (C, C), 0) == lax.broadcasted_iota(jnp.int32, (C, C), 1)).astype(F32)
    ts = [eye + n for n in ns]
    ps = list(ns)
    for _ in range(int(math.log2(CHUNK)) - 1):
        ps = [_dot3_raw(p, p, NN) for p in ps]
        ts = [t + _dot3_raw(t, p, NN) for t, p in zip(ts, ps)]
    return tuple(ts)


def _tri_inv_fwd(ns):
    ts = _tri_inv(ns)
    return ts, ts


def _tri_inv_bwd(ts, cts):
    xs = [_dot3_raw(t, ct, TN) for t, ct in zip(ts, cts)]
    return (tuple(_dot3_raw(x, t, NT) for x, t in zip(xs, ts)),)


_tri_inv.defvjp(_tri_inv_fwd, _tri_inv_bwd)


def _dn_prep(qs, ks, vs, ba, alog, dtb, *, nh):
    C2, Dk = qs[0].shape
    z = ba + dtb
    g_all = -jnp.exp(alog) * (jnp.maximum(z, 0.0) + jnp.log(1.0 + jnp.exp(-jnp.abs(z))))
    ri = lax.broadcasted_iota(jnp.int32, (C2, C2), 0)
    cj = lax.broadcasted_iota(jnp.int32, (C2, C2), 1)
    same = jnp.logical_not(jnp.logical_xor(ri >= CHUNK, cj >= CHUNK))
    causal, strict = jnp.logical_and(ri >= cj, same), jnp.logical_and(ri > cj, same)
    gc_all = _dot_exact(causal.astype(F32), g_all, "nn")
    gc_rows = _dot_exact(g_all, jnp.logical_and(ri <= cj, same).astype(F32), "tn")
    gl_all = _dot_exact(same.astype(F32), g_all, "nn")
    lane = lax.broadcasted_iota(jnp.int32, (1, ba.shape[1]), 1)
    subl = lax.broadcasted_iota(jnp.int32, (ba.shape[1], 1), 0)
    heads = range(nh)
    sel = [(lane == nh + h).astype(F32) for h in heads]
    gc = [jnp.sum(gc_all * s, axis=1, keepdims=True) for s in sel]
    gl = [jnp.sum(gl_all * s, axis=1, keepdims=True) for s in sel]
    gcr = [jnp.sum(gc_rows * (subl == nh + h).astype(F32), axis=0, keepdims=True) for h in heads]
    decay = [jnp.where(causal, jnp.exp(jnp.where(causal, a - b, 0.0)), 0.0) for a, b in zip(gc, gcr)]
    beta = [_sigmoid(jnp.sum(ba * (lane == h).astype(F32), axis=1, keepdims=True)) for h in heads]
    q = [_silu(t) for t in qs]
    q = [t * lax.rsqrt(jnp.sum(t * t, axis=-1, keepdims=True) + EPS) * (Dk ** -0.5) for t in q]
    k = [_silu(t) for t in ks]
    k = [t * lax.rsqrt(jnp.sum(t * t, axis=-1, keepdims=True) + EPS) for t in k]
    kb = [a * b for a, b in zip(k, beta)]
    vb = [_silu(a) * b for a, b in zip(vs, beta)]
    kk = [_dot1(a, b, "nt") for a, b in zip(kb, k)]
    qk = [_dot1(a, b, "nt") for a, b in zip(q, k)]
    t = _tri_inv(tuple(-jnp.where(strict, a * d, 0.0) for a, d in zip(kk, decay)))
    egc = [jnp.exp(a) for a in gc]
    u = [_dot3(a, b, "nn") for a, b in zip(t, vb)]
    w = [_dot3(a, b * e, "nn") for a, b, e in zip(t, kb, egc)]
    attn = [jnp.where(causal, a * d, 0.0) for a, d in zip(qk, decay)]
    qd = [a * e for a, e in zip(q, egc)]
    kd = [a * jnp.exp(b - c) for a, b, c in zip(k, gl, gc)]
    glb = [jnp.exp(a) * jnp.ones((1, Dk), F32) for a in gl]
    return tuple(u), tuple(w), tuple(qd), tuple(kd), tuple(attn), tuple(glb)


def _dn_prep_specs(DN, col_q):
    qs = [pl.BlockSpec((PAIR, DN), functools.partial(lambda i, j: (i, j), j=j)) for j in range(3)]
    return qs + [pl.BlockSpec((PAIR, LANE), lambda i: (i, col_q)), pl.BlockSpec((1, LANE), lambda i: (0, 0)),
                 pl.BlockSpec((1, LANE), lambda i: (0, 0))]


def _heads(ref, NH, Dk):
    return tuple(ref[:, h * Dk:(h + 1) * Dk] for h in range(NH))


def _dn_prep_fwd(name, qkv, proj, ba_col, alog, dtb, *, NH, comm=None):
    T = qkv.shape[0]
    DN = qkv.shape[1] // 3
    Dk = DN // NH
    nsteps = T // PAIR

    def body(q_ref, k_ref, v_ref, ba_ref, al_ref, dt_ref, u_ref, w_ref, qd_ref, kd_ref, gl_ref, at_ref):
        res = _dn_prep(_heads(q_ref, NH, Dk), _heads(k_ref, NH, Dk), _heads(v_ref, NH, Dk), ba_ref[...], al_ref[...],
                       dt_ref[...], nh=NH)
        for h in range(NH):
            sl = slice(h * Dk, (h + 1) * Dk)
            u_ref[:, sl], w_ref[:, sl], qd_ref[:, sl], kd_ref[:, sl] = res[0][h], res[1][h], res[2][h], res[3][h]
            at_ref[h] = res[4][h]
            gl_ref[:, sl] = res[5][h]

    big = pl.BlockSpec((PAIR, DN), lambda i: (i, 0))
    kw = dict(in_specs=_dn_prep_specs(DN, ba_col // LANE),
              out_specs=[big] * 5 + [pl.BlockSpec((NH, PAIR, PAIR), lambda i: (0, i, 0))],
              out_shape=[jax.ShapeDtypeStruct((T, DN), F32)] * 5 + [jax.ShapeDtypeStruct((NH, T, PAIR), F32)],
              scratch_shapes=[])
    body, kw, args = _attach(comm, body, kw, [qkv, qkv, qkv, proj, alog, dtb],
                             lambda: (pl.program_id(0) == 0, pl.program_id(0) == nsteps - 1))
    res = pl.pallas_call(body, name=name, grid=(nsteps,), compiler_params=_params(("arbitrary",)), **kw)(*args)
    return res[:6], res[6:]


def _dn_prep_bwd(name, qkv, proj, ba_col, alog, dtb, cts, out_dtype, *, NH, comm=None):
    T = qkv.shape[0]
    DN = qkv.shape[1] // 3
    Dk = DN // NH
    nsteps = T // PAIR

    def body(q_ref, k_ref, v_ref, ba_ref, al_ref, dt_ref, du, dw, dqd, dkd, dgl, dat,
             dq_ref, dk_ref, dv_ref, dba_ref, dal_ref, ddt_ref):
        i = pl.program_id(0)
        _, pull = jax.vjp(functools.partial(_dn_prep, nh=NH), _heads(q_ref, NH, Dk), _heads(k_ref, NH, Dk),
                          _heads(v_ref, NH, Dk), ba_ref[...], al_ref[...], dt_ref[...])
        gq, gk, gv, gba, gal, gdt = pull((_heads(du, NH, Dk), _heads(dw, NH, Dk), _heads(dqd, NH, Dk),
                                          _heads(dkd, NH, Dk), tuple(dat[h] for h in range(NH)), _heads(dgl, NH, Dk)))
        for h in range(NH):
            sl = slice(h * Dk, (h + 1) * Dk)
            dq_ref[:, sl], dk_ref[:, sl], dv_ref[:, sl] = gq[h].astype(out_dtype), gk[h].astype(out_dtype), gv[h].astype(out_dtype)
        dba_ref[...] = gba.astype(out_dtype)

        @pl.when(i == 0)
        def _():
            dal_ref[...] = jnp.zeros_like(dal_ref)
            ddt_ref[...] = jnp.zeros_like(ddt_ref)

        dal_ref[...] += gal
        ddt_ref[...] += gdt

    big = pl.BlockSpec((PAIR, DN), lambda i: (i, 0))
    row = pl.BlockSpec((1, LANE), lambda i: (0, 0))
    kw = dict(in_specs=_dn_prep_specs(DN, ba_col // LANE) + [big] * 5 + [pl.BlockSpec((NH, PAIR, PAIR), lambda i: (0, i, 0))],
              out_specs=[big] * 3 + [pl.BlockSpec((PAIR, LANE), lambda i: (i, 0)), row, row],
              out_shape=[jax.ShapeDtypeStruct((T, DN), out_dtype)] * 3 + [jax.ShapeDtypeStruct((T, LANE), out_dtype),
                                                                           jax.ShapeDtypeStruct((1, LANE), F32),
                                                                           jax.ShapeDtypeStruct((1, LANE), F32)],
              scratch_shapes=[])
    body, kw, args = _attach(comm, body, kw, [qkv, qkv, qkv, proj, alog, dtb, *cts],
                             lambda: (pl.program_id(0) == 0, pl.program_id(0) == nsteps - 1))
    res = pl.pallas_call(body, name=name, grid=(nsteps,), compiler_params=_params(("arbitrary",)), **kw)(*args)
    return res[:6], res[6:]


def _dn_scan_fwd(name, u, w, qd, kd, gl, attn, *, NH, B):
    T, DN = u.shape
    Dk = DN // NH
    C = CHUNK
    NP = T // (B * PAIR)

    def body(u_ref, w_ref, qd_ref, kd_ref, gl_ref, at_ref, o_ref, st_ref, s_ref):
        @pl.when(pl.program_id(1) == 0)
        def _():
            s_ref[...] = jnp.zeros_like(s_ref)

        zeros = jnp.zeros((C, Dk), F32)
        for sub in range(2):
            rs = slice(sub * C, (sub + 1) * C)
            for h in range(NH):
                sl = slice(h * Dk, (h + 1) * Dk)
                s = s_ref[h]
                st_ref[sub, h] = s
                vnew = u_ref[rs, sl] - _bdot(w_ref[rs, sl], s, NN)
                vext = jnp.concatenate([vnew, zeros] if sub == 0 else [zeros, vnew], axis=0)
                o_ref[rs, sl] = _bdot(qd_ref[rs, sl], s, NN) + _bdot(at_ref[h, rs, :], vext, NN)
                s_ref[h] = s * gl_ref[sub * C:sub * C + 1, sl] + _bdot(kd_ref[rs, sl], vnew, TN)

    big = pl.BlockSpec((PAIR, DN), lambda b, n: (b * NP + n, 0))
    return pl.pallas_call(
        body, name=name, grid=(B, NP),
        in_specs=[big] * 5 + [pl.BlockSpec((NH, PAIR, PAIR), lambda b, n: (0, b * NP + n, 0))],
        out_specs=[big, pl.BlockSpec((2, NH, Dk, Dk), lambda b, n: (b * NP + n, 0, 0, 0))],
        out_shape=[jax.ShapeDtypeStruct((T, DN), F32), jax.ShapeDtypeStruct((T // C, NH, Dk, Dk), F32)],
        scratch_shapes=[pltpu.VMEM((NH, Dk, Dk), F32)],
        compiler_params=_params(("arbitrary", "arbitrary")),
    )(u, w, qd, kd, gl, attn)


def _dn_scan_bwd(name, do, u, w, qd, kd, gl, attn, st, *, NH, B):
    T, DN = u.shape
    Dk = DN // NH
    C = CHUNK
    NP = T // (B * PAIR)

    def body(do_ref, u_ref, w_ref, qd_ref, kd_ref, gl_ref, at_ref, st_ref,
             du_ref, dw_ref, dqd_ref, dkd_ref, dgl_ref, dat_ref, ds_ref):
        @pl.when(pl.program_id(1) == 0)
        def _():
            ds_ref[...] = jnp.zeros_like(ds_ref)

        row0 = lax.broadcasted_iota(jnp.int32, (C, Dk), 0) == 0
        zeros = jnp.zeros((C, Dk), F32)
        for sub in (1, 0):
            rs = slice(sub * C, (sub + 1) * C)
            for h in range(NH):
                sl = slice(h * Dk, (h + 1) * Dk)
                s, ds, g = st_ref[sub, h], ds_ref[h], do_ref[rs, sl]
                wv, at, kdv = w_ref[rs, sl], at_ref[h, rs, :], kd_ref[rs, sl]
                vnew = u_ref[rs, sl] - _bdot(wv, s, NN)
                vext = jnp.concatenate([vnew, zeros] if sub == 0 else [zeros, vnew], axis=0)
                dvnew = _bdot(at, g, TN)[rs] + _bdot(kdv, ds, NN)
                dat_ref[h, rs, :] = _bdot(g, vext, NT)
                dqd_ref[rs, sl] = _bdot(g, s, NT)
                dkd_ref[rs, sl] = _bdot(vnew, ds, NT)
                dgl_ref[rs, sl] = jnp.where(row0, jnp.sum(s * ds, axis=0, keepdims=True), 0.0)
                du_ref[rs, sl] = dvnew
                dw_ref[rs, sl] = -_bdot(dvnew, s, NT)
                ds_ref[h] = _bdot(qd_ref[rs, sl], g, TN) + ds * gl_ref[sub * C:sub * C + 1, sl] - _bdot(wv, dvnew, TN)

    big = pl.BlockSpec((PAIR, DN), lambda b, n: (b * NP + NP - 1 - n, 0))
    att = pl.BlockSpec((NH, PAIR, PAIR), lambda b, n: (0, b * NP + NP - 1 - n, 0))
    return pl.pallas_call(
        body, name=name, grid=(B, NP),
        in_specs=[big] * 6 + [att, pl.BlockSpec((2, NH, Dk, Dk), lambda b, n: (b * NP + NP - 1 - n, 0, 0, 0))],
        out_specs=[big] * 5 + [att],
        out_shape=[jax.ShapeDtypeStruct((T, DN), F32)] * 5 + [jax.ShapeDtypeStruct((NH, T, PAIR), F32)],
        scratch_shapes=[pltpu.VMEM((NH, Dk, Dk), F32)],
        compiler_params=_params(("arbitrary", "arbitrary")),
    )(do, u, w, qd, kd, gl, attn, st)


def _sum0(name, a):
    n, r, ln = a.shape
    tr = _pick(r, 64, SUB)

    def body(a_ref, o_ref):
        acc = a_ref[0]
        for k in range(1, n):
            acc = acc + a_ref[k]
        o_ref[...] = acc

    return pl.pallas_call(
        body, name=name, grid=(r // tr,),
        in_specs=[pl.BlockSpec((n, tr, ln), lambda i: (0, i, 0))],
        out_specs=pl.BlockSpec((tr, ln), lambda i: (i, 0)),
        out_shape=jax.ShapeDtypeStruct((r, ln), F32),
        compiler_params=_params(("arbitrary",)),
    )(a)


def _adamw(name, w, g, m, v):
    R, Cc = w.shape
    tr = _pick(R, max(SUB, (1 << 18) // Cc // SUB * SUB), SUB)
    c1 = 1.0 - ADAM_B1 ** ADAM_STEP
    c2 = 1.0 - ADAM_B2 ** ADAM_STEP

    def body(w_ref, g_ref, m_ref, v_ref, d_ref, mo_ref, vo_ref):
        gv = g_ref[...]
        mn = ADAM_B1 * m_ref[...] + (1.0 - ADAM_B1) * gv
        vn = ADAM_B2 * v_ref[...] + (1.0 - ADAM_B2) * (gv * gv)
        mo_ref[...] = mn
        vo_ref[...] = vn
        d_ref[...] = -ADAM_LR * ((mn / c1) / (jnp.sqrt(vn / c2) + ADAM_EPS) + ADAM_WD * w_ref[...])

    spec = pl.BlockSpec((tr, Cc), lambda i: (i, 0))
    return pl.pallas_call(
        body, name=name, grid=(R // tr,), in_specs=[spec] * 4, out_specs=[spec] * 3,
        out_shape=[jax.ShapeDtypeStruct((R, Cc), F32)] * 3,
        compiler_params=_params(("arbitrary",)),
    )(w, g, m, v)


def _pack(arrs):
    flat = jnp.concatenate([a.reshape(-1).astype(F32) for a in arrs])
    pad = (-flat.shape[0]) % (SUB * PACK_LANES)
    return jnp.pad(flat, (0, pad)).reshape(-1, PACK_LANES)


def _unpack(flat, shapes):
    out, pos = [], 0
    for shp in shapes:
        n = math.prod(shp)
        out.append(flat[..., pos:pos + n].reshape(flat.shape[:-1] + tuple(shp)))
        pos += n
    return out


def _remote(src, dst, ssem, rsem, dev):
    return pltpu.make_async_remote_copy(src_ref=src, dst_ref=dst, send_sem=ssem, recv_sem=rsem, device_id=dev,
                                        device_id_type=MESH)


def _place():
    return lax.axis_index("x"), lax.axis_index("y"), lax.axis_index("c")


def _all_gather8(name, a):
    m, n = a.shape

    def body(x_ref, out_ref, send_sems, recv_sems, local_sem):
        x, y, c = _place()
        me, sibling = (x, y, c), (x, y, 1 - c)
        chips = [(1 - x, y), (x, 1 - y), (1 - x, 1 - y)]

        def rows(px, py, pc):
            return out_ref.at[pl.ds((4 * px + 2 * py + pc) * m, m), :]

        def copy(k, block, to, src=None):
            return _remote(rows(*block) if src is None else src, rows(*block), send_sems.at[k], recv_sems.at[k], to)

        mine = pltpu.make_async_copy(x_ref, rows(*me), local_sem)
        mine.start()
        first = [copy(0, me, sibling, src=x_ref)]
        first += [copy(1 + j, me, (*chip, c), src=x_ref) for j, chip in enumerate(chips)]
        for cp in first:
            cp.start()
        passed = [copy(4 + j, (*chip, c), sibling) for j, chip in enumerate(chips)]
        for j, chip in enumerate(chips):
            copy(1 + j, (*chip, c), me).wait_recv()
            passed[j].start()
        copy(0, sibling, me).wait_recv()
        for j, chip in enumerate(chips):
            copy(4 + j, (*chip, 1 - c), me).wait_recv()
        for cp in first + passed:
            cp.wait_send()
        mine.wait()

    return pl.pallas_call(
        body, name=name,
        out_shape=jax.ShapeDtypeStruct((8 * m, n), a.dtype),
        in_specs=[pl.BlockSpec(memory_space=pltpu.VMEM)],
        out_specs=pl.BlockSpec(memory_space=pltpu.VMEM),
        scratch_shapes=[pltpu.SemaphoreType.DMA((7,)), pltpu.SemaphoreType.DMA((7,)), pltpu.SemaphoreType.DMA],
        compiler_params=pltpu.CompilerParams(vmem_limit_bytes=VMEM_LIMIT),
    )(a)


def _chip_peers(x, y):
    return [(1 - x, y), (x, 1 - y), (1 - x, 1 - y)]


def _sds(a):
    return jax.ShapeDtypeStruct(a.shape, a.dtype)


def _cast_into_slot(name, idx, a, dtype):
    R, Cc = a.shape
    tr = _pick(R, 256, 16)

    def body(i_ref, a_ref, o_ref):
        o_ref[0] = a_ref[...].astype(dtype)

    return pl.pallas_call(
        body, name=name,
        grid_spec=pltpu.PrefetchScalarGridSpec(
            num_scalar_prefetch=1, grid=(R // tr,),
            in_specs=[pl.BlockSpec((tr, Cc), lambda i, ix: (i, 0))],
            out_specs=pl.BlockSpec((1, tr, Cc), lambda i, ix: (ix[0], i, 0))),
        out_shape=jax.ShapeDtypeStruct((4, R, Cc), dtype),
        compiler_params=_params(("arbitrary",)),
    )(idx, a)


def _gather_ici(bufs):
    n = len(bufs)

    def plan(ins, outs, ssem, rsem, starting):
        x, y, c = _place()
        sends, recvs = [], []
        for a in range(n):
            for k, (px, py) in enumerate(_chip_peers(x, y)):
                mine, got = outs[a].at[2 * x + y, c], outs[a].at[2 * px + py, c]
                sends.append(_remote(mine, mine, ssem.at[3 * a + k], rsem.at[3 * a + k], (px, py, c)))
                if not starting:
                    recvs.append(_remote(got, got, ssem.at[3 * a + k], rsem.at[3 * a + k], (px, py, c)))
        return sends if starting else (sends, recvs)

    return _Comm(list(bufs), [_sds(b) for b in bufs], {a: a for a in range(n)}, 3 * n, plan)


def _gather_d2d(bufs):
    n = len(bufs)

    def plan(ins, outs, ssem, rsem, starting):
        x, y, c = _place()
        sends, recvs = [], []
        for a in range(n):
            for k, (px, py) in enumerate(_chip_peers(x, y)):
                got, other = outs[a].at[2 * px + py, c], outs[a].at[2 * px + py, 1 - c]
                sends.append(_remote(got, got, ssem.at[3 * a + k], rsem.at[3 * a + k], (x, y, 1 - c)))
                if not starting:
                    recvs.append(_remote(other, other, ssem.at[3 * a + k], rsem.at[3 * a + k], (x, y, 1 - c)))
        return sends if starting else (sends, recvs)

    return _Comm(list(bufs), [_sds(b) for b in bufs], {a: a for a in range(n)}, 3 * n, plan)


def _swap_halves(grads):
    n = len(grads)

    def plan(ins, outs, ssem, rsem, starting):
        x, y, c = _place()
        cps = [_remote(ins[a].at[:, 1 - c], outs[a], ssem.at[a], rsem.at[a], (x, y, 1 - c)) for a in range(n)]
        return cps if starting else (cps, cps)

    return _Comm(list(grads), [jax.ShapeDtypeStruct((4,) + g.shape[2:], g.dtype) for g in grads], {}, n, plan)


def _scatter_chips(halves):
    n = len(halves)

    def plan(ins, outs, ssem, rsem, starting):
        x, y, c = _place()
        cps = []
        for a in range(n):
            for k, (px, py) in enumerate(_chip_peers(x, y)):
                cps.append(_remote(ins[a].at[2 * px + py], outs[a].at[k], ssem.at[3 * a + k], rsem.at[3 * a + k], (px, py, c)))
        return cps if starting else (cps, cps)

    return _Comm(list(halves), [jax.ShapeDtypeStruct((3,) + h.shape[1:], h.dtype) for h in halves], {}, 3 * n, plan)


def _join_halves(parts):
    n = len(parts)

    def plan(ins, outs, ssem, rsem, starting):
        x, y, c = _place()
        sends = [_remote(outs[a].at[c], outs[a].at[c], ssem.at[a], rsem.at[a], (x, y, 1 - c)) for a in range(n)]
        if starting:
            return sends
        return sends, [_remote(outs[a].at[1 - c], outs[a].at[1 - c], ssem.at[a], rsem.at[a], (x, y, 1 - c)) for a in range(n)]

    return _Comm(list(parts), [_sds(p) for p in parts], {a: a for a in range(n)}, n, plan)


def _add_half(name, idx, g, r):
    _, _, Rh, Cc = g.shape
    tr = _pick(Rh, 128, 16)

    def body(i_ref, g_ref, r_ref, o_ref):
        o_ref[...] = (g_ref[0].astype(F32) + r_ref[...].astype(F32)).astype(o_ref.dtype)

    return pl.pallas_call(
        body, name=name,
        grid_spec=pltpu.PrefetchScalarGridSpec(
            num_scalar_prefetch=1, grid=(4, Rh // tr),
            in_specs=[pl.BlockSpec((1, 1, tr, Cc), lambda s, i, ix: (s, ix[1], i, 0)),
                      pl.BlockSpec((1, tr, Cc), lambda s, i, ix: (s, i, 0))],
            out_specs=pl.BlockSpec((1, tr, Cc), lambda s, i, ix: (s, i, 0))),
        out_shape=jax.ShapeDtypeStruct((4, Rh, Cc), g.dtype),
        compiler_params=_params(("arbitrary", "arbitrary")),
    )(idx, g, r)


def _add_chips(name, idx, h, r):
    _, Rh, Cc = h.shape
    tr = _pick(Rh, 128, 16)

    def body(i_ref, h_ref, r0, r1, r2, o_ref):
        o_ref[0] = ((h_ref[0].astype(F32) + r0[0].astype(F32)) + r1[0].astype(F32)) + r2[0].astype(F32)

    return pl.pallas_call(
        body, name=name,
        grid_spec=pltpu.PrefetchScalarGridSpec(
            num_scalar_prefetch=1, grid=(Rh // tr,),
            in_specs=[pl.BlockSpec((1, tr, Cc), lambda i, ix: (ix[0], i, 0))]
            + [pl.BlockSpec((1, tr, Cc), functools.partial(lambda i, ix, k: (k, i, 0), k=k)) for k in range(3)],
            out_specs=pl.BlockSpec((1, tr, Cc), lambda i, ix: (ix[1], i, 0))),
        out_shape=jax.ShapeDtypeStruct((2, Rh, Cc), F32),
        compiler_params=_params(("arbitrary",)),
    )(idx, h, r, r, r)


def kernel(x, c, w_ada, b_ada, norm1_w, w_in, conv_dw_w, conv_dw_b, conv_ln_w, conv_ln_b, w_pw2, conv_out_norm_w, qkv_conv_w, a_log, dt_bias, dn_norm_w, w_out, norm2_w, w_up, w_down, final_ada_w, final_ada_b, final_norm_w, loss_target, m_w_ada, m_b_ada, m_norm1_w, m_w_in, m_conv_dw_w, m_conv_dw_b, m_conv_ln_w, m_conv_ln_b, m_w_pw2, m_conv_out_norm_w, m_qkv_conv_w, m_a_log, m_dt_bias, m_dn_norm_w, m_w_out, m_norm2_w, m_w_up, m_w_down, m_final_ada_w, m_final_ada_b, m_final_norm_w, v_w_ada, v_b_ada, v_norm1_w, v_w_in, v_conv_dw_w, v_conv_dw_b, v_conv_ln_w, v_conv_ln_b, v_w_pw2, v_conv_out_norm_w, v_qkv_conv_w, v_a_log, v_dt_bias, v_dn_norm_w, v_w_out, v_norm2_w, v_w_up, v_w_down, v_final_ada_w, v_final_ada_b, v_final_norm_w):
    names = ["w_ada", "b_ada", "norm1_w", "w_in", "conv_dw_w", "conv_dw_b", "conv_ln_w", "conv_ln_b", "w_pw2",
             "conv_out_norm_w", "qkv_conv_w", "a_log", "dt_bias", "dn_norm_w", "w_out", "norm2_w", "w_up", "w_down",
             "final_ada_w", "final_ada_b", "final_norm_w"]
    weights = dict(zip(names, [w_ada, b_ada, norm1_w, w_in, conv_dw_w, conv_dw_b, conv_ln_w, conv_ln_b, w_pw2,
                               conv_out_norm_w, qkv_conv_w, a_log, dt_bias, dn_norm_w, w_out, norm2_w, w_up, w_down,
                               final_ada_w, final_ada_b, final_norm_w]))
    mom1 = dict(zip(names, [m_w_ada, m_b_ada, m_norm1_w, m_w_in, m_conv_dw_w, m_conv_dw_b, m_conv_ln_w, m_conv_ln_b,
                            m_w_pw2, m_conv_out_norm_w, m_qkv_conv_w, m_a_log, m_dt_bias, m_dn_norm_w, m_w_out,
                            m_norm2_w, m_w_up, m_w_down, m_final_ada_w, m_final_ada_b, m_final_norm_w]))
    mom2 = dict(zip(names, [v_w_ada, v_b_ada, v_norm1_w, v_w_in, v_conv_dw_w, v_conv_dw_b, v_conv_ln_w, v_conv_ln_b,
                            v_w_pw2, v_conv_out_norm_w, v_qkv_conv_w, v_a_log, v_dt_bias, v_dn_norm_w, v_w_out,
                            v_norm2_w, v_w_up, v_w_down, v_final_ada_w, v_final_ada_b, v_final_norm_w]))

    B, S, D = x.shape
    T = B * S
    L = w_in.shape[0]
    C1 = conv_ln_w.shape[-1]
    NH, DH = a_log.shape[-1], dn_norm_w.shape[-1]
    DN = NH * DH
    FF = w_down.shape[1] * 4
    IN = w_in.shape[-1] * 4
    INP = 6 * C1 + LANE
    KC, KQ = conv_dw_w.shape[1], qkv_conv_w.shape[1]
    NW, NF = w_ada.shape[-1], final_ada_w.shape[-1]
    assert DH == LANE and DN == C1 and IN == 6 * C1 + 2 * NH and S % PAIR == 0
    xi, yi, ci = _place()
    s_me, me = 2 * xi + yi, 4 * xi + 2 * yi + ci
    idx = jnp.stack([s_me, ci]).astype(jnp.int32)
    tmf, tmb = _pick(S, 256, SUB), _pick(S, 128, SUB)
    ts, tsq = _pick(S, 256, _halo(KC)), _pick(S, 512, _halo(KQ))

    shapes1 = [(B, D), conv_dw_w.shape, qkv_conv_w.shape]
    g1 = _pack([c, conv_dw_w, qkv_conv_w])
    g1 = _all_gather8("gather_cond", g1).reshape(8, -1)
    c_all, cw_all, qw_all = _unpack(g1, shapes1)
    c_all = c_all.reshape(8 * B, D)
    conv_w_full = jnp.moveaxis(cw_all[0::2], 0, 2).reshape(L, KC, C1)
    qkv_w_full = jnp.moveaxis(qw_all[0::2], 0, 2).reshape(L, KQ, 3 * DN)
    (c_act,) = _rows_fwd("cond_silu", _silu_row, [_whole(c_all)], [], [], [(D, F32)], tm=8 * B)

    mods = []
    for l in range(L):
        bsh = lax.dynamic_slice(b_ada[l], (s_me * NW,), (NW,)).reshape(1, NW)
        mods.append(_mm(f"mod_{l}", c_act, w_ada[l], "nn", bias=bsh))
    bsh = lax.dynamic_slice(final_ada_b, (s_me * NF,), (NF,)).reshape(1, NF)
    mods.append(_mm("mod_final", c_act, final_ada_w, "nn", bias=bsh))
    shapes2 = [(8 * B, NW)] * L + [(8 * B, NF)]
    g2 = _all_gather8("gather_mod", _pack(mods)).reshape(8, -1)[0::2]
    mod_all = [jnp.moveaxis(t, 0, 1).reshape(8 * B, -1) for t in _unpack(g2, shapes2)]
    mod_me = [lax.dynamic_slice(t, (B * me, 0), (B, t.shape[1])) for t in mod_all]

    def split_mod(t, n):
        return [t[:, k * D:(k + 1) * D].reshape(B, 1, D) for k in range(n)]

    def cast_weights(l):
        sh = [w_in[l], w_pw2[l], w_out[l], w_up[l], w_down[l]]
        bufs = [_cast_into_slot(f"cast_{l}_{k}", idx, a, MXU_DTYPE) for k, a in enumerate(sh)]
        return [b.reshape(4, 2, b.shape[1] // 2, b.shape[2]) for b in bufs]

    def natural(got):
        gi, gp, go, gu, gd = got
        wi = jnp.pad(jnp.moveaxis(gi.reshape(4, D, IN // 4), 0, 1).reshape(D, IN), ((0, 0), (0, INP - IN)))
        return dict(w_in=wi, w_pw2=gp.reshape(C1, C1), w_out=go.reshape(D, D),
                    w_up=jnp.moveaxis(gu.reshape(4, D, FF // 4), 0, 1).reshape(D, FF), w_down=gd.reshape(FF, D))

    first = _run_comm("gather_w_ici_0", _gather_ici(cast_weights(0)))
    wfull = {0: natural(_run_comm("gather_w_d2d_0", _gather_d2d(first)))}

    pad_row = lambda v: jnp.pad(v.reshape(1, -1), ((0, 0), (NH, LANE - 2 * NH)))
    row = lambda v: v.reshape(1, -1)

    saved = []
    xcur = x.reshape(T, D)
    ycur = jnp.zeros((T, D), F32)
    gate_prev = jnp.zeros((B, 1, D), F32)
    for l in range(L):
        W = wfull[l]
        sh1, sc1, g1_, sh2, sc2, g2_ = split_mod(mod_me[l], 6)
        sv = dict(x_in=xcur, y_in=ycur, gate_in=gate_prev, sh1=sh1, sc1=sc1, g1=g1_, sh2=sh2, sc2=sc2, g2=g2_)
        x0, h1 = _rows_fwd(f"junction1_{l}", _junction, [_whole(xcur), _whole(ycur)], [gate_prev, sh1, sc1],
                           [row(norm1_w[l])], [(D, F32), (D, MXU_DTYPE)], tm=tmf, tpe=S // tmf)
        proj = _mm(f"proj_{l}", h1, W["w_in"], "nn", tn=896)
        (u0,) = _rows_fwd(f"glu_{l}", _glu, [(proj, C1, 0), (proj, C1, 1)], [], [], [(C1, F32)], tm=tmf)
        u1 = _conv_fwd(f"conv_{l}", u0, 0, conv_w_full[l], row(conv_dw_b[l]), ts=ts, tpe=S // ts)
        (u2,) = _rows_fwd(f"ln_silu_{l}", _ln_silu, [_whole(u1)], [], [row(conv_ln_w[l]), row(conv_ln_b[l])],
                          [(C1, MXU_DTYPE)], tm=tmf)
        u3 = _mm(f"pw2_{l}", u2, W["w_pw2"], "nn")
        (y_conv,) = _rows_fwd(f"conv_out_norm_{l}", _rms, [_whole(u3)], [], [row(conv_out_norm_w[l])],
                              [(C1, MXU_DTYPE)], tm=tmf)
        qkv = _conv_fwd(f"qkv_conv_{l}", proj, 2 * C1, qkv_w_full[l], None, ts=tsq, tpe=S // tsq)
        al, dtb = pad_row(a_log[l]), pad_row(dt_bias[l])
        nxt = _gather_ici(cast_weights(l + 1)) if l + 1 < L else None
        (u, w, qd, kd, gl, attn), nxt = _dn_prep_fwd(f"dn_prep_{l}", qkv, proj, 6 * C1, al, dtb, NH=NH, comm=nxt)
        o, st = _dn_scan_fwd(f"dn_scan_{l}", u, w, qd, kd, gl, attn, NH=NH, B=B)
        (y_dn,) = _rows_fwd(f"dn_out_norm_{l}", _gated_rms, [_whole(o), (proj, DN, 5)], [], [row(dn_norm_w[l])],
                            [(DN, MXU_DTYPE)], tm=tmf, ngroups=NH)
        ycat = jnp.concatenate([y_conv, y_dn], axis=1)
        y = _mm(f"out_{l}", ycat, W["w_out"], "nn")
        x1, h2 = _rows_fwd(f"junction2_{l}", _junction, [_whole(x0), _whole(y)], [g1_, sh2, sc2],
                           [row(norm2_w[l])], [(D, F32), (D, MXU_DTYPE)], tm=tmf, tpe=S // tmf)
        relu2 = (lambda t: (t, _relu2(t)[0]), [], [F32, MXU_DTYPE])
        if l + 1 < L:
            (up, act), nxt = _mm(f"up_{l}", h2, W["w_up"], "nn", post=relu2, comm=_gather_d2d(list(nxt)))
            wfull[l + 1] = natural(nxt)
        else:
            up, act = _mm(f"up_{l}", h2, W["w_up"], "nn", post=relu2)
        mlp = _mm(f"down_{l}", act, W["w_down"], "nn")
        sv.update(x0=x0, h1=h1, proj=proj, u0=u0, u1=u1, u2=u2, u3=u3, qkv=qkv, al=al, dtb=dtb, u=u, w=w, qd=qd, kd=kd,
                  gl=gl, attn=attn, o=o, st=st, ycat=ycat, y=y, x1=x1, h2=h2, up=up, act=act)
        saved.append(sv)
        xcur, ycur, gate_prev = x1, mlp, g2_

    shf, scf = split_mod(mod_me[L], 2)
    tgt = loss_target.reshape(T, D)
    rowloss, dx, dy, dgate, dshf, dscf, dfinal_norm = _rows_vjp(
        "loss_head", _final_loss, [_whole(xcur), _whole(ycur), _whole(tgt)], [gate_prev, shf, scf], [row(final_norm_w)],
        None, [F32, MXU_DTYPE, None], tm=tmb, tpe=S // tmb, primal=[(1, F32)])
    loss = lax.psum(jnp.sum(rowloss), ("x", "y", "c"))

    dmods = [None] * L
    small = [None] * L
    big_grads = {}
    pending = None
    for l in reversed(range(L)):
        W, sv = wfull[l], saved[l]
        d_relu2 = (lambda t, u_: (t * (2.0 * jnp.maximum(u_, 0.0)),), [sv["up"]], [MXU_DTYPE])
        if pending is None:
            dup = _mm(f"d_down_x_{l}", dy, W["w_down"], "nt", post=d_relu2)
        else:
            dup, got = _mm(f"d_down_x_{l}", dy, W["w_down"], "nt", post=d_relu2, comm=_swap_halves(pending))
            halves = [_add_half(f"rs_add_half_{l + 1}_{a}", idx, g, r) for a, (g, r) in enumerate(zip(pending, got))]
        gw_down = _mm(f"d_down_w_{l}", sv["act"], dy, "tn", COMM_DTYPE)
        dh2 = _mm(f"d_up_x_{l}", dup, W["w_up"], "nt")
        gw_up = _mm(f"d_up_w_{l}", sv["h2"], dup, "tn", COMM_DTYPE)
        dx0, dyo, dg1, dsh2, dsc2, dn2 = _rows_vjp(
            f"d_junction2_{l}", _junction, [_whole(sv["x0"]), _whole(sv["y"])], [sv["g1"], sv["sh2"], sv["sc2"]],
            [row(norm2_w[l])], [_whole(dx), _whole(dh2)], [F32, MXU_DTYPE], tm=tmb, tpe=S // tmb)
        dycat = _mm(f"d_out_x_{l}", dyo, W["w_out"], "nt")
        gw_out = _mm(f"d_out_w_{l}", sv["ycat"], dyo, "tn", COMM_DTYPE)
        proj = sv["proj"]
        do, dz, ddn = _rows_vjp(f"d_dn_out_norm_{l}", _gated_rms, [_whole(sv["o"]), (proj, DN, 5)], [],
                                [row(dn_norm_w[l])], [(dycat, DN, 1)], [F32, MXU_DTYPE], tm=tmb, ngroups=NH)
        cts = _dn_scan_bwd(f"d_dn_scan_{l}", do, sv["u"], sv["w"], sv["qd"], sv["kd"], sv["gl"], sv["attn"], sv["st"],
                           NH=NH, B=B)
        (dq, dk, dv, dba, dal, ddt), got = _dn_prep_bwd(
            f"d_dn_prep_{l}", sv["qkv"], proj, 6 * C1, sv["al"], sv["dtb"], list(cts), F32, NH=NH,
            comm=None if pending is None else _scatter_chips(halves))
        if pending is not None:
            parts = [_add_chips(f"rs_add_chips_{l + 1}_{a}", idx, h, r) for a, (h, r) in enumerate(zip(halves, got))]
        dqkv, gqw = [], []
        for k, dpart in enumerate((dq, dk, dv)):
            dxp, dwp, _ = _conv_bwd(f"d_qkv_conv_{l}_{k}", proj, (2 + k) * C1, dpart,
                                    qkv_w_full[l][:, k * DN:(k + 1) * DN], MXU_DTYPE, ts=tsq, tpe=S // tsq)
            dqkv.append(dxp)
            gqw.append(dwp)
        (du3, dcon) = _rows_vjp(f"d_conv_out_norm_{l}", _rms, [_whole(sv["u3"])], [], [row(conv_out_norm_w[l])],
                                [(dycat, C1, 0)], [MXU_DTYPE], tm=tmb)
        du2 = _mm(f"d_pw2_x_{l}", du3, W["w_pw2"], "nt")
        gw_pw2 = _mm(f"d_pw2_w_{l}", sv["u2"], du3, "tn", COMM_DTYPE)
        du1, dlnw, dlnb = _rows_vjp(f"d_ln_silu_{l}", _ln_silu, [_whole(sv["u1"])], [],
                                    [row(conv_ln_w[l]), row(conv_ln_b[l])], [_whole(du2)], [F32], tm=tmb)
        du0, gcw, gcb = _conv_bwd(f"d_conv_{l}", sv["u0"], 0, du1, conv_w_full[l], F32, ts=ts, tpe=S // ts)
        dval, dgate_c = _rows_vjp(f"d_glu_{l}", _glu, [(proj, C1, 0), (proj, C1, 1)], [], [], [_whole(du0)],
                                  [MXU_DTYPE, MXU_DTYPE], tm=tmb)
        dproj = jnp.concatenate([dval, dgate_c] + dqkv + [dz, dba.astype(MXU_DTYPE)], axis=1)
        if pending is None:
            dh1 = _mm(f"d_proj_x_{l}", dproj, W["w_in"], "nt", tk=896)
        else:
            dh1, full = _mm(f"d_proj_x_{l}", dproj, W["w_in"], "nt", tk=896, comm=_join_halves(parts))
            big_grads[l + 1] = [f.reshape(-1, f.shape[-1]) for f in full]
        gw_in = _mm(f"d_proj_w_{l}", sv["h1"], dproj, "tn", COMM_DTYPE, tn=896)[:, :IN]
        dxn, dyn, dg2p, dsh1, dsc1, dn1 = _rows_vjp(
            f"d_junction1_{l}", _junction, [_whole(sv["x_in"]), _whole(sv["y_in"])], [sv["gate_in"], sv["sh1"], sv["sc1"]],
            [row(norm1_w[l])], [_whole(dx0), _whole(dh1)], [F32, MXU_DTYPE], tm=tmb, tpe=S // tmb)
        dmods[l] = [dsh1, dsc1, dg1, dsh2, dsc2, dgate]
        small[l] = dict(norm1_w=dn1, conv_dw_w=gcw, conv_dw_b=gcb, conv_ln_w=dlnw, conv_ln_b=dlnb, conv_out_norm_w=dcon,
                        qkv_conv_w=jnp.concatenate(gqw, axis=1), a_log=dal[:, NH:2 * NH], dt_bias=ddt[:, NH:2 * NH],
                        dn_norm_w=ddn, norm2_w=dn2)
        shard_major = [jnp.moveaxis(gw_in.reshape(D, 4, IN // 4), 1, 0), gw_pw2.reshape(4, C1 // 4, C1),
                       gw_out.reshape(4, D // 4, D), jnp.moveaxis(gw_up.reshape(D, 4, FF // 4), 1, 0),
                       gw_down.reshape(4, FF // 4, D)]
        pending = [g.reshape(4, 2, g.shape[1] // 2, g.shape[2]) for g in shard_major]
        dx, dy, dgate = dxn, dyn, dg2p

    got = _run_comm("rs_swap_0", _swap_halves(pending))
    halves = [_add_half(f"rs_add_half_0_{a}", idx, g, r) for a, (g, r) in enumerate(zip(pending, got))]
    got = _run_comm("rs_scatter_0", _scatter_chips(halves))
    parts = [_add_chips(f"rs_add_chips_0_{a}", idx, h, r) for a, (h, r) in enumerate(zip(halves, got))]
    big_grads[0] = [f.reshape(-1, f.shape[-1]) for f in _run_comm("rs_join_0", _join_halves(parts))]

    grad_x = dx.reshape(B, S, D)

    small_names = ["norm1_w", "conv_dw_w", "conv_dw_b", "conv_ln_w", "conv_ln_b", "conv_out_norm_w", "qkv_conv_w", "a_log",
                   "dt_bias", "dn_norm_w", "norm2_w"]
    dmod_flat = jnp.concatenate([jnp.concatenate([t.reshape(B, D) for t in dmods[l]], axis=1) for l in range(L)]
                                + [dshf.reshape(B, D), dscf.reshape(B, D)], axis=1)
    small_list = [small[l][n] for l in range(L) for n in small_names] + [dfinal_norm]
    shapes3 = [dmod_flat.shape] + [t.shape for t in small_list]
    g3 = _all_gather8("gather_grads", _pack([dmod_flat] + small_list))
    rows3 = g3.shape[0] // 8
    g3 = g3.reshape(8, rows3, PACK_LANES)
    summed = _unpack(_sum0("sum_small_grads", g3).reshape(-1), shapes3)[1:]
    dmod_all = _unpack(g3.reshape(8, -1), shapes3[:1])[0].reshape(8 * B, -1)
    nm = dmod_all.shape[1]
    grad_b_all = _sum0("sum_mod_grads", dmod_all.reshape(8 * B, nm // PACK_LANES, PACK_LANES)).reshape(-1)
    grads = {}
    gwa = []
    for l in range(L):
        dm = lax.dynamic_slice(dmod_all, (0, l * 6 * D + s_me * NW), (8 * B, NW))
        gwa.append(_mm(f"d_ada_w_{l}", c_act, dm, "tn"))
    grads["w_ada"] = jnp.stack(gwa)
    grads["b_ada"] = grad_b_all[:L * 6 * D].reshape(L, 6 * D)
    dm = lax.dynamic_slice(dmod_all, (0, L * 6 * D + s_me * NF), (8 * B, NF))
    grads["final_ada_w"] = _mm("d_final_ada_w", c_act, dm, "tn")
    grads["final_ada_b"] = grad_b_all[L * 6 * D:]
    per_layer = {n: [] for n in small_names}
    for l in range(L):
        for k, n in enumerate(small_names):
            per_layer[n].append(summed[l * len(small_names) + k])
    for n in small_names:
        t = jnp.stack(per_layer[n])
        if n == "conv_dw_w":
            t = lax.dynamic_slice(t, (0, 0, s_me * (C1 // 4)), (L, KC, C1 // 4))
        elif n == "qkv_conv_w":
            t = lax.dynamic_slice(t, (0, 0, s_me * (3 * DN // 4)), (L, KQ, 3 * DN // 4))
        grads[n] = t.reshape(weights[n].shape)
    grads["final_norm_w"] = summed[-1].reshape(final_norm_w.shape)
    for k, n in enumerate(["w_in", "w_pw2", "w_out", "w_up", "w_down"]):
        grads[n] = jnp.stack([big_grads[l][k] for l in range(L)])

    delta, new_m, new_v = {}, {}, {}
    big_names = ["w_ada", "w_in", "w_pw2", "w_out", "w_up", "w_down", "final_ada_w"]
    for n in big_names:
        shp = weights[n].shape
        two = lambda t: t.reshape(-1, shp[-1])
        d_, m_, v_ = _adamw(f"adamw_{n}", two(weights[n]), two(grads[n]), two(mom1[n]), two(mom2[n]))
        delta[n], new_m[n], new_v[n] = d_.reshape(shp), m_.reshape(shp), v_.reshape(shp)
    rest = [n for n in names if n not in big_names]
    rshapes = [weights[n].shape for n in rest]
    packed = [_pack([d[n] for n in rest]) for d in (weights, grads, mom1, mom2)]
    outs = _adamw("adamw_small", *packed)
    for dst, arr in zip((delta, new_m, new_v), outs):
        for n, t in zip(rest, _unpack(arr.reshape(-1), rshapes)):
            dst[n] = t

    return (loss, grad_x, *[grads[n] for n in names], *[delta[n] for n in names], *[new_m[n] for n in names],
            *[new_v[n] for n in names])
```

```python
import functools
import math
import typing

import jax
import jax.numpy as jnp
from jax import lax
from jax.experimental import pallas as pl
from jax.experimental.pallas import tpu as pltpu

F32 = jnp.float32
MXU_DTYPE = jnp.bfloat16
COMM_DTYPE = jnp.bfloat16
HI = lax.Precision.HIGHEST
CHUNK = 64
PAIR = 2 * CHUNK
EPS = 1e-6
LANE = 128
SUB = 8
PACK_LANES = 1024
VMEM_LIMIT = 56 * 1024 * 1024
ADAM_LR, ADAM_B1, ADAM_B2, ADAM_EPS, ADAM_WD, ADAM_STEP = 0.001, 0.9, 0.999, 1e-08, 0.01, 10
MESH = pl.DeviceIdType.MESH
ANY = pl.BlockSpec(memory_space=pl.ANY)
NN, NT, TN = ((1,), (0,)), ((1,), (1,)), ((0,), (0,))


def _pick(dim, pref, mult):
    for t in range(min(pref, dim), 0, -1):
        if dim % t == 0 and t % mult == 0:
            return t
    return dim


def _params(sem):
    return pltpu.CompilerParams(dimension_semantics=sem, vmem_limit_bytes=VMEM_LIMIT)


def _sigmoid(v):
    return 1.0 / (1.0 + jnp.exp(-v))


def _silu(v):
    return v * _sigmoid(v)


def _row_specs(rows, exps, gls, tm, tpe):
    specs = [pl.BlockSpec((tm, w), functools.partial(lambda i, j: (i, j), j=cb)) for _, w, cb in rows]
    specs += [pl.BlockSpec((1, 1, e.shape[-1]), lambda i: (i // tpe, 0, 0)) for e in exps]
    specs += [pl.BlockSpec((1, g.shape[-1]), lambda i: (0, 0)) for g in gls]
    return specs


def _rows_fwd(name, fn, rows, exps, gls, outs, *, tm, tpe=1, ngroups=1, comm=None):
    T = rows[0][0].shape[0]
    nr, ne, ng = len(rows), len(exps), len(gls)
    nsteps = T // tm

    def body(*refs):
        r, e, g, o = refs[:nr], refs[nr:nr + ne], refs[nr + ne:nr + ne + ng], refs[nr + ne + ng:]
        ev = [t[0].astype(F32) for t in e]
        gv = [t[...].astype(F32) for t in g]
        for k in range(ngroups):
            rv = [t[:, k * (w // ngroups):(k + 1) * (w // ngroups)].astype(F32) for t, (_, w, _) in zip(r, rows)]
            res = fn(*rv, *ev, *gv)
            for oref, val, (w, dt) in zip(o, res, outs):
                gw = w // ngroups
                oref[:, k * gw:(k + 1) * gw] = val.astype(dt)

    kw = dict(in_specs=_row_specs(rows, exps, gls, tm, tpe),
              out_specs=[pl.BlockSpec((tm, w), lambda i: (i, 0)) for w, _ in outs],
              out_shape=[jax.ShapeDtypeStruct((T, w), dt) for w, dt in outs], scratch_shapes=[])
    body, kw, args = _attach(comm, body, kw, [a for a, _, _ in rows] + list(exps) + list(gls),
                             lambda: (pl.program_id(0) == 0, pl.program_id(0) == nsteps - 1))
    res = pl.pallas_call(body, name=name, grid=(nsteps,), compiler_params=_params(("arbitrary",)), **kw)(*args)
    return res if comm is None else (res[:len(outs)], res[len(outs):])


def _rows_vjp(name, fn, rows, exps, gls, cts, row_dtypes, *, tm, tpe=1, ngroups=1, primal=None):
    T = rows[0][0].shape[0]
    nr, ne, ng = len(rows), len(exps), len(gls)
    nc = 0 if cts is None else len(cts)
    keep = [k for k, dt in enumerate(row_dtypes) if dt is not None]
    npr = 0 if primal is None else len(primal)

    def body(*refs):
        r, e, g = refs[:nr], refs[nr:nr + ne], refs[nr + ne:nr + ne + ng]
        c = refs[nr + ne + ng:nr + ne + ng + nc]
        o = refs[nr + ne + ng + nc:]
        po, ro, eo, go = o[:npr], o[npr:npr + len(keep)], o[npr + len(keep):npr + len(keep) + ne], o[npr + len(keep) + ne:]
        i = pl.program_id(0)
        ev = [t[0].astype(F32) for t in e]
        gv = [t[...].astype(F32) for t in g]
        esum = [jnp.zeros_like(v) for v in ev]
        gsum = [jnp.zeros_like(v) for v in gv]
        for k in range(ngroups):
            rv = [t[:, k * (w // ngroups):(k + 1) * (w // ngroups)].astype(F32) for t, (_, w, _) in zip(r, rows)]
            res, pull = jax.vjp(fn, *rv, *ev, *gv)
            if cts is None:
                ct = tuple(jnp.ones_like(v) for v in res)
            else:
                ct = tuple(t[:, k * (w // ngroups):(k + 1) * (w // ngroups)].astype(F32) for t, (_, w, _) in zip(c, cts))
            grads = pull(ct)
            for oref, val, (w, dt) in zip(po, res, primal or ()):
                gw = w // ngroups
                oref[:, k * gw:(k + 1) * gw] = val.astype(dt)
            for oref, idx in zip(ro, keep):
                gw = rows[idx][1] // ngroups
                oref[:, k * gw:(k + 1) * gw] = grads[idx].astype(row_dtypes[idx])
            esum = [s + d for s, d in zip(esum, grads[nr:nr + ne])]
            gsum = [s + d for s, d in zip(gsum, grads[nr + ne:])]

        if ne:
            @pl.when(i % tpe == 0)
            def _():
                for oref in eo:
                    oref[...] = jnp.zeros_like(oref)
            for oref, s in zip(eo, esum):
                oref[0] += s
        if ng:
            @pl.when(i == 0)
            def _():
                for oref in go:
                    oref[...] = jnp.zeros_like(oref)
            for oref, s in zip(go, gsum):
                oref[...] += s

    out_specs = [pl.BlockSpec((tm, w), lambda i: (i, 0)) for w, _ in (primal or ())]
    out_shape = [jax.ShapeDtypeStruct((T, w), dt) for w, dt in (primal or ())]
    out_specs += [pl.BlockSpec((tm, rows[k][1]), lambda i: (i, 0)) for k in keep]
    out_shape += [jax.ShapeDtypeStruct((T, rows[k][1]), row_dtypes[k]) for k in keep]
    out_specs += [pl.BlockSpec((1, 1, e.shape[-1]), lambda i: (i // tpe, 0, 0)) for e in exps]
    out_shape += [jax.ShapeDtypeStruct(e.shape, F32) for e in exps]
    out_specs += [pl.BlockSpec((1, g.shape[-1]), lambda i: (0, 0)) for g in gls]
    out_shape += [jax.ShapeDtypeStruct(g.shape, F32) for g in gls]
    ct_specs = [] if cts is None else [pl.BlockSpec((tm, w), functools.partial(lambda i, j: (i, j), j=cb)) for _, w, cb in cts]
    ct_arrs = [] if cts is None else [a for a, _, _ in cts]
    return pl.pallas_call(
        body, name=name, grid=(T // tm,),
        in_specs=_row_specs(rows, exps, gls, tm, tpe) + ct_specs,
        out_specs=out_specs, out_shape=out_shape,
        compiler_params=_params(("arbitrary",)),
    )(*[a for a, _, _ in rows], *exps, *gls, *ct_arrs)


def _whole(a):
    return (a, a.shape[-1], 0)


def _junction(x, y, gate, shift, scale, w):
    xn = x + gate * y
    r = lax.rsqrt(jnp.mean(xn * xn, axis=-1, keepdims=True) + EPS)
    return xn, (xn * r * w) * (1.0 + scale) + shift


def _final_loss(x, y, tgt, gate, shift, scale, w):
    _, out = _junction(x, y, gate, shift, scale, w)
    err = out - tgt
    return (0.5 * jnp.mean(err * err, axis=-1, keepdims=True),)


def _glu(val, gate):
    return (val * _sigmoid(gate),)


def _ln_silu(u, w, b):
    xc = u - jnp.mean(u, axis=-1, keepdims=True)
    y = xc * lax.rsqrt(jnp.mean(xc * xc, axis=-1, keepdims=True) + EPS) * w + b
    return (_silu(y),)


def _rms(u, w):
    return (u * lax.rsqrt(jnp.mean(u * u, axis=-1, keepdims=True) + EPS) * w,)


def _gated_rms(o, z, w):
    return (o * lax.rsqrt(jnp.mean(o * o, axis=-1, keepdims=True) + EPS) * w * _silu(z),)


def _relu2(u):
    r = jnp.maximum(u, 0.0)
    return (r * r,)


def _silu_row(u):
    return (_silu(u),)


class _Comm(typing.NamedTuple):
    ins: list
    outs: list
    aliases: dict
    nsem: int
    plan: typing.Callable


def _attach(comm, body, kw, args, first_last):
    if comm is None:
        return body, kw, args
    ni0, no0, ns0 = len(kw["in_specs"]), len(kw["out_specs"]), len(kw["scratch_shapes"])
    ni, no = len(comm.ins), len(comm.outs)
    kw = dict(kw, in_specs=kw["in_specs"] + [ANY] * ni, out_specs=kw["out_specs"] + [ANY] * no,
              out_shape=kw["out_shape"] + comm.outs,
              scratch_shapes=kw["scratch_shapes"] + [pltpu.SemaphoreType.DMA((comm.nsem,)), pltpu.SemaphoreType.DMA((comm.nsem,))],
              input_output_aliases={ni0 + k: no0 + v for k, v in comm.aliases.items()})

    def carrying(*refs):
        own_in, c_in = refs[:ni0], refs[ni0:ni0 + ni]
        own_out, c_out = refs[ni0 + ni:ni0 + ni + no0], refs[ni0 + ni + no0:ni0 + ni + no0 + no]
        scratch = refs[ni0 + ni + no0 + no:]
        ssem, rsem = scratch[ns0], scratch[ns0 + 1]
        first, last = first_last()

        @pl.when(first)
        def _():
            for cp in comm.plan(c_in, c_out, ssem, rsem, True):
                cp.start()

        body(*own_in, *own_out, *scratch[:ns0])

        @pl.when(last)
        def _():
            sends, recvs = comm.plan(c_in, c_out, ssem, rsem, False)
            for cp in sends:
                cp.wait_send()
            for cp in recvs:
                cp.wait_recv()

    return carrying, kw, args + list(comm.ins)


def _run_comm(name, comm):
    ni, no = len(comm.ins), len(comm.outs)

    def body(*refs):
        for cp in comm.plan(refs[:ni], refs[ni:ni + no], refs[-2], refs[-1], True):
            cp.start()
        sends, recvs = comm.plan(refs[:ni], refs[ni:ni + no], refs[-2], refs[-1], False)
        for cp in sends:
            cp.wait_send()
        for cp in recvs:
            cp.wait_recv()

    return pl.pallas_call(
        body, name=name, in_specs=[ANY] * ni, out_specs=[ANY] * no, out_shape=comm.outs,
        input_output_aliases=comm.aliases,
        scratch_shapes=[pltpu.SemaphoreType.DMA((comm.nsem,)), pltpu.SemaphoreType.DMA((comm.nsem,))],
    )(*comm.ins)


def _mm(name, a, b, mode, out_dtype=F32, bias=None, tm=1024, tn=1024, tk=2048, comm=None, post=None):
    if mode == "nn":
        (M, K), N = a.shape, b.shape[1]
    elif mode == "nt":
        (M, K), N = a.shape, b.shape[0]
    else:
        (K, M), N = a.shape, b.shape[1]
    tm, tn, tk = _pick(M, tm, LANE), _pick(N, tn, LANE), _pick(K, tk, LANE)
    nk = K // tk
    grid = (M // tm, N // tn, nk)
    dn = {"nn": NN, "nt": NT, "tn": TN}[mode]
    a_spec = pl.BlockSpec((tk, tm), lambda i, j, k: (k, i)) if mode == "tn" else pl.BlockSpec((tm, tk), lambda i, j, k: (i, k))
    b_spec = pl.BlockSpec((tn, tk), lambda i, j, k: (j, k)) if mode == "nt" else pl.BlockSpec((tk, tn), lambda i, j, k: (k, j))
    specs, args = [a_spec, b_spec], [a, b]
    if bias is not None:
        specs.append(pl.BlockSpec((1, tn), lambda i, j, k: (0, j)))
        args.append(bias)

    o_spec = pl.BlockSpec((tm, tn), lambda i, j, k: (i, j))
    fn, extra, out_dtypes = post if post is not None else (lambda t: (t,), [], [out_dtype])
    specs += [o_spec] * len(extra)
    args += list(extra)
    nin, nout = len(args), len(out_dtypes)

    def body(*refs):
        a_ref, b_ref = refs[0], refs[1]
        outs, acc = refs[nin:nin + nout], refs[-1]
        k = pl.program_id(2)

        @pl.when(k == 0)
        def _():
            acc[...] = jnp.zeros_like(acc)

        acc[...] += lax.dot_general(a_ref[...].astype(MXU_DTYPE), b_ref[...].astype(MXU_DTYPE), (dn, ((), ())),
                                    preferred_element_type=F32)

        @pl.when(k == nk - 1)
        def _():
            res = acc[...]
            if bias is not None:
                res = res + refs[2][...]
            for o_ref, val in zip(outs, fn(res, *[r[...] for r in refs[nin - len(extra):nin]])):
                o_ref[...] = val.astype(o_ref.dtype)

    def first_last():
        at = [pl.program_id(d) for d in range(3)]
        first = jnp.logical_and(jnp.logical_and(at[0] == 0, at[1] == 0), at[2] == 0)
        last = jnp.logical_and(jnp.logical_and(at[0] == grid[0] - 1, at[1] == grid[1] - 1), at[2] == grid[2] - 1)
        return first, last

    kw = dict(in_specs=specs, out_specs=[o_spec] * nout, out_shape=[jax.ShapeDtypeStruct((M, N), dt) for dt in out_dtypes],
              scratch_shapes=[pltpu.VMEM((tm, tn), F32)])
    body, kw, args = _attach(comm, body, kw, args, first_last)
    sem = ("arbitrary",) * 3 if comm is not None else ("parallel", "parallel", "arbitrary")
    res = pl.pallas_call(body, name=name, grid=grid, compiler_params=_params(sem), **kw)(*args)
    main = res[0] if nout == 1 else res[:nout]
    return main if comm is None else (main, res[nout:])


def _halo(K):
    return SUB * -(-(K - 1) // SUB)


def _conv_fwd(name, x, col0, w, bias, *, ts, tpe):
    T = x.shape[0]
    K, C = w.shape
    H = _halo(K)
    cb = _pick(C, 256, LANE)
    rb = _pick(ts, 64, SUB)
    off = col0 // cb
    specs = [pl.BlockSpec((ts, cb), lambda j, i: (i, off + j)),
             pl.BlockSpec((H, cb), lambda j, i: (jnp.maximum(i * (ts // H) - 1, 0), off + j)),
             pl.BlockSpec((K, cb), lambda j, i: (0, j))]
    args = [x, x, w]
    if bias is not None:
        specs.append(pl.BlockSpec((1, cb), lambda j, i: (0, j)))
        args.append(bias)

    def body(*refs):
        cur, halo, w_ref = refs[:3]
        o_ref, xp = refs[-2], refs[-1]
        i = pl.program_id(1)
        xp[0:H, :] = jnp.where(i % tpe == 0, 0.0, halo[...].astype(F32))
        xp[H:H + ts, :] = cur[...].astype(F32)
        for r0 in range(0, ts, rb):
            acc = jnp.zeros((rb, cb), F32) if bias is None else jnp.zeros((rb, cb), F32) + refs[3][...]
            for j in range(K):
                lo = H - (K - 1) + j + r0
                acc = acc + w_ref[j:j + 1, :] * xp[lo:lo + rb, :]
            o_ref[r0:r0 + rb, :] = acc

    return pl.pallas_call(
        body, name=name, grid=(C // cb, T // ts), in_specs=specs,
        out_specs=pl.BlockSpec((ts, cb), lambda j, i: (i, j)),
        out_shape=jax.ShapeDtypeStruct((T, C), F32),
        scratch_shapes=[pltpu.VMEM((H + ts, cb), F32)],
        compiler_params=_params(("parallel", "arbitrary")),
    )(*args)


def _conv_bwd(name, x, col0, dy, w, out_dtype, *, ts, tpe):
    T = x.shape[0]
    K, C = w.shape
    H = _halo(K)
    cb = _pick(C, 256, LANE)
    rb = _pick(ts, 64, SUB)
    off = col0 // cb
    nt = T // ts

    def body(cur, halo, dyc, dyn, w_ref, dx_ref, dw_ref, db_ref, xp, dyp):
        i = pl.program_id(1)
        xp[0:H, :] = jnp.where(i % tpe == 0, 0.0, halo[...].astype(F32))
        xp[H:H + ts, :] = cur[...].astype(F32)
        dyp[0:ts, :] = dyc[...]
        dyp[ts:ts + H, :] = jnp.where(i % tpe == tpe - 1, 0.0, dyn[...])
        for r0 in range(0, ts, rb):
            acc = jnp.zeros((rb, cb), F32)
            for j in range(K):
                lo = K - 1 - j + r0
                acc = acc + w_ref[j:j + 1, :] * dyp[lo:lo + rb, :]
            dx_ref[r0:r0 + rb, :] = acc.astype(out_dtype)

        @pl.when(i == 0)
        def _():
            dw_ref[...] = jnp.zeros_like(dw_ref)
            db_ref[...] = jnp.zeros_like(db_ref)

        for j in range(K):
            part = jnp.zeros((1, cb), F32)
            for r0 in range(0, ts, rb):
                lo = H - (K - 1) + j + r0
                part = part + jnp.sum(dyp[r0:r0 + rb, :] * xp[lo:lo + rb, :], axis=0, keepdims=True)
            dw_ref[j:j + 1, :] += part
        db_ref[...] += jnp.sum(dyc[...], axis=0, keepdims=True)

    return pl.pallas_call(
        body, name=name, grid=(C // cb, nt),
        in_specs=[pl.BlockSpec((ts, cb), lambda j, i: (i, off + j)),
                  pl.BlockSpec((H, cb), lambda j, i: (jnp.maximum(i * (ts // H) - 1, 0), off + j)),
                  pl.BlockSpec((ts, cb), lambda j, i: (i, j)),
                  pl.BlockSpec((H, cb), lambda j, i: (jnp.minimum((i + 1) * (ts // H), T // H - 1), j)),
                  pl.BlockSpec((K, cb), lambda j, i: (0, j))],
        out_specs=[pl.BlockSpec((ts, cb), lambda j, i: (i, j)),
                   pl.BlockSpec((K, cb), lambda j, i: (0, j)),
                   pl.BlockSpec((1, cb), lambda j, i: (0, j))],
        out_shape=[jax.ShapeDtypeStruct((T, C), out_dtype), jax.ShapeDtypeStruct((K, C), F32),
                   jax.ShapeDtypeStruct((1, C), F32)],
        scratch_shapes=[pltpu.VMEM((H + ts, cb), F32), pltpu.VMEM((ts + H, cb), F32)],
        compiler_params=_params(("parallel", "arbitrary")),
    )(x, x, dy, dy, w)


def _hdot(a, b, dn):
    return lax.dot_general(a, b, (dn, ((), ())), precision=HI, preferred_element_type=F32)


def _bdot(a, b, dn):
    return lax.dot_general(a.astype(MXU_DTYPE), b.astype(MXU_DTYPE), (dn, ((), ())), preferred_element_type=F32)


def _split(a):
    hi = a.astype(MXU_DTYPE)
    return hi, (a - hi.astype(F32)).astype(MXU_DTYPE)


def _dot3_raw(a, b, dn):
    if MXU_DTYPE == F32:
        return _hdot(a, b, dn)
    (ah, al), (bh, bl) = _split(a), _split(b)
    d = lambda p, q: lax.dot_general(p, q, (dn, ((), ())), preferred_element_type=F32)
    return d(ah, bh) + (d(ah, bl) + d(al, bh))


def _with_vjp(raw):
    dn = {"nn": NN, "nt": NT, "tn": TN}

    @functools.partial(jax.custom_vjp, nondiff_argnums=(2,))
    def dot(a, b, mode):
        return raw(a, b, dn[mode])

    def fwd(a, b, mode):
        return raw(a, b, dn[mode]), (a, b)

    def bwd(mode, res, ct):
        a, b = res
        if mode == "nn":
            return raw(ct, b, NT), raw(a, ct, TN)
        if mode == "nt":
            return raw(ct, b, NN), raw(ct, a, TN)
        return raw(b, ct, NT), raw(a, ct, NN)

    dot.defvjp(fwd, bwd)
    return dot


_dot_exact = _with_vjp(_hdot)
_dot3 = _with_vjp(_dot3_raw)
_dot1 = _with_vjp(_bdot)


@jax.custom_vjp
def _tri_inv(ns):
    C = ns[0].shape[0]
    eye = (lax.broadcasted_iota(jnp.int32, (C, C), 0) == lax.broadcasted_iota(jnp.int32, (C, C), 1)).astype(F32)
    ts = [eye + n for n in ns]
    ps = list(ns)
    for _ in range(int(math.log2(CHUNK)) - 1):
        ps = [_dot3_raw(p, p, NN) for p in ps]
        ts = [t + _dot3_raw(t, p, NN) for t, p in zip(ts, ps)]
    return tuple(ts)


def _tri_inv_fwd(ns):
    ts = _tri_inv(ns)
    return ts, ts


def _tri_inv_bwd(ts, cts):
    xs = [_dot3_raw(t, ct, TN) for t, ct in zip(ts, cts)]
    return (tuple(_dot3_raw(x, t, NT) for x, t in zip(xs, ts)),)


_tri_inv.defvjp(_tri_inv_fwd, _tri_inv_bwd)


def _dn_prep(qs, ks, vs, ba, alog, dtb, *, nh):
    C2, Dk = qs[0].shape
    z = ba + dtb
    g_all = -jnp.exp(alog) * (jnp.maximum(z, 0.0) + jnp.log(1.0 + jnp.exp(-jnp.abs(z))))
    ri = lax.broadcasted_iota(jnp.int32, (C2, C2), 0)
    cj = lax.broadcasted_iota(jnp.int32, (C2, C2), 1)
    same = jnp.logical_not(jnp.logical_xor(ri >= CHUNK, cj >= CHUNK))
    causal, strict = jnp.logical_and(ri >= cj, same), jnp.logical_and(ri > cj, same)
    gc_all = _dot_exact(causal.astype(F32), g_all, "nn")
    gc_rows = _dot_exact(g_all, jnp.logical_and(ri <= cj, same).astype(F32), "tn")
    gl_all = _dot_exact(same.astype(F32), g_all, "nn")
    lane = lax.broadcasted_iota(jnp.int32, (1, ba.shape[1]), 1)
    subl = lax.broadcasted_iota(jnp.int32, (ba.shape[1], 1), 0)
    heads = range(nh)
    sel = [(lane == nh + h).astype(F32) for h in heads]
    gc = [jnp.sum(gc_all * s, axis=1, keepdims=True) for s in sel]
    gl = [jnp.sum(gl_all * s, axis=1, keepdims=True) for s in sel]
    gcr = [jnp.sum(gc_rows * (subl == nh + h).astype(F32), axis=0, keepdims=True) for h in heads]
    decay = [jnp.where(causal, jnp.exp(jnp.where(causal, a - b, 0.0)), 0.0) for a, b in zip(gc, gcr)]
    beta = [_sigmoid(jnp.sum(ba * (lane == h).astype(F32), axis=1, keepdims=True)) for h in heads]
    q = [_silu(t) for t in qs]
    q = [t * lax.rsqrt(jnp.sum(t * t, axis=-1, keepdims=True) + EPS) * (Dk ** -0.5) for t in q]
    k = [_silu(t) for t in ks]
    k = [t * lax.rsqrt(jnp.sum(t * t, axis=-1, keepdims=True) + EPS) for t in k]
    kb = [a * b for a, b in zip(k, beta)]
    vb = [_silu(a) * b for a, b in zip(vs, beta)]
    kk = [_dot1(a, b, "nt") for a, b in zip(kb, k)]
    qk = [_dot1(a, b, "nt") for a, b in zip(q, k)]
    t = _tri_inv(tuple(-jnp.where(strict, a * d, 0.0) for a, d in zip(kk, decay)))
    egc = [jnp.exp(a) for a in gc]
    u = [_dot3(a, b, "nn") for a, b in zip(t, vb)]
    w = [_dot3(a, b * e, "nn") for a, b, e in zip(t, kb, egc)]
    attn = [jnp.where(causal, a * d, 0.0) for a, d in zip(qk, decay)]
    qd = [a * e for a, e in zip(q, egc)]
    kd = [a * jnp.exp(b - c) for a, b, c in zip(k, gl, gc)]
    glb = [jnp.exp(a) * jnp.ones((1, Dk), F32) for a in gl]
    return tuple(u), tuple(w), tuple(qd), tuple(kd), tuple(attn), tuple(glb)


def _dn_prep_specs(DN, col_q):
    qs = [pl.BlockSpec((PAIR, DN), functools.partial(lambda i, j: (i, j), j=j)) for j in range(3)]
    return qs + [pl.BlockSpec((PAIR, LANE), lambda i: (i, col_q)), pl.BlockSpec((1, LANE), lambda i: (0, 0)),
                 pl.BlockSpec((1, LANE), lambda i: (0, 0))]


def _heads(ref, NH, Dk):
    return tuple(ref[:, h * Dk:(h + 1) * Dk] for h in range(NH))


def _dn_prep_fwd(name, qkv, proj, ba_col, alog, dtb, *, NH, comm=None):
    T = qkv.shape[0]
    DN = qkv.shape[1] // 3
    Dk = DN // NH
    nsteps = T // PAIR

    def body(q_ref, k_ref, v_ref, ba_ref, al_ref, dt_ref, u_ref, w_ref, qd_ref, kd_ref, gl_ref, at_ref):
        res = _dn_prep(_heads(q_ref, NH, Dk), _heads(k_ref, NH, Dk), _heads(v_ref, NH, Dk), ba_ref[...], al_ref[...],
                       dt_ref[...], nh=NH)
        for h in range(NH):
            sl = slice(h * Dk, (h + 1) * Dk)
            u_ref[:, sl], w_ref[:, sl], qd_ref[:, sl], kd_ref[:, sl] = res[0][h], res[1][h], res[2][h], res[3][h]
            at_ref[h] = res[4][h]
            gl_ref[:, sl] = res[5][h]

    big = pl.BlockSpec((PAIR, DN), lambda i: (i, 0))
    kw = dict(in_specs=_dn_prep_specs(DN, ba_col // LANE),
              out_specs=[big] * 5 + [pl.BlockSpec((NH, PAIR, PAIR), lambda i: (0, i, 0))],
              out_shape=[jax.ShapeDtypeStruct((T, DN), F32)] * 5 + [jax.ShapeDtypeStruct((NH, T, PAIR), F32)],
              scratch_shapes=[])
    body, kw, args = _attach(comm, body, kw, [qkv, qkv, qkv, proj, alog, dtb],
                             lambda: (pl.program_id(0) == 0, pl.program_id(0) == nsteps - 1))
    res = pl.pallas_call(body, name=name, grid=(nsteps,), compiler_params=_params(("arbitrary",)), **kw)(*args)
    return res[:6], res[6:]


def _dn_prep_bwd(name, qkv, proj, ba_col, alog, dtb, cts, out_dtype, *, NH, comm=None):
    T = qkv.shape[0]
    DN = qkv.shape[1] // 3
    Dk = DN // NH
    nsteps = T // PAIR

    def body(q_ref, k_ref, v_ref, ba_ref, al_ref, dt_ref, du, dw, dqd, dkd, dgl, dat,
             dq_ref, dk_ref, dv_ref, dba_ref, dal_ref, ddt_ref):
        i = pl.program_id(0)
        _, pull = jax.vjp(functools.partial(_dn_prep, nh=NH), _heads(q_ref, NH, Dk), _heads(k_ref, NH, Dk),
                          _heads(v_ref, NH, Dk), ba_ref[...], al_ref[...], dt_ref[...])
        gq, gk, gv, gba, gal, gdt = pull((_heads(du, NH, Dk), _heads(dw, NH, Dk), _heads(dqd, NH, Dk),
                                          _heads(dkd, NH, Dk), tuple(dat[h] for h in range(NH)), _heads(dgl, NH, Dk)))
        for h in range(NH):
            sl = slice(h * Dk, (h + 1) * Dk)
            dq_ref[:, sl], dk_ref[:, sl], dv_ref[:, sl] = gq[h].astype(out_dtype), gk[h].astype(out_dtype), gv[h].astype(out_dtype)
        dba_ref[...] = gba.astype(out_dtype)

        @pl.when(i == 0)
        def _():
            dal_ref[...] = jnp.zeros_like(dal_ref)
            ddt_ref[...] = jnp.zeros_like(ddt_ref)

        dal_ref[...] += gal
        ddt_ref[...] += gdt

    big = pl.BlockSpec((PAIR, DN), lambda i: (i, 0))
    row = pl.BlockSpec((1, LANE), lambda i: (0, 0))
    kw = dict(in_specs=_dn_prep_specs(DN, ba_col // LANE) + [big] * 5 + [pl.BlockSpec((NH, PAIR, PAIR), lambda i: (0, i, 0))],
              out_specs=[big] * 3 + [pl.BlockSpec((PAIR, LANE), lambda i: (i, 0)), row, row],
              out_shape=[jax.ShapeDtypeStruct((T, DN), out_dtype)] * 3 + [jax.ShapeDtypeStruct((T, LANE), out_dtype),
                                                                           jax.ShapeDtypeStruct((1, LANE), F32),
                                                                           jax.ShapeDtypeStruct((1, LANE), F32)],
              scratch_shapes=[])
    body, kw, args = _attach(comm, body, kw, [qkv, qkv, qkv, proj, alog, dtb, *cts],
                             lambda: (pl.program_id(0) == 0, pl.program_id(0) == nsteps - 1))
    res = pl.pallas_call(body, name=name, grid=(nsteps,), compiler_params=_params(("arbitrary",)), **kw)(*args)
    return res[:6], res[6:]


def _dn_scan_fwd(name, u, w, qd, kd, gl, attn, *, NH, B):
    T, DN = u.shape
    Dk = DN // NH
    C = CHUNK
    NP = T // (B * PAIR)

    def body(u_ref, w_ref, qd_ref, kd_ref, gl_ref, at_ref, o_ref, st_ref, s_ref):
        @pl.when(pl.program_id(1) == 0)
        def _():
            s_ref[...] = jnp.zeros_like(s_ref)

        zeros = jnp.zeros((C, Dk), F32)
        for sub in range(2):
            rs = slice(sub * C, (sub + 1) * C)
            for h in range(NH):
                sl = slice(h * Dk, (h + 1) * Dk)
                s = s_ref[h]
                st_ref[sub, h] = s
                vnew = u_ref[rs, sl] - _bdot(w_ref[rs, sl], s, NN)
                vext = jnp.concatenate([vnew, zeros] if sub == 0 else [zeros, vnew], axis=0)
                o_ref[rs, sl] = _bdot(qd_ref[rs, sl], s, NN) + _bdot(at_ref[h, rs, :], vext, NN)
                s_ref[h] = s * gl_ref[sub * C:sub * C + 1, sl] + _bdot(kd_ref[rs, sl], vnew, TN)

    big = pl.BlockSpec((PAIR, DN), lambda b, n: (b * NP + n, 0))
    return pl.pallas_call(
        body, name=name, grid=(B, NP),
        in_specs=[big] * 5 + [pl.BlockSpec((NH, PAIR, PAIR), lambda b, n: (0, b * NP + n, 0))],
        out_specs=[big, pl.BlockSpec((2, NH, Dk, Dk), lambda b, n: (b * NP + n, 0, 0, 0))],
        out_shape=[jax.ShapeDtypeStruct((T, DN), F32), jax.ShapeDtypeStruct((T // C, NH, Dk, Dk), F32)],
        scratch_shapes=[pltpu.VMEM((NH, Dk, Dk), F32)],
        compiler_params=_params(("arbitrary", "arbitrary")),
    )(u, w, qd, kd, gl, attn)


def _dn_scan_bwd(name, do, u, w, qd, kd, gl, attn, st, *, NH, B):
    T, DN = u.shape
    Dk = DN // NH
    C = CHUNK
    NP = T // (B * PAIR)

    def body(do_ref, u_ref, w_ref, qd_ref, kd_ref, gl_ref, at_ref, st_ref,
             du_ref, dw_ref, dqd_ref, dkd_ref, dgl_ref, dat_ref, ds_ref):
        @pl.when(pl.program_id(1) == 0)
        def _():
            ds_ref[...] = jnp.zeros_like(ds_ref)

        row0 = lax.broadcasted_iota(jnp.int32, (C, Dk), 0) == 0
        zeros = jnp.zeros((C, Dk), F32)
        for sub in (1, 0):
            rs = slice(sub * C, (sub + 1) * C)
            for h in range(NH):
                sl = slice(h * Dk, (h + 1) * Dk)
                s, ds, g = st_ref[sub, h], ds_ref[h], do_ref[rs, sl]
                wv, at, kdv = w_ref[rs, sl], at_ref[h, rs, :], kd_ref[rs, sl]
                vnew = u_ref[rs, sl] - _bdot(wv, s, NN)
                vext = jnp.concatenate([vnew, zeros] if sub == 0 else [zeros, vnew], axis=0)
                dvnew = _bdot(at, g, TN)[rs] + _bdot(kdv, ds, NN)
                dat_ref[h, rs, :] = _bdot(g, vext, NT)
                dqd_ref[rs, sl] = _bdot(g, s, NT)
                dkd_ref[rs, sl] = _bdot(vnew, ds, NT)
                dgl_ref[rs, sl] = jnp.where(row0, jnp.sum(s * ds, axis=0, keepdims=True), 0.0)
                du_ref[rs, sl] = dvnew
                dw_ref[rs, sl] = -_bdot(dvnew, s, NT)
                ds_ref[h] = _bdot(qd_ref[rs, sl], g, TN) + ds * gl_ref[sub * C:sub * C + 1, sl] - _bdot(wv, dvnew, TN)

    big = pl.BlockSpec((PAIR, DN), lambda b, n: (b * NP + NP - 1 - n, 0))
    att = pl.BlockSpec((NH, PAIR, PAIR), lambda b, n: (0, b * NP + NP - 1 - n, 0))
    return pl.pallas_call(
        body, name=name, grid=(B, NP),
        in_specs=[big] * 6 + [att, pl.BlockSpec((2, NH, Dk, Dk), lambda b, n: (b * NP + NP - 1 - n, 0, 0, 0))],
        out_specs=[big] * 5 + [att],
        out_shape=[jax.ShapeDtypeStruct((T, DN), F32)] * 5 + [jax.ShapeDtypeStruct((NH, T, PAIR), F32)],
        scratch_shapes=[pltpu.VMEM((NH, Dk, Dk), F32)],
        compiler_params=_params(("arbitrary", "arbitrary")),
    )(do, u, w, qd, kd, gl, attn, st)


def _sum0(name, a):
    n, r, ln = a.shape
    tr = _pick(r, 64, SUB)

    def body(a_ref, o_ref):
        acc = a_ref[0]
        for k in range(1, n):
            acc = acc + a_ref[k]
        o_ref[...] = acc

    return pl.pallas_call(
        body, name=name, grid=(r // tr,),
        in_specs=[pl.BlockSpec((n, tr, ln), lambda i: (0, i, 0))],
        out_specs=pl.BlockSpec((tr, ln), lambda i: (i, 0)),
        out_shape=jax.ShapeDtypeStruct((r, ln), F32),
        compiler_params=_params(("arbitrary",)),
    )(a)


def _adamw(name, w, g, m, v):
    R, Cc = w.shape
    tr = _pick(R, max(SUB, (1 << 18) // Cc // SUB * SUB), SUB)
    c1 = 1.0 - ADAM_B1 ** ADAM_STEP
    c2 = 1.0 - ADAM_B2 ** ADAM_STEP

    def body(w_ref, g_ref, m_ref, v_ref, d_ref, mo_ref, vo_ref):
        gv = g_ref[...]
        mn = ADAM_B1 * m_ref[...] + (1.0 - ADAM_B1) * gv
        vn = ADAM_B2 * v_ref[...] + (1.0 - ADAM_B2) * (gv * gv)
        mo_ref[...] = mn
        vo_ref[...] = vn
        d_ref[...] = -ADAM_LR * ((mn / c1) / (jnp.sqrt(vn / c2) + ADAM_EPS) + ADAM_WD * w_ref[...])

    spec = pl.BlockSpec((tr, Cc), lambda i: (i, 0))
    return pl.pallas_call(
        body, name=name, grid=(R // tr,), in_specs=[spec] * 4, out_specs=[spec] * 3,
        out_shape=[jax.ShapeDtypeStruct((R, Cc), F32)] * 3,
        compiler_params=_params(("arbitrary",)),
    )(w, g, m, v)


def _pack(arrs):
    flat = jnp.concatenate([a.reshape(-1).astype(F32) for a in arrs])
    pad = (-flat.shape[0]) % (SUB * PACK_LANES)
    return jnp.pad(flat, (0, pad)).reshape(-1, PACK_LANES)


def _unpack(flat, shapes):
    out, pos = [], 0
    for shp in shapes:
        n = math.prod(shp)
        out.append(flat[..., pos:pos + n].reshape(flat.shape[:-1] + tuple(shp)))
        pos += n
    return out


def _remote(src, dst, ssem, rsem, dev):
    return pltpu.make_async_remote_copy(src_ref=src, dst_ref=dst, send_sem=ssem, recv_sem=rsem, device_id=dev,
                                        device_id_type=MESH)


def _place():
    return lax.axis_index("x"), lax.axis_index("y"), lax.axis_index("c")


def _all_gather8(name, a):
    m, n = a.shape

    def body(x_ref, out_ref, send_sems, recv_sems, local_sem):
        x, y, c = _place()
        me, sibling = (x, y, c), (x, y, 1 - c)
        chips = [(1 - x, y), (x, 1 - y), (1 - x, 1 - y)]

        def rows(px, py, pc):
            return out_ref.at[pl.ds((4 * px + 2 * py + pc) * m, m), :]

        def copy(k, block, to, src=None):
            return _remote(rows(*block) if src is None else src, rows(*block), send_sems.at[k], recv_sems.at[k], to)

        mine = pltpu.make_async_copy(x_ref, rows(*me), local_sem)
        mine.start()
        first = [copy(0, me, sibling, src=x_ref)]
        first += [copy(1 + j, me, (*chip, c), src=x_ref) for j, chip in enumerate(chips)]
        for cp in first:
            cp.start()
        passed = [copy(4 + j, (*chip, c), sibling) for j, chip in enumerate(chips)]
        for j, chip in enumerate(chips):
            copy(1 + j, (*chip, c), me).wait_recv()
            passed[j].start()
        copy(0, sibling, me).wait_recv()
        for j, chip in enumerate(chips):
            copy(4 + j, (*chip, 1 - c), me).wait_recv()
        for cp in first + passed:
            cp.wait_send()
        mine.wait()

    return pl.pallas_call(
        body, name=name,
        out_shape=jax.ShapeDtypeStruct((8 * m, n), a.dtype),
        in_specs=[pl.BlockSpec(memory_space=pltpu.VMEM)],
        out_specs=pl.BlockSpec(memory_space=pltpu.VMEM),
        scratch_shapes=[pltpu.SemaphoreType.DMA((7,)), pltpu.SemaphoreType.DMA((7,)), pltpu.SemaphoreType.DMA],
        compiler_params=pltpu.CompilerParams(vmem_limit_bytes=VMEM_LIMIT),
    )(a)


def _chip_peers(x, y):
    return [(1 - x, y), (x, 1 - y), (1 - x, 1 - y)]


def _sds(a):
    return jax.ShapeDtypeStruct(a.shape, a.dtype)


def _cast_into_slot(name, idx, a, dtype):
    R, Cc = a.shape
    tr = _pick(R, 256, 16)

    def body(i_ref, a_ref, o_ref):
        o_ref[0] = a_ref[...].astype(dtype)

    return pl.pallas_call(
        body, name=name,
        grid_spec=pltpu.PrefetchScalarGridSpec(
            num_scalar_prefetch=1, grid=(R // tr,),
            in_specs=[pl.BlockSpec((tr, Cc), lambda i, ix: (i, 0))],
            out_specs=pl.BlockSpec((1, tr, Cc), lambda i, ix: (ix[0], i, 0))),
        out_shape=jax.ShapeDtypeStruct((4, R, Cc), dtype),
        compiler_params=_params(("arbitrary",)),
    )(idx, a)


def _gather_ici(bufs):
    n = len(bufs)

    def plan(ins, outs, ssem, rsem, starting):
        x, y, c = _place()
        sends, recvs = [], []
        for a in range(n):
            for k, (px, py) in enumerate(_chip_peers(x, y)):
                mine, got = outs[a].at[2 * x + y, c], outs[a].at[2 * px + py, c]
                sends.append(_remote(mine, mine, ssem.at[3 * a + k], rsem.at[3 * a + k], (px, py, c)))
                if not starting:
                    recvs.append(_remote(got, got, ssem.at[3 * a + k], rsem.at[3 * a + k], (px, py, c)))
        return sends if starting else (sends, recvs)

    return _Comm(list(bufs), [_sds(b) for b in bufs], {a: a for a in range(n)}, 3 * n, plan)


def _gather_d2d(bufs):
    n = len(bufs)

    def plan(ins, outs, ssem, rsem, starting):
        x, y, c = _place()
        sends, recvs = [], []
        for a in range(n):
            for k, (px, py) in enumerate(_chip_peers(x, y)):
                got, other = outs[a].at[2 * px + py, c], outs[a].at[2 * px + py, 1 - c]
                sends.append(_remote(got, got, ssem.at[3 * a + k], rsem.at[3 * a + k], (x, y, 1 - c)))
                if not starting:
                    recvs.append(_remote(other, other, ssem.at[3 * a + k], rsem.at[3 * a + k], (x, y, 1 - c)))
        return sends if starting else (sends, recvs)

    return _Comm(list(bufs), [_sds(b) for b in bufs], {a: a for a in range(n)}, 3 * n, plan)


def _swap_halves(grads):
    n = len(grads)

    def plan(ins, outs, ssem, rsem, starting):
        x, y, c = _place()
        cps = [_remote(ins[a].at[:, 1 - c], outs[a], ssem.at[a], rsem.at[a], (x, y, 1 - c)) for a in range(n)]
        return cps if starting else (cps, cps)

    return _Comm(list(grads), [jax.ShapeDtypeStruct((4,) + g.shape[2:], g.dtype) for g in grads], {}, n, plan)


def _scatter_chips(halves):
    n = len(halves)

    def plan(ins, outs, ssem, rsem, starting):
        x, y, c = _place()
        cps = []
        for a in range(n):
            for k, (px, py) in enumerate(_chip_peers(x, y)):
                cps.append(_remote(ins[a].at[2 * px + py], outs[a].at[k], ssem.at[3 * a + k], rsem.at[3 * a + k], (px, py, c)))
        return cps if starting else (cps, cps)

    return _Comm(list(halves), [jax.ShapeDtypeStruct((3,) + h.shape[1:], h.dtype) for h in halves], {}, 3 * n, plan)


def _join_halves(parts):
    n = len(parts)

    def plan(ins, outs, ssem, rsem, starting):
        x, y, c = _place()
        sends = [_remote(outs[a].at[c], outs[a].at[c], ssem.at[a], rsem.at[a], (x, y, 1 - c)) for a in range(n)]
        if starting:
            return sends
        return sends, [_remote(outs[a].at[1 - c], outs[a].at[1 - c], ssem.at[a], rsem.at[a], (x, y, 1 - c)) for a in range(n)]

    return _Comm(list(parts), [_sds(p) for p in parts], {a: a for a in range(n)}, n, plan)


def _add_half(name, idx, g, r):
    _, _, Rh, Cc = g.shape
    tr = _pick(Rh, 512, 16)

    def body(i_ref, g_ref, r_ref, o_ref):
        o_ref[...] = (g_ref[0].astype(F32) + r_ref[...].astype(F32)).astype(o_ref.dtype)

    return pl.pallas_call(
        body, name=name,
        grid_spec=pltpu.PrefetchScalarGridSpec(
            num_scalar_prefetch=1, grid=(4, Rh // tr),
            in_specs=[pl.BlockSpec((1, 1, tr, Cc), lambda s, i, ix: (s, ix[1], i, 0)),
                      pl.BlockSpec((1, tr, Cc), lambda s, i, ix: (s, i, 0))],
            out_specs=pl.BlockSpec((1, tr, Cc), lambda s, i, ix: (s, i, 0))),
        out_shape=jax.ShapeDtypeStruct((4, Rh, Cc), g.dtype),
        compiler_params=_params(("arbitrary", "arbitrary")),
    )(idx, g, r)


def _add_chips(name, idx, h, r):
    _, Rh, Cc = h.shape
    tr = _pick(Rh, 256, 16)

    def body(i_ref, h_ref, r0, r1, r2, o_ref):
        o_ref[0] = ((h_ref[0].astype(F32) + r0[0].astype(F32)) + r1[0].astype(F32)) + r2[0].astype(F32)

    return pl.pallas_call(
        body, name=name,
        grid_spec=pltpu.PrefetchScalarGridSpec(
            num_scalar_prefetch=1, grid=(Rh // tr,),
            in_specs=[pl.BlockSpec((1, tr, Cc), lambda i, ix: (ix[0], i, 0))]
            + [pl.BlockSpec((1, tr, Cc), functools.partial(lambda i, ix, k: (k, i, 0), k=k)) for k in range(3)],
            out_specs=pl.BlockSpec((1, tr, Cc), lambda i, ix: (ix[1], i, 0))),
        out_shape=jax.ShapeDtypeStruct((2, Rh, Cc), F32),
        compiler_params=_params(("arbitrary",)),
    )(idx, h, r, r, r)


def kernel(x, c, w_ada, b_ada, norm1_w, w_in, conv_dw_w, conv_dw_b, conv_ln_w, conv_ln_b, w_pw2, conv_out_norm_w, qkv_conv_w, a_log, dt_bias, dn_norm_w, w_out, norm2_w, w_up, w_down, final_ada_w, final_ada_b, final_norm_w, loss_target, m_w_ada, m_b_ada, m_norm1_w, m_w_in, m_conv_dw_w, m_conv_dw_b, m_conv_ln_w, m_conv_ln_b, m_w_pw2, m_conv_out_norm_w, m_qkv_conv_w, m_a_log, m_dt_bias, m_dn_norm_w, m_w_out, m_norm2_w, m_w_up, m_w_down, m_final_ada_w, m_final_ada_b, m_final_norm_w, v_w_ada, v_b_ada, v_norm1_w, v_w_in, v_conv_dw_w, v_conv_dw_b, v_conv_ln_w, v_conv_ln_b, v_w_pw2, v_conv_out_norm_w, v_qkv_conv_w, v_a_log, v_dt_bias, v_dn_norm_w, v_w_out, v_norm2_w, v_w_up, v_w_down, v_final_ada_w, v_final_ada_b, v_final_norm_w):
    names = ["w_ada", "b_ada", "norm1_w", "w_in", "conv_dw_w", "conv_dw_b", "conv_ln_w", "conv_ln_b", "w_pw2",
             "conv_out_norm_w", "qkv_conv_w", "a_log", "dt_bias", "dn_norm_w", "w_out", "norm2_w", "w_up", "w_down",
             "final_ada_w", "final_ada_b", "final_norm_w"]
    weights = dict(zip(names, [w_ada, b_ada, norm1_w, w_in, conv_dw_w, conv_dw_b, conv_ln_w, conv_ln_b, w_pw2,
                               conv_out_norm_w, qkv_conv_w, a_log, dt_bias, dn_norm_w, w_out, norm2_w, w_up, w_down,
                               final_ada_w, final_ada_b, final_norm_w]))
    mom1 = dict(zip(names, [m_w_ada, m_b_ada, m_norm1_w, m_w_in, m_conv_dw_w, m_conv_dw_b, m_conv_ln_w, m_conv_ln_b,
                            m_w_pw2, m_conv_out_norm_w, m_qkv_conv_w, m_a_log, m_dt_bias, m_dn_norm_w, m_w_out,
                            m_norm2_w, m_w_up, m_w_down, m_final_ada_w, m_final_ada_b, m_final_norm_w]))
    mom2 = dict(zip(names, [v_w_ada, v_b_ada, v_norm1_w, v_w_in, v_conv_dw_w, v_conv_dw_b, v_conv_ln_w, v_conv_ln_b,
                            v_w_pw2, v_conv_out_norm_w, v_qkv_conv_w, v_a_log, v_dt_bias, v_dn_norm_w, v_w_out,
                            v_norm2_w, v_w_up, v_w_down, v_final_ada_w, v_final_ada_b, v_final_norm_w]))

    B, S, D = x.shape
    T = B * S
    L = w_in.shape[0]
    C1 = conv_ln_w.shape[-1]
    NH, DH = a_log.shape[-1], dn_norm_w.shape[-1]
    DN = NH * DH
    FF = w_down.shape[1] * 4
    IN = w_in.shape[-1] * 4
    INP = 6 * C1 + LANE
    KC, KQ = conv_dw_w.shape[1], qkv_conv_w.shape[1]
    NW, NF = w_ada.shape[-1], final_ada_w.shape[-1]
    assert DH == LANE and DN == C1 and IN == 6 * C1 + 2 * NH and S % PAIR == 0
    xi, yi, ci = _place()
    s_me, me = 2 * xi + yi, 4 * xi + 2 * yi + ci
    idx = jnp.stack([s_me, ci]).astype(jnp.int32)
    tmf, tmb = _pick(S, 256, SUB), _pick(S, 128, SUB)
    ts, tsq = _pick(S, 256, _halo(KC)), _pick(S, 512, _halo(KQ))

    shapes1 = [(B, D), conv_dw_w.shape, qkv_conv_w.shape]
    g1 = _pack([c, conv_dw_w, qkv_conv_w])
    g1 = _all_gather8("gather_cond", g1).reshape(8, -1)
    c_all, cw_all, qw_all = _unpack(g1, shapes1)
    c_all = c_all.reshape(8 * B, D)
    conv_w_full = jnp.moveaxis(cw_all[0::2], 0, 2).reshape(L, KC, C1)
    qkv_w_full = jnp.moveaxis(qw_all[0::2], 0, 2).reshape(L, KQ, 3 * DN)
    (c_act,) = _rows_fwd("cond_silu", _silu_row, [_whole(c_all)], [], [], [(D, F32)], tm=8 * B)

    mods = []
    for l in range(L):
        bsh = lax.dynamic_slice(b_ada[l], (s_me * NW,), (NW,)).reshape(1, NW)
        mods.append(_mm(f"mod_{l}", c_act, w_ada[l], "nn", bias=bsh))
    bsh = lax.dynamic_slice(final_ada_b, (s_me * NF,), (NF,)).reshape(1, NF)
    mods.append(_mm("mod_final", c_act, final_ada_w, "nn", bias=bsh))
    shapes2 = [(8 * B, NW)] * L + [(8 * B, NF)]
    g2 = _all_gather8("gather_mod", _pack(mods)).reshape(8, -1)[0::2]
    mod_all = [jnp.moveaxis(t, 0, 1).reshape(8 * B, -1) for t in _unpack(g2, shapes2)]
    mod_me = [lax.dynamic_slice(t, (B * me, 0), (B, t.shape[1])) for t in mod_all]

    def split_mod(t, n):
        return [t[:, k * D:(k + 1) * D].reshape(B, 1, D) for k in range(n)]

    def cast_weights(l):
        sh = [w_in[l], w_pw2[l], w_out[l], w_up[l], w_down[l]]
        bufs = [_cast_into_slot(f"cast_{l}_{k}", idx, a, MXU_DTYPE) for k, a in enumerate(sh)]
        return [b.reshape(4, 2, b.shape[1] // 2, b.shape[2]) for b in bufs]

    def natural(got):
        gi, gp, go, gu, gd = got
        wi = jnp.pad(jnp.moveaxis(gi.reshape(4, D, IN // 4), 0, 1).reshape(D, IN), ((0, 0), (0, INP - IN)))
        return dict(w_in=wi, w_pw2=gp.reshape(C1, C1), w_out=go.reshape(D, D),
                    w_up=jnp.moveaxis(gu.reshape(4, D, FF // 4), 0, 1).reshape(D, FF), w_down=gd.reshape(FF, D))

    first = _run_comm("gather_w_ici_0", _gather_ici(cast_weights(0)))
    wfull = {0: natural(_run_comm("gather_w_d2d_0", _gather_d2d(first)))}

    pad_row = lambda v: jnp.pad(v.reshape(1, -1), ((0, 0), (NH, LANE - 2 * NH)))
    row = lambda v: v.reshape(1, -1)

    saved = []
    xcur = x.reshape(T, D)
    ycur = jnp.zeros((T, D), F32)
    gate_prev = jnp.zeros((B, 1, D), F32)
    for l in range(L):
        sh1, sc1, g1_, sh2, sc2, g2_ = split_mod(mod_me[l], 6)
        sv = dict(x_in=xcur, y_in=ycur, gate_in=gate_prev, sh1=sh1, sc1=sc1, g1=g1_, sh2=sh2, sc2=sc2, g2=g2_)
        junction1 = functools.partial(_rows_fwd, f"junction1_{l}", _junction, [_whole(xcur), _whole(ycur)],
                                      [gate_prev, sh1, sc1], [row(norm1_w[l])], [(D, F32), (D, MXU_DTYPE)],
                                      tm=tmf, tpe=S // tmf)
        if l == 0:
            x0, h1 = junction1()
        else:
            (x0, h1), got = junction1(comm=_gather_d2d(travelling))
            wfull[l] = natural(got)
        W = wfull[l]
        proj = _mm(f"proj_{l}", h1, W["w_in"], "nn", tn=896)
        (u0,) = _rows_fwd(f"glu_{l}", _glu, [(proj, C1, 0), (proj, C1, 1)], [], [], [(C1, F32)], tm=tmf)
        u1 = _conv_fwd(f"conv_{l}", u0, 0, conv_w_full[l], row(conv_dw_b[l]), ts=ts, tpe=S // ts)
        (u2,) = _rows_fwd(f"ln_silu_{l}", _ln_silu, [_whole(u1)], [], [row(conv_ln_w[l]), row(conv_ln_b[l])],
                          [(C1, MXU_DTYPE)], tm=tmf)
        u3 = _mm(f"pw2_{l}", u2, W["w_pw2"], "nn")
        (y_conv,) = _rows_fwd(f"conv_out_norm_{l}", _rms, [_whole(u3)], [], [row(conv_out_norm_w[l])],
                              [(C1, MXU_DTYPE)], tm=tmf)
        qkv = _conv_fwd(f"qkv_conv_{l}", proj, 2 * C1, qkv_w_full[l], None, ts=tsq, tpe=S // tsq)
        al, dtb = pad_row(a_log[l]), pad_row(dt_bias[l])
        nxt = cast_weights(l + 1) if l + 1 < L else None
        (u, w, qd, kd, gl, attn), got_a = _dn_prep_fwd(f"dn_prep_{l}", qkv, proj, 6 * C1, al, dtb, NH=NH,
                                                       comm=None if nxt is None else _gather_ici(nxt[:3]))
        o, st = _dn_scan_fwd(f"dn_scan_{l}", u, w, qd, kd, gl, attn, NH=NH, B=B)
        (y_dn,) = _rows_fwd(f"dn_out_norm_{l}", _gated_rms, [_whole(o), (proj, DN, 5)], [], [row(dn_norm_w[l])],
                            [(DN, MXU_DTYPE)], tm=tmf, ngroups=NH)
        ycat = jnp.concatenate([y_conv, y_dn], axis=1)
        y = _mm(f"out_{l}", ycat, W["w_out"], "nn")
        x1, h2 = _rows_fwd(f"junction2_{l}", _junction, [_whole(x0), _whole(y)], [g1_, sh2, sc2],
                           [row(norm2_w[l])], [(D, F32), (D, MXU_DTYPE)], tm=tmf, tpe=S // tmf)
        relu2 = (lambda t: (t, _relu2(t)[0]), [], [F32, MXU_DTYPE])
        if nxt is not None:
            (up, act), got_b = _mm(f"up_{l}", h2, W["w_up"], "nn", post=relu2, comm=_gather_ici(nxt[3:4]))
            mlp, got_c = _mm(f"down_{l}", act, W["w_down"], "nn", comm=_gather_ici(nxt[4:]))
            travelling = list(got_a) + list(got_b) + list(got_c)
        else:
            up, act = _mm(f"up_{l}", h2, W["w_up"], "nn", post=relu2)
            mlp = _mm(f"down_{l}", act, W["w_down"], "nn")
        sv.update(x0=x0, h1=h1, proj=proj, u0=u0, u1=u1, u2=u2, u3=u3, qkv=qkv, al=al, dtb=dtb, u=u, w=w, qd=qd, kd=kd,
                  gl=gl, attn=attn, o=o, st=st, ycat=ycat, y=y, x1=x1, h2=h2, up=up, act=act)
        saved.append(sv)
        xcur, ycur, gate_prev = x1, mlp, g2_

    shf, scf = split_mod(mod_me[L], 2)
    tgt = loss_target.reshape(T, D)
    rowloss, dx, dy, dgate, dshf, dscf, dfinal_norm = _rows_vjp(
        "loss_head", _final_loss, [_whole(xcur), _whole(ycur), _whole(tgt)], [gate_prev, shf, scf], [row(final_norm_w)],
        None, [F32, MXU_DTYPE, None], tm=tmb, tpe=S // tmb, primal=[(1, F32)])
    loss = lax.psum(jnp.sum(rowloss), ("x", "y", "c"))

    dmods = [None] * L
    small = [None] * L
    big_grads = {}
    pending = None
    for l in reversed(range(L)):
        W, sv = wfull[l], saved[l]
        d_relu2 = (lambda t, u_: (t * (2.0 * jnp.maximum(u_, 0.0)),), [sv["up"]], [MXU_DTYPE])
        if pending is None:
            dup = _mm(f"d_down_x_{l}", dy, W["w_down"], "nt", post=d_relu2)
            gw_down = _mm(f"d_down_w_{l}", sv["act"], dy, "tn", COMM_DTYPE)
            dh2 = _mm(f"d_up_x_{l}", dup, W["w_up"], "nt")
            gw_up = _mm(f"d_up_w_{l}", sv["h2"], dup, "tn", COMM_DTYPE)
        else:
            dup, got = _mm(f"d_down_x_{l}", dy, W["w_down"], "nt", post=d_relu2, comm=_swap_halves(pending))
            halves = [_add_half(f"rs_add_half_{l + 1}_{a}", idx, g, r) for a, (g, r) in enumerate(zip(pending, got))]
            gw_down, got_a = _mm(f"d_down_w_{l}", sv["act"], dy, "tn", COMM_DTYPE, comm=_scatter_chips(halves[:3]))
            dh2, got_b = _mm(f"d_up_x_{l}", dup, W["w_up"], "nt", comm=_scatter_chips(halves[3:4]))
            gw_up, got_c = _mm(f"d_up_w_{l}", sv["h2"], dup, "tn", COMM_DTYPE, comm=_scatter_chips(halves[4:]))
            got = list(got_a) + list(got_b) + list(got_c)
            parts = [_add_chips(f"rs_add_chips_{l + 1}_{a}", idx, h, r) for a, (h, r) in enumerate(zip(halves, got))]
        dx0, dyo, dg1, dsh2, dsc2, dn2 = _rows_vjp(
            f"d_junction2_{l}", _junction, [_whole(sv["x0"]), _whole(sv["y"])], [sv["g1"], sv["sh2"], sv["sc2"]],
            [row(norm2_w[l])], [_whole(dx), _whole(dh2)], [F32, MXU_DTYPE], tm=tmb, tpe=S // tmb)
        if pending is None:
            dycat = _mm(f"d_out_x_{l}", dyo, W["w_out"], "nt")
        else:
            dycat, full = _mm(f"d_out_x_{l}", dyo, W["w_out"], "nt", comm=_join_halves(parts))
            big_grads[l + 1] = [f.reshape(-1, f.shape[-1]) for f in full]
        gw_out = _mm(f"d_out_w_{l}", sv["ycat"], dyo, "tn", COMM_DTYPE)
        proj = sv["proj"]
        do, dz, ddn = _rows_vjp(f"d_dn_out_norm_{l}", _gated_rms, [_whole(sv["o"]), (proj, DN, 5)], [],
                                [row(dn_norm_w[l])], [(dycat, DN, 1)], [F32, MXU_DTYPE], tm=tmb, ngroups=NH)
        cts = _dn_scan_bwd(f"d_dn_scan_{l}", do, sv["u"], sv["w"], sv["qd"], sv["kd"], sv["gl"], sv["attn"], sv["st"],
                           NH=NH, B=B)
        (dq, dk, dv, dba, dal, ddt), _ = _dn_prep_bwd(
            f"d_dn_prep_{l}", sv["qkv"], proj, 6 * C1, sv["al"], sv["dtb"], list(cts), F32, NH=NH)
        dqkv, gqw = [], []
        for k, dpart in enumerate((dq, dk, dv)):
            dxp, dwp, _ = _conv_bwd(f"d_qkv_conv_{l}_{k}", proj, (2 + k) * C1, dpart,
                                    qkv_w_full[l][:, k * DN:(k + 1) * DN], MXU_DTYPE, ts=tsq, tpe=S // tsq)
            dqkv.append(dxp)
            gqw.append(dwp)
        (du3, dcon) = _rows_vjp(f"d_conv_out_norm_{l}", _rms, [_whole(sv["u3"])], [], [row(conv_out_norm_w[l])],
                                [(dycat, C1, 0)], [MXU_DTYPE], tm=tmb)
        du2 = _mm(f"d_pw2_x_{l}", du3, W["w_pw2"], "nt")
        gw_pw2 = _mm(f"d_pw2_w_{l}", sv["u2"], du3, "tn", COMM_DTYPE)
        du1, dlnw, dlnb = _rows_vjp(f"d_ln_silu_{l}", _ln_silu, [_whole(sv["u1"])], [],
                                    [row(conv_ln_w[l]), row(conv_ln_b[l])], [_whole(du2)], [F32], tm=tmb)
        du0, gcw, gcb = _conv_bwd(f"d_conv_{l}", sv["u0"], 0, du1, conv_w_full[l], F32, ts=ts, tpe=S // ts)
        dval, dgate_c = _rows_vjp(f"d_glu_{l}", _glu, [(proj, C1, 0), (proj, C1, 1)], [], [], [_whole(du0)],
                                  [MXU_DTYPE, MXU_DTYPE], tm=tmb)
        dproj = jnp.concatenate([dval, dgate_c] + dqkv + [dz, dba.astype(MXU_DTYPE)], axis=1)
        dh1 = _mm(f"d_proj_x_{l}", dproj, W["w_in"], "nt", tk=896)
        gw_in = _mm(f"d_proj_w_{l}", sv["h1"], dproj, "tn", COMM_DTYPE, tn=896)[:, :IN]
        dxn, dyn, dg2p, dsh1, dsc1, dn1 = _rows_vjp(
            f"d_junction1_{l}", _junction, [_whole(sv["x_in"]), _whole(sv["y_in"])], [sv["gate_in"], sv["sh1"], sv["sc1"]],
            [row(norm1_w[l])], [_whole(dx0), _whole(dh1)], [F32, MXU_DTYPE], tm=tmb, tpe=S // tmb)
        dmods[l] = [dsh1, dsc1, dg1, dsh2, dsc2, dgate]
        small[l] = dict(norm1_w=dn1, conv_dw_w=gcw, conv_dw_b=gcb, conv_ln_w=dlnw, conv_ln_b=dlnb, conv_out_norm_w=dcon,
                        qkv_conv_w=jnp.concatenate(gqw, axis=1), a_log=dal[:, NH:2 * NH], dt_bias=ddt[:, NH:2 * NH],
                        dn_norm_w=ddn, norm2_w=dn2)
        shard_major = [jnp.moveaxis(gw_in.reshape(D, 4, IN // 4), 1, 0), gw_pw2.reshape(4, C1 // 4, C1),
                       gw_out.reshape(4, D // 4, D), jnp.moveaxis(gw_up.reshape(D, 4, FF // 4), 1, 0),
                       gw_down.reshape(4, FF // 4, D)]
        pending = [g.reshape(4, 2, g.shape[1] // 2, g.shape[2]) for g in shard_major]
        dx, dy, dgate = dxn, dyn, dg2p

    got = _run_comm("rs_swap_0", _swap_halves(pending))
    halves = [_add_half(f"rs_add_half_0_{a}", idx, g, r) for a, (g, r) in enumerate(zip(pending, got))]
    got = _run_comm("rs_scatter_0", _scatter_chips(halves))
    parts = [_add_chips(f"rs_add_chips_0_{a}", idx, h, r) for a, (h, r) in enumerate(zip(halves, got))]
    big_grads[0] = [f.reshape(-1, f.shape[-1]) for f in _run_comm("rs_join_0", _join_halves(parts))]

    grad_x = dx.reshape(B, S, D)

    small_names = ["norm1_w", "conv_dw_w", "conv_dw_b", "conv_ln_w", "conv_ln_b", "conv_out_norm_w", "qkv_conv_w", "a_log",
                   "dt_bias", "dn_norm_w", "norm2_w"]
    dmod_flat = jnp.concatenate([jnp.concatenate([t.reshape(B, D) for t in dmods[l]], axis=1) for l in range(L)]
                                + [dshf.reshape(B, D), dscf.reshape(B, D)], axis=1)
    small_list = [small[l][n] for l in range(L) for n in small_names] + [dfinal_norm]
    shapes3 = [dmod_flat.shape] + [t.shape for t in small_list]
    g3 = _all_gather8("gather_grads", _pack([dmod_flat] + small_list))
    rows3 = g3.shape[0] // 8
    g3 = g3.reshape(8, rows3, PACK_LANES)
    summed = _unpack(_sum0("sum_small_grads", g3).reshape(-1), shapes3)[1:]
    dmod_all = _unpack(g3.reshape(8, -1), shapes3[:1])[0].reshape(8 * B, -1)
    nm = dmod_all.shape[1]
    grad_b_all = _sum0("sum_mod_grads", dmod_all.reshape(8 * B, nm // PACK_LANES, PACK_LANES)).reshape(-1)
    grads = {}
    gwa = []
    for l in range(L):
        dm = lax.dynamic_slice(dmod_all, (0, l * 6 * D + s_me * NW), (8 * B, NW))
        gwa.append(_mm(f"d_ada_w_{l}", c_act, dm, "tn"))
    grads["w_ada"] = jnp.stack(gwa)
    grads["b_ada"] = grad_b_all[:L * 6 * D].reshape(L, 6 * D)
    dm = lax.dynamic_slice(dmod_all, (0, L * 6 * D + s_me * NF), (8 * B, NF))
    grads["final_ada_w"] = _mm("d_final_ada_w", c_act, dm, "tn")
    grads["final_ada_b"] = grad_b_all[L * 6 * D:]
    per_layer = {n: [] for n in small_names}
    for l in range(L):
        for k, n in enumerate(small_names):
            per_layer[n].append(summed[l * len(small_names) + k])
    for n in small_names:
        t = jnp.stack(per_layer[n])
        if n == "conv_dw_w":
            t = lax.dynamic_slice(t, (0, 0, s_me * (C1 // 4)), (L, KC, C1 // 4))
        elif n == "qkv_conv_w":
            t = lax.dynamic_slice(t, (0, 0, s_me * (3 * DN // 4)), (L, KQ, 3 * DN // 4))
        grads[n] = t.reshape(weights[n].shape)
    grads["final_norm_w"] = summed[-1].reshape(final_norm_w.shape)
    for k, n in enumerate(["w_in", "w_pw2", "w_out", "w_up", "w_down"]):
        grads[n] = jnp.stack([big_grads[l][k] for l in range(L)])

    delta, new_m, new_v = {}, {}, {}
    big_names = ["w_ada", "w_in", "w_pw2", "w_out", "w_up", "w_down", "final_ada_w"]
    for n in big_names:
        shp = weights[n].shape
        two = lambda t: t.reshape(-1, shp[-1])
        d_, m_, v_ = _adamw(f"adamw_{n}", two(weights[n]), two(grads[n]), two(mom1[n]), two(mom2[n]))
        delta[n], new_m[n], new_v[n] = d_.reshape(shp), m_.reshape(shp), v_.reshape(shp)
    rest = [n for n in names if n not in big_names]
    rshapes = [weights[n].shape for n in rest]
    packed = [_pack([d[n] for n in rest]) for d in (weights, grads, mom1, mom2)]
    outs = _adamw("adamw_small", *packed)
    for dst, arr in zip((delta, new_m, new_v), outs):
        for n, t in zip(rest, _unpack(arr.reshape(-1), rshapes)):
            dst[n] = t

    return (loss, grad_x, *[grads[n] for n in names], *[delta[n] for n in names], *[new_m[n] for n in names],
            *[new_v[n] for n in names])
```

```python
import functools
import math
import typing

import jax
import jax.numpy as jnp
from jax import lax
from jax.experimental import pallas as pl
from jax.experimental.pallas import tpu as pltpu

F32 = jnp.float32
MXU_DTYPE = jnp.bfloat16
COMM_DTYPE = jnp.bfloat16
HI = lax.Precision.HIGHEST
CHUNK = 64
PAIR = 2 * CHUNK
EPS = 1e-6
LANE = 128
SUB = 8
PACK_LANES = 1024
VMEM_LIMIT = 56 * 1024 * 1024
ADAM_LR, ADAM_B1, ADAM_B2, ADAM_EPS, ADAM_WD, ADAM_STEP = 0.001, 0.9, 0.999, 1e-08, 0.01, 10
MESH = pl.DeviceIdType.MESH
ANY = pl.BlockSpec(memory_space=pl.ANY)
NN, NT, TN = ((1,), (0,)), ((1,), (1,)), ((0,), (0,))


def _pick(dim, pref, mult):
    for t in range(min(pref, dim), 0, -1):
        if dim % t == 0 and t % mult == 0:
            return t
    return dim


def _params(sem):
    return pltpu.CompilerParams(dimension_semantics=sem, vmem_limit_bytes=VMEM_LIMIT)


def _sigmoid(v):
    return 1.0 / (1.0 + jnp.exp(-v))


def _silu(v):
    return v * _sigmoid(v)


def _row_specs(rows, exps, gls, tm, tpe):
    specs = [pl.BlockSpec((tm, w), functools.partial(lambda i, j: (i, j), j=cb)) for _, w, cb in rows]
    specs += [pl.BlockSpec((1, 1, e.shape[-1]), lambda i: (i // tpe, 0, 0)) for e in exps]
    specs += [pl.BlockSpec((1, g.shape[-1]), lambda i: (0, 0)) for g in gls]
    return specs


def _rows_fwd(name, fn, rows, exps, gls, outs, *, tm, tpe=1, ngroups=1, comm=None):
    T = rows[0][0].shape[0]
    nr, ne, ng = len(rows), len(exps), len(gls)
    nsteps = T // tm

    def body(*refs):
        r, e, g, o = refs[:nr], refs[nr:nr + ne], refs[nr + ne:nr + ne + ng], refs[nr + ne + ng:]
        ev = [t[0].astype(F32) for t in e]
        gv = [t[...].astype(F32) for t in g]
        for k in range(ngroups):
            rv = [t[:, k * (w // ngroups):(k + 1) * (w // ngroups)].astype(F32) for t, (_, w, _) in zip(r, rows)]
            res = fn(*rv, *ev, *gv)
            for oref, val, (w, dt) in zip(o, res, outs):
                gw = w // ngroups
                oref[:, k * gw:(k + 1) * gw] = val.astype(dt)

    kw = dict(in_specs=_row_specs(rows, exps, gls, tm, tpe),
              out_specs=[pl.BlockSpec((tm, w), lambda i: (i, 0)) for w, _ in outs],
              out_shape=[jax.ShapeDtypeStruct((T, w), dt) for w, dt in outs], scratch_shapes=[])
    body, kw, args = _attach(comm, body, kw, [a for a, _, _ in rows] + list(exps) + list(gls),
                             lambda: (pl.program_id(0) == 0, pl.program_id(0) == nsteps - 1))
    res = pl.pallas_call(body, name=name, grid=(nsteps,), compiler_params=_params(("arbitrary",)), **kw)(*args)
    return res if comm is None else (res[:len(outs)], res[len(outs):])


def _rows_vjp(name, fn, rows, exps, gls, cts, row_dtypes, *, tm, tpe=1, ngroups=1, primal=None):
    T = rows[0][0].shape[0]
    nr, ne, ng = len(rows), len(exps), len(gls)
    nc = 0 if cts is None else len(cts)
    keep = [k for k, dt in enumerate(row_dtypes) if dt is not None]
    npr = 0 if primal is None else len(primal)

    def body(*refs):
        r, e, g = refs[:nr], refs[nr:nr + ne], refs[nr + ne:nr + ne + ng]
        c = refs[nr + ne + ng:nr + ne + ng + nc]
        o = refs[nr + ne + ng + nc:]
        po, ro, eo, go = o[:npr], o[npr:npr + len(keep)], o[npr + len(keep):npr + len(keep) + ne], o[npr + len(keep) + ne:]
        i = pl.program_id(0)
        ev = [t[0].astype(F32) for t in e]
        gv = [t[...].astype(F32) for t in g]
        esum = [jnp.zeros_like(v) for v in ev]
        gsum = [jnp.zeros_like(v) for v in gv]
        for k in range(ngroups):
            rv = [t[:, k * (w // ngroups):(k + 1) * (w // ngroups)].astype(F32) for t, (_, w, _) in zip(r, rows)]
            res, pull = jax.vjp(fn, *rv, *ev, *gv)
            if cts is None:
                ct = tuple(jnp.ones_like(v) for v in res)
            else:
                ct = tuple(t[:, k * (w // ngroups):(k + 1) * (w // ngroups)].astype(F32) for t, (_, w, _) in zip(c, cts))
            grads = pull(ct)
            for oref, val, (w, dt) in zip(po, res, primal or ()):
                gw = w // ngroups
                oref[:, k * gw:(k + 1) * gw] = val.astype(dt)
            for oref, idx in zip(ro, keep):
                gw = rows[idx][1] // ngroups
                oref[:, k * gw:(k + 1) * gw] = grads[idx].astype(row_dtypes[idx])
            esum = [s + d for s, d in zip(esum, grads[nr:nr + ne])]
            gsum = [s + d for s, d in zip(gsum, grads[nr + ne:])]

        if ne:
            @pl.when(i % tpe == 0)
            def _():
                for oref in eo:
                    oref[...] = jnp.zeros_like(oref)
            for oref, s in zip(eo, esum):
                oref[0] += s
        if ng:
            @pl.when(i == 0)
            def _():
                for oref in go:
                    oref[...] = jnp.zeros_like(oref)
            for oref, s in zip(go, gsum):
                oref[...] += s

    out_specs = [pl.BlockSpec((tm, w), lambda i: (i, 0)) for w, _ in (primal or ())]
    out_shape = [jax.ShapeDtypeStruct((T, w), dt) for w, dt in (primal or ())]
    out_specs += [pl.BlockSpec((tm, rows[k][1]), lambda i: (i, 0)) for k in keep]
    out_shape += [jax.ShapeDtypeStruct((T, rows[k][1]), row_dtypes[k]) for k in keep]
    out_specs += [pl.BlockSpec((1, 1, e.shape[-1]), lambda i: (i // tpe, 0, 0)) for e in exps]
    out_shape += [jax.ShapeDtypeStruct(e.shape, F32) for e in exps]
    out_specs += [pl.BlockSpec((1, g.shape[-1]), lambda i: (0, 0)) for g in gls]
    out_shape += [jax.ShapeDtypeStruct(g.shape, F32) for g in gls]
    ct_specs = [] if cts is None else [pl.BlockSpec((tm, w), functools.partial(lambda i, j: (i, j), j=cb)) for _, w, cb in cts]
    ct_arrs = [] if cts is None else [a for a, _, _ in cts]
    return pl.pallas_call(
        body, name=name, grid=(T // tm,),
        in_specs=_row_specs(rows, exps, gls, tm, tpe) + ct_specs,
        out_specs=out_specs, out_shape=out_shape,
        compiler_params=_params(("arbitrary",)),
    )(*[a for a, _, _ in rows], *exps, *gls, *ct_arrs)


def _whole(a):
    return (a, a.shape[-1], 0)


def _junction(x, y, gate, shift, scale, w):
    xn = x + gate * y
    r = lax.rsqrt(jnp.mean(xn * xn, axis=-1, keepdims=True) + EPS)
    return xn, (xn * r * w) * (1.0 + scale) + shift


def _final_loss(x, y, tgt, gate, shift, scale, w):
    _, out = _junction(x, y, gate, shift, scale, w)
    err = out - tgt
    return (0.5 * jnp.mean(err * err, axis=-1, keepdims=True),)


def _glu(val, gate):
    return (val * _sigmoid(gate),)


def _ln_silu(u, w, b):
    xc = u - jnp.mean(u, axis=-1, keepdims=True)
    y = xc * lax.rsqrt(jnp.mean(xc * xc, axis=-1, keepdims=True) + EPS) * w + b
    return (_silu(y),)


def _rms(u, w):
    return (u * lax.rsqrt(jnp.mean(u * u, axis=-1, keepdims=True) + EPS) * w,)


def _gated_rms(o, z, w):
    return (o * lax.rsqrt(jnp.mean(o * o, axis=-1, keepdims=True) + EPS) * w * _silu(z),)


def _relu2(u):
    r = jnp.maximum(u, 0.0)
    return (r * r,)


def _silu_row(u):
    return (_silu(u),)


class _Comm(typing.NamedTuple):
    ins: list
    outs: list
    aliases: dict
    nsem: int
    plan: typing.Callable


def _attach(comm, body, kw, args, first_last):
    if comm is None:
        return body, kw, args
    ni0, no0, ns0 = len(kw["in_specs"]), len(kw["out_specs"]), len(kw["scratch_shapes"])
    ni, no = len(comm.ins), len(comm.outs)
    kw = dict(kw, in_specs=kw["in_specs"] + [ANY] * ni, out_specs=kw["out_specs"] + [ANY] * no,
              out_shape=kw["out_shape"] + comm.outs,
              scratch_shapes=kw["scratch_shapes"] + [pltpu.SemaphoreType.DMA((comm.nsem,)), pltpu.SemaphoreType.DMA((comm.nsem,))],
              input_output_aliases={ni0 + k: no0 + v for k, v in comm.aliases.items()})

    def carrying(*refs):
        own_in, c_in = refs[:ni0], refs[ni0:ni0 + ni]
        own_out, c_out = refs[ni0 + ni:ni0 + ni + no0], refs[ni0 + ni + no0:ni0 + ni + no0 + no]
        scratch = refs[ni0 + ni + no0 + no:]
        ssem, rsem = scratch[ns0], scratch[ns0 + 1]
        first, last = first_last()

        @pl.when(first)
        def _():
            for cp in comm.plan(c_in, c_out, ssem, rsem, True):
                cp.start()

        body(*own_in, *own_out, *scratch[:ns0])

        @pl.when(last)
        def _():
            sends, recvs = comm.plan(c_in, c_out, ssem, rsem, False)
            for cp in sends:
                cp.wait_send()
            for cp in recvs:
                cp.wait_recv()

    return carrying, kw, args + list(comm.ins)


def _run_comm(name, comm):
    ni, no = len(comm.ins), len(comm.outs)

    def body(*refs):
        for cp in comm.plan(refs[:ni], refs[ni:ni + no], refs[-2], refs[-1], True):
            cp.start()
        sends, recvs = comm.plan(refs[:ni], refs[ni:ni + no], refs[-2], refs[-1], False)
        for cp in sends:
            cp.wait_send()
        for cp in recvs:
            cp.wait_recv()

    return pl.pallas_call(
        body, name=name, in_specs=[ANY] * ni, out_specs=[ANY] * no, out_shape=comm.outs,
        input_output_aliases=comm.aliases,
        scratch_shapes=[pltpu.SemaphoreType.DMA((comm.nsem,)), pltpu.SemaphoreType.DMA((comm.nsem,))],
    )(*comm.ins)


def _mm(name, a, b, mode, out_dtype=F32, bias=None, tm=1024, tn=1024, tk=2048, comm=None, post=None, b_layer=None, stack=1):
    bshape = b.shape[1:] if b_layer is not None else b.shape
    if mode == "nn":
        (M, K), N = a.shape, bshape[1]
    elif mode == "nt":
        (M, K), N = a.shape, bshape[0]
    else:
        (K, M), N = a.shape, bshape[1] // stack
    tm, tn, tk = _pick(M, tm, LANE), _pick(N, tn, LANE), _pick(K, tk, LANE)
    nk, nj = K // tk, N // tn
    grid = (M // tm, stack * nj, nk)
    dn = {"nn": NN, "nt": NT, "tn": TN}[mode]
    a_spec = pl.BlockSpec((tk, tm), lambda i, j, k: (k, i)) if mode == "tn" else pl.BlockSpec((tm, tk), lambda i, j, k: (i, k))
    b_spec = pl.BlockSpec((tn, tk), lambda i, j, k: (j, k)) if mode == "nt" else pl.BlockSpec((tk, tn), lambda i, j, k: (k, j))
    if b_layer is not None:
        b_spec = pl.BlockSpec((1, tk, tn), lambda i, j, k: (b_layer, k, j))
    specs, args = [a_spec, b_spec], [a, b]
    if bias is not None:
        specs.append(pl.BlockSpec((1, tn), lambda i, j, k: (0, j)))
        args.append(bias)

    o_spec = pl.BlockSpec((tm, tn), lambda i, j, k: (i, j))
    if stack > 1:
        o_spec = pl.BlockSpec((1, tm, tn), lambda i, j, k: (j // nj, i, j % nj))
    fn, extra, out_dtypes = post if post is not None else (lambda t: (t,), [], [out_dtype])
    specs += [o_spec] * len(extra)
    args += list(extra)
    nin, nout = len(args), len(out_dtypes)

    def body(*refs):
        a_ref, b_ref = refs[0], refs[1]
        outs, acc = refs[nin:nin + nout], refs[-1]
        k = pl.program_id(2)

        @pl.when(k == 0)
        def _():
            acc[...] = jnp.zeros_like(acc)

        b_tile = b_ref[...] if b_layer is None else b_ref[0]
        acc[...] += lax.dot_general(a_ref[...].astype(MXU_DTYPE), b_tile.astype(MXU_DTYPE), (dn, ((), ())),
                                    preferred_element_type=F32)

        @pl.when(k == nk - 1)
        def _():
            res = acc[...]
            if bias is not None:
                res = res + refs[2][...]
            for o_ref, val in zip(outs, fn(res, *[r[...] for r in refs[nin - len(extra):nin]])):
                o_ref[...] = val.astype(o_ref.dtype).reshape(o_ref.shape)

    def first_last():
        at = [pl.program_id(d) for d in range(3)]
        first = jnp.logical_and(jnp.logical_and(at[0] == 0, at[1] == 0), at[2] == 0)
        last = jnp.logical_and(jnp.logical_and(at[0] == grid[0] - 1, at[1] == grid[1] - 1), at[2] == grid[2] - 1)
        return first, last

    oshape = (M, N) if stack == 1 else (stack, M, N)
    kw = dict(in_specs=specs, out_specs=[o_spec] * nout, out_shape=[jax.ShapeDtypeStruct(oshape, dt) for dt in out_dtypes],
              scratch_shapes=[pltpu.VMEM((tm, tn), F32)])
    body, kw, args = _attach(comm, body, kw, args, first_last)
    sem = ("arbitrary",) * 3 if comm is not None else ("parallel", "parallel", "arbitrary")
    res = pl.pallas_call(body, name=name, grid=grid, compiler_params=_params(sem), **kw)(*args)
    main = res[0] if nout == 1 else res[:nout]
    return main if comm is None else (main, res[nout:])


def _halo(K):
    return SUB * -(-(K - 1) // SUB)


def _conv_fwd(name, x, col0, w, bias, *, ts, tpe):
    T = x.shape[0]
    K, C = w.shape
    H = _halo(K)
    cb = _pick(C, 256, LANE)
    rb = _pick(ts, 64, SUB)
    off = col0 // cb
    specs = [pl.BlockSpec((ts, cb), lambda j, i: (i, off + j)),
             pl.BlockSpec((H, cb), lambda j, i: (jnp.maximum(i * (ts // H) - 1, 0), off + j)),
             pl.BlockSpec((K, cb), lambda j, i: (0, j))]
    args = [x, x, w]
    if bias is not None:
        specs.append(pl.BlockSpec((1, cb), lambda j, i: (0, j)))
        args.append(bias)

    def body(*refs):
        cur, halo, w_ref = refs[:3]
        o_ref, xp = refs[-2], refs[-1]
        i = pl.program_id(1)
        xp[0:H, :] = jnp.where(i % tpe == 0, 0.0, halo[...].astype(F32))
        xp[H:H + ts, :] = cur[...].astype(F32)
        for r0 in range(0, ts, rb):
            acc = jnp.zeros((rb, cb), F32) if bias is None else jnp.zeros((rb, cb), F32) + refs[3][...]
            for j in range(K):
                lo = H - (K - 1) + j + r0
                acc = acc + w_ref[j:j + 1, :] * xp[lo:lo + rb, :]
            o_ref[r0:r0 + rb, :] = acc

    return pl.pallas_call(
        body, name=name, grid=(C // cb, T // ts), in_specs=specs,
        out_specs=pl.BlockSpec((ts, cb), lambda j, i: (i, j)),
        out_shape=jax.ShapeDtypeStruct((T, C), F32),
        scratch_shapes=[pltpu.VMEM((H + ts, cb), F32)],
        compiler_params=_params(("parallel", "arbitrary")),
    )(*args)


def _conv_bwd(name, x, col0, dy, w, out_dtype, *, ts, tpe):
    T = x.shape[0]
    K, C = w.shape
    H = _halo(K)
    cb = _pick(C, 256, LANE)
    rb = _pick(ts, 64, SUB)
    off = col0 // cb
    nt = T // ts

    def body(cur, halo, dyc, dyn, w_ref, dx_ref, dw_ref, db_ref, xp, dyp):
        i = pl.program_id(1)
        xp[0:H, :] = jnp.where(i % tpe == 0, 0.0, halo[...].astype(F32))
        xp[H:H + ts, :] = cur[...].astype(F32)
        dyp[0:ts, :] = dyc[...]
        dyp[ts:ts + H, :] = jnp.where(i % tpe == tpe - 1, 0.0, dyn[...])
        for r0 in range(0, ts, rb):
            acc = jnp.zeros((rb, cb), F32)
            for j in range(K):
                lo = K - 1 - j + r0
                acc = acc + w_ref[j:j + 1, :] * dyp[lo:lo + rb, :]
            dx_ref[r0:r0 + rb, :] = acc.astype(out_dtype)

        @pl.when(i == 0)
        def _():
            dw_ref[...] = jnp.zeros_like(dw_ref)
            db_ref[...] = jnp.zeros_like(db_ref)

        for j in range(K):
            part = jnp.zeros((1, cb), F32)
            for r0 in range(0, ts, rb):
                lo = H - (K - 1) + j + r0
                part = part + jnp.sum(dyp[r0:r0 + rb, :] * xp[lo:lo + rb, :], axis=0, keepdims=True)
            dw_ref[j:j + 1, :] += part
        db_ref[...] += jnp.sum(dyc[...], axis=0, keepdims=True)

    return pl.pallas_call(
        body, name=name, grid=(C // cb, nt),
        in_specs=[pl.BlockSpec((ts, cb), lambda j, i: (i, off + j)),
                  pl.BlockSpec((H, cb), lambda j, i: (jnp.maximum(i * (ts // H) - 1, 0), off + j)),
                  pl.BlockSpec((ts, cb), lambda j, i: (i, j)),
                  pl.BlockSpec((H, cb), lambda j, i: (jnp.minimum((i + 1) * (ts // H), T // H - 1), j)),
                  pl.BlockSpec((K, cb), lambda j, i: (0, j))],
        out_specs=[pl.BlockSpec((ts, cb), lambda j, i: (i, j)),
                   pl.BlockSpec((K, cb), lambda j, i: (0, j)),
                   pl.BlockSpec((1, cb), lambda j, i: (0, j))],
        out_shape=[jax.ShapeDtypeStruct((T, C), out_dtype), jax.ShapeDtypeStruct((K, C), F32),
                   jax.ShapeDtypeStruct((1, C), F32)],
        scratch_shapes=[pltpu.VMEM((H + ts, cb), F32), pltpu.VMEM((ts + H, cb), F32)],
        compiler_params=_params(("parallel", "arbitrary")),
    )(x, x, dy, dy, w)


def _hdot(a, b, dn):
    return lax.dot_general(a, b, (dn, ((), ())), precision=HI, preferred_element_type=F32)


def _bdot(a, b, dn):
    return lax.dot_general(a.astype(MXU_DTYPE), b.astype(MXU_DTYPE), (dn, ((), ())), preferred_element_type=F32)


def _split(a):
    hi = a.astype(MXU_DTYPE)
    return hi, (a - hi.astype(F32)).astype(MXU_DTYPE)


def _dot3_raw(a, b, dn):
    if MXU_DTYPE == F32:
        return _hdot(a, b, dn)
    (ah, al), (bh, bl) = _split(a), _split(b)
    d = lambda p, q: lax.dot_general(p, q, (dn, ((), ())), preferred_element_type=F32)
    return d(ah, bh) + (d(ah, bl) + d(al, bh))


def _with_vjp(raw):
    dn = {"nn": NN, "nt": NT, "tn": TN}

    @functools.partial(jax.custom_vjp, nondiff_argnums=(2,))
    def dot(a, b, mode):
        return raw(a, b, dn[mode])

    def fwd(a, b, mode):
        return raw(a, b, dn[mode]), (a, b)

    def bwd(mode, res, ct):
        a, b = res
        if mode == "nn":
            return raw(ct, b, NT), raw(a, ct, TN)
        if mode == "nt":
            return raw(ct, b, NN), raw(ct, a, TN)
        return raw(b, ct, NT), raw(a, ct, NN)

    dot.defvjp(fwd, bwd)
    return dot


_dot_exact = _with_vjp(_hdot)
_dot3 = _with_vjp(_dot3_raw)
_dot1 = _with_vjp(_bdot)


@jax.custom_vjp
def _tri_inv(ns):
    C = ns[0].shape[0]
    eye = (lax.broadcasted_iota(jnp.int32, (C, C), 0) == lax.broadcasted_iota(jnp.int32, (C, C), 1)).astype(F32)
    ts = [eye + n for n in ns]
    ps = list(ns)
    for _ in range(int(math.log2(CHUNK)) - 1):
        ps = [_dot3_raw(p, p, NN) for p in ps]
        ts = [t + _dot3_raw(t, p, NN) for t, p in zip(ts, ps)]
    return tuple(ts)


def _tri_inv_fwd(ns):
    ts = _tri_inv(ns)
    return ts, ts


def _tri_inv_bwd(ts, cts):
    xs = [_dot3_raw(t, ct, TN) for t, ct in zip(ts, cts)]
    return (tuple(_dot3_raw(x, t, NT) for x, t in zip(xs, ts)),)


_tri_inv.defvjp(_tri_inv_fwd, _tri_inv_bwd)


def _dn_prep(qs, ks, vs, ba, alog, dtb, *, nh):
    C2, Dk = qs[0].shape
    z = ba + dtb
    g_all = -jnp.exp(alog) * (jnp.maximum(z, 0.0) + jnp.log(1.0 + jnp.exp(-jnp.abs(z))))
    ri = lax.broadcasted_iota(jnp.int32, (C2, C2), 0)
    cj = lax.broadcasted_iota(jnp.int32, (C2, C2), 1)
    same = jnp.logical_not(jnp.logical_xor(ri >= CHUNK, cj >= CHUNK))
    causal, strict = jnp.logical_and(ri >= cj, same), jnp.logical_and(ri > cj, same)
    gc_all = _dot_exact(causal.astype(F32), g_all, "nn")
    gc_rows = _dot_exact(g_all, jnp.logical_and(ri <= cj, same).astype(F32), "tn")
    gl_all = _dot_exact(same.astype(F32), g_all, "nn")
    lane = lax.broadcasted_iota(jnp.int32, (1, ba.shape[1]), 1)
    subl = lax.broadcasted_iota(jnp.int32, (ba.shape[1], 1), 0)
    heads = range(nh)
    sel = [(lane == nh + h).astype(F32) for h in heads]
    gc = [jnp.sum(gc_all * s, axis=1, keepdims=True) for s in sel]
    gl = [jnp.sum(gl_all * s, axis=1, keepdims=True) for s in sel]
    gcr = [jnp.sum(gc_rows * (subl == nh + h).astype(F32), axis=0, keepdims=True) for h in heads]
    decay = [jnp.where(causal, jnp.exp(jnp.where(causal, a - b, 0.0)), 0.0) for a, b in zip(gc, gcr)]
    beta = [_sigmoid(jnp.sum(ba * (lane == h).astype(F32), axis=1, keepdims=True)) for h in heads]
    q = [_silu(t) for t in qs]
    q = [t * lax.rsqrt(jnp.sum(t * t, axis=-1, keepdims=True) + EPS) * (Dk ** -0.5) for t in q]
    k = [_silu(t) for t in ks]
    k = [t * lax.rsqrt(jnp.sum(t * t, axis=-1, keepdims=True) + EPS) for t in k]
    kb = [a * b for a, b in zip(k, beta)]
    vb = [_silu(a) * b for a, b in zip(vs, beta)]
    kk = [_dot1(a, b, "nt") for a, b in zip(kb, k)]
    qk = [_dot1(a, b, "nt") for a, b in zip(q, k)]
    t = _tri_inv(tuple(-jnp.where(strict, a * d, 0.0) for a, d in zip(kk, decay)))
    egc = [jnp.exp(a) for a in gc]
    u = [_dot3(a, b, "nn") for a, b in zip(t, vb)]
    w = [_dot3(a, b * e, "nn") for a, b, e in zip(t, kb, egc)]
    attn = [jnp.where(causal, a * d, 0.0) for a, d in zip(qk, decay)]
    qd = [a * e for a, e in zip(q, egc)]
    kd = [a * jnp.exp(b - c) for a, b, c in zip(k, gl, gc)]
    glb = [jnp.exp(a) * jnp.ones((1, Dk), F32) for a in gl]
    return tuple(u), tuple(w), tuple(qd), tuple(kd), tuple(attn), tuple(glb)


def _dn_prep_specs(DN, col_q):
    qs = [pl.BlockSpec((PAIR, DN), functools.partial(lambda i, j: (i, j), j=j)) for j in range(3)]
    return qs + [pl.BlockSpec((PAIR, LANE), lambda i: (i, col_q)), pl.BlockSpec((1, LANE), lambda i: (0, 0)),
                 pl.BlockSpec((1, LANE), lambda i: (0, 0))]


def _heads(ref, NH, Dk):
    return tuple(ref[:, h * Dk:(h + 1) * Dk] for h in range(NH))


def _dn_prep_fwd(name, qkv, proj, ba_col, alog, dtb, *, NH, comm=None):
    T = qkv.shape[0]
    DN = qkv.shape[1] // 3
    Dk = DN // NH
    nsteps = T // PAIR

    def body(q_ref, k_ref, v_ref, ba_ref, al_ref, dt_ref, u_ref, w_ref, qd_ref, kd_ref, gl_ref, at_ref):
        res = _dn_prep(_heads(q_ref, NH, Dk), _heads(k_ref, NH, Dk), _heads(v_ref, NH, Dk), ba_ref[...], al_ref[...],
                       dt_ref[...], nh=NH)
        for h in range(NH):
            sl = slice(h * Dk, (h + 1) * Dk)
            u_ref[:, sl], w_ref[:, sl], qd_ref[:, sl], kd_ref[:, sl] = res[0][h], res[1][h], res[2][h], res[3][h]
            at_ref[h] = res[4][h]
            gl_ref[:, sl] = res[5][h]

    big = pl.BlockSpec((PAIR, DN), lambda i: (i, 0))
    kw = dict(in_specs=_dn_prep_specs(DN, ba_col // LANE),
              out_specs=[big] * 5 + [pl.BlockSpec((NH, PAIR, PAIR), lambda i: (0, i, 0))],
              out_shape=[jax.ShapeDtypeStruct((T, DN), F32)] * 5 + [jax.ShapeDtypeStruct((NH, T, PAIR), F32)],
              scratch_shapes=[])
    body, kw, args = _attach(comm, body, kw, [qkv, qkv, qkv, proj, alog, dtb],
                             lambda: (pl.program_id(0) == 0, pl.program_id(0) == nsteps - 1))
    res = pl.pallas_call(body, name=name, grid=(nsteps,), compiler_params=_params(("arbitrary",)), **kw)(*args)
    return res[:6], res[6:]


def _dn_prep_bwd(name, qkv, proj, ba_col, alog, dtb, cts, out_dtype, *, NH, comm=None):
    T = qkv.shape[0]
    DN = qkv.shape[1] // 3
    Dk = DN // NH
    nsteps = T // PAIR

    def body(q_ref, k_ref, v_ref, ba_ref, al_ref, dt_ref, du, dw, dqd, dkd, dgl, dat,
             dq_ref, dk_ref, dv_ref, dba_ref, dal_ref, ddt_ref):
        i = pl.program_id(0)
        _, pull = jax.vjp(functools.partial(_dn_prep, nh=NH), _heads(q_ref, NH, Dk), _heads(k_ref, NH, Dk),
                          _heads(v_ref, NH, Dk), ba_ref[...], al_ref[...], dt_ref[...])
        gq, gk, gv, gba, gal, gdt = pull((_heads(du, NH, Dk), _heads(dw, NH, Dk), _heads(dqd, NH, Dk),
                                          _heads(dkd, NH, Dk), tuple(dat[h] for h in range(NH)), _heads(dgl, NH, Dk)))
        for h in range(NH):
            sl = slice(h * Dk, (h + 1) * Dk)
            dq_ref[:, sl], dk_ref[:, sl], dv_ref[:, sl] = gq[h].astype(out_dtype), gk[h].astype(out_dtype), gv[h].astype(out_dtype)
        dba_ref[...] = gba.astype(out_dtype)

        @pl.when(i == 0)
        def _():
            dal_ref[...] = jnp.zeros_like(dal_ref)
            ddt_ref[...] = jnp.zeros_like(ddt_ref)

        dal_ref[...] += gal
        ddt_ref[...] += gdt

    big = pl.BlockSpec((PAIR, DN), lambda i: (i, 0))
    row = pl.BlockSpec((1, LANE), lambda i: (0, 0))
    kw = dict(in_specs=_dn_prep_specs(DN, ba_col // LANE) + [big] * 5 + [pl.BlockSpec((NH, PAIR, PAIR), lambda i: (0, i, 0))],
              out_specs=[big] * 3 + [pl.BlockSpec((PAIR, LANE), lambda i: (i, 0)), row, row],
              out_shape=[jax.ShapeDtypeStruct((T, DN), out_dtype)] * 3 + [jax.ShapeDtypeStruct((T, LANE), out_dtype),
                                                                           jax.ShapeDtypeStruct((1, LANE), F32),
                                                                           jax.ShapeDtypeStruct((1, LANE), F32)],
              scratch_shapes=[])
    body, kw, args = _attach(comm, body, kw, [qkv, qkv, qkv, proj, alog, dtb, *cts],
                             lambda: (pl.program_id(0) == 0, pl.program_id(0) == nsteps - 1))
    res = pl.pallas_call(body, name=name, grid=(nsteps,), compiler_params=_params(("arbitrary",)), **kw)(*args)
    return res[:6], res[6:]


def _dn_scan_fwd(name, u, w, qd, kd, gl, attn, *, NH, B):
    T, DN = u.shape
    Dk = DN // NH
    C = CHUNK
    NP = T // (B * PAIR)

    def body(u_ref, w_ref, qd_ref, kd_ref, gl_ref, at_ref, o_ref, st_ref, s_ref):
        @pl.when(pl.program_id(1) == 0)
        def _():
            s_ref[...] = jnp.zeros_like(s_ref)

        zeros = jnp.zeros((C, Dk), F32)
        for sub in range(2):
            rs = slice(sub * C, (sub + 1) * C)
            for h in range(NH):
                sl = slice(h * Dk, (h + 1) * Dk)
                s = s_ref[h]
                st_ref[sub, h] = s
                vnew = u_ref[rs, sl] - _bdot(w_ref[rs, sl], s, NN)
                vext = jnp.concatenate([vnew, zeros] if sub == 0 else [zeros, vnew], axis=0)
                o_ref[rs, sl] = _bdot(qd_ref[rs, sl], s, NN) + _bdot(at_ref[h, rs, :], vext, NN)
                s_ref[h] = s * gl_ref[sub * C:sub * C + 1, sl] + _bdot(kd_ref[rs, sl], vnew, TN)

    big = pl.BlockSpec((PAIR, DN), lambda b, n: (b * NP + n, 0))
    return pl.pallas_call(
        body, name=name, grid=(B, NP),
        in_specs=[big] * 5 + [pl.BlockSpec((NH, PAIR, PAIR), lambda b, n: (0, b * NP + n, 0))],
        out_specs=[big, pl.BlockSpec((2, NH, Dk, Dk), lambda b, n: (b * NP + n, 0, 0, 0))],
        out_shape=[jax.ShapeDtypeStruct((T, DN), F32), jax.ShapeDtypeStruct((T // C, NH, Dk, Dk), F32)],
        scratch_shapes=[pltpu.VMEM((NH, Dk, Dk), F32)],
        compiler_params=_params(("arbitrary", "arbitrary")),
    )(u, w, qd, kd, gl, attn)


def _dn_scan_bwd(name, do, u, w, qd, kd, gl, attn, st, *, NH, B):
    T, DN = u.shape
    Dk = DN // NH
    C = CHUNK
    NP = T // (B * PAIR)

    def body(do_ref, u_ref, w_ref, qd_ref, kd_ref, gl_ref, at_ref, st_ref,
             du_ref, dw_ref, dqd_ref, dkd_ref, dgl_ref, dat_ref, ds_ref):
        @pl.when(pl.program_id(1) == 0)
        def _():
            ds_ref[...] = jnp.zeros_like(ds_ref)

        row0 = lax.broadcasted_iota(jnp.int32, (C, Dk), 0) == 0
        zeros = jnp.zeros((C, Dk), F32)
        for sub in (1, 0):
            rs = slice(sub * C, (sub + 1) * C)
            for h in range(NH):
                sl = slice(h * Dk, (h + 1) * Dk)
                s, ds, g = st_ref[sub, h], ds_ref[h], do_ref[rs, sl]
                wv, at, kdv = w_ref[rs, sl], at_ref[h, rs, :], kd_ref[rs, sl]
                vnew = u_ref[rs, sl] - _bdot(wv, s, NN)
                vext = jnp.concatenate([vnew, zeros] if sub == 0 else [zeros, vnew], axis=0)
                dvnew = _bdot(at, g, TN)[rs] + _bdot(kdv, ds, NN)
                dat_ref[h, rs, :] = _bdot(g, vext, NT)
                dqd_ref[rs, sl] = _bdot(g, s, NT)
                dkd_ref[rs, sl] = _bdot(vnew, ds, NT)
                dgl_ref[rs, sl] = jnp.where(row0, jnp.sum(s * ds, axis=0, keepdims=True), 0.0)
                du_ref[rs, sl] = dvnew
                dw_ref[rs, sl] = -_bdot(dvnew, s, NT)
                ds_ref[h] = _bdot(qd_ref[rs, sl], g, TN) + ds * gl_ref[sub * C:sub * C + 1, sl] - _bdot(wv, dvnew, TN)

    big = pl.BlockSpec((PAIR, DN), lambda b, n: (b * NP + NP - 1 - n, 0))
    att = pl.BlockSpec((NH, PAIR, PAIR), lambda b, n: (0, b * NP + NP - 1 - n, 0))
    return pl.pallas_call(
        body, name=name, grid=(B, NP),
        in_specs=[big] * 6 + [att, pl.BlockSpec((2, NH, Dk, Dk), lambda b, n: (b * NP + NP - 1 - n, 0, 0, 0))],
        out_specs=[big] * 5 + [att],
        out_shape=[jax.ShapeDtypeStruct((T, DN), F32)] * 5 + [jax.ShapeDtypeStruct((NH, T, PAIR), F32)],
        scratch_shapes=[pltpu.VMEM((NH, Dk, Dk), F32)],
        compiler_params=_params(("arbitrary", "arbitrary")),
    )(do, u, w, qd, kd, gl, attn, st)


def _sum0(name, a):
    n, r, ln = a.shape
    tr = _pick(r, 64, SUB)

    def body(a_ref, o_ref):
        acc = a_ref[0]
        for k in range(1, n):
            acc = acc + a_ref[k]
        o_ref[...] = acc

    return pl.pallas_call(
        body, name=name, grid=(r // tr,),
        in_specs=[pl.BlockSpec((n, tr, ln), lambda i: (0, i, 0))],
        out_specs=pl.BlockSpec((tr, ln), lambda i: (i, 0)),
        out_shape=jax.ShapeDtypeStruct((r, ln), F32),
        compiler_params=_params(("arbitrary",)),
    )(a)


def _adamw(name, w, g, m, v):
    Lw, R, Cc = w.shape
    tr = _pick(R, max(SUB, (1 << 18) // Cc // SUB * SUB), SUB)
    c1 = 1.0 - ADAM_B1 ** ADAM_STEP
    c2 = 1.0 - ADAM_B2 ** ADAM_STEP

    def body(w_ref, g_ref, m_ref, v_ref, d_ref, mo_ref, vo_ref):
        gv = g_ref[...]
        mn = ADAM_B1 * m_ref[...] + (1.0 - ADAM_B1) * gv
        vn = ADAM_B2 * v_ref[...] + (1.0 - ADAM_B2) * (gv * gv)
        mo_ref[...] = mn
        vo_ref[...] = vn
        d_ref[...] = -ADAM_LR * ((mn / c1) / (jnp.sqrt(vn / c2) + ADAM_EPS) + ADAM_WD * w_ref[...])

    spec = pl.BlockSpec((1, tr, Cc), lambda l, i: (l, i, 0))
    return pl.pallas_call(
        body, name=name, grid=(Lw, R // tr), in_specs=[spec] * 4, out_specs=[spec] * 3,
        out_shape=[jax.ShapeDtypeStruct((Lw, R, Cc), F32)] * 3,
        compiler_params=_params(("arbitrary", "arbitrary")),
    )(w, g, m, v)


def _pack(arrs):
    flat = jnp.concatenate([a.reshape(-1).astype(F32) for a in arrs])
    pad = (-flat.shape[0]) % (SUB * PACK_LANES)
    return jnp.pad(flat, (0, pad)).reshape(-1, PACK_LANES)


def _unpack(flat, shapes):
    out, pos = [], 0
    for shp in shapes:
        n = math.prod(shp)
        out.append(flat[..., pos:pos + n].reshape(flat.shape[:-1] + tuple(shp)))
        pos += n
    return out


def _remote(src, dst, ssem, rsem, dev):
    return pltpu.make_async_remote_copy(src_ref=src, dst_ref=dst, send_sem=ssem, recv_sem=rsem, device_id=dev,
                                        device_id_type=MESH)


def _place():
    return lax.axis_index("x"), lax.axis_index("y"), lax.axis_index("c")


def _all_gather8(name, a):
    m, n = a.shape

    def body(x_ref, out_ref, send_sems, recv_sems, local_sem):
        x, y, c = _place()
        me, sibling = (x, y, c), (x, y, 1 - c)
        chips = [(1 - x, y), (x, 1 - y), (1 - x, 1 - y)]

        def rows(px, py, pc):
            return out_ref.at[pl.ds((4 * px + 2 * py + pc) * m, m), :]

        def copy(k, block, to, src=None):
            return _remote(rows(*block) if src is None else src, rows(*block), send_sems.at[k], recv_sems.at[k], to)

        mine = pltpu.make_async_copy(x_ref, rows(*me), local_sem)
        mine.start()
        first = [copy(0, me, sibling, src=x_ref)]
        first += [copy(1 + j, me, (*chip, c), src=x_ref) for j, chip in enumerate(chips)]
        for cp in first:
            cp.start()
        passed = [copy(4 + j, (*chip, c), sibling) for j, chip in enumerate(chips)]
        for j, chip in enumerate(chips):
            copy(1 + j, (*chip, c), me).wait_recv()
            passed[j].start()
        copy(0, sibling, me).wait_recv()
        for j, chip in enumerate(chips):
            copy(4 + j, (*chip, 1 - c), me).wait_recv()
        for cp in first + passed:
            cp.wait_send()
        mine.wait()

    return pl.pallas_call(
        body, name=name,
        out_shape=jax.ShapeDtypeStruct((8 * m, n), a.dtype),
        in_specs=[pl.BlockSpec(memory_space=pltpu.VMEM)],
        out_specs=pl.BlockSpec(memory_space=pltpu.VMEM),
        scratch_shapes=[pltpu.SemaphoreType.DMA((7,)), pltpu.SemaphoreType.DMA((7,)), pltpu.SemaphoreType.DMA],
        compiler_params=pltpu.CompilerParams(vmem_limit_bytes=VMEM_LIMIT),
    )(a)


def _chip_peers(x, y):
    return [(1 - x, y), (x, 1 - y), (1 - x, 1 - y)]


def _sds(a):
    return jax.ShapeDtypeStruct(a.shape, a.dtype)


def _cast_into_slot(name, idx, a, layer, dtype):
    _, R, Cc = a.shape
    tr = _pick(R, 256, 16)

    def body(i_ref, a_ref, o_ref):
        o_ref[0] = a_ref[0].astype(dtype)

    return pl.pallas_call(
        body, name=name,
        grid_spec=pltpu.PrefetchScalarGridSpec(
            num_scalar_prefetch=1, grid=(R // tr,),
            in_specs=[pl.BlockSpec((1, tr, Cc), lambda i, ix: (layer, i, 0))],
            out_specs=pl.BlockSpec((1, tr, Cc), lambda i, ix: (ix[0], i, 0))),
        out_shape=jax.ShapeDtypeStruct((4, R, Cc), dtype),
        compiler_params=_params(("arbitrary",)),
    )(idx, a)


def _gather_ici(bufs):
    n = len(bufs)

    def plan(ins, outs, ssem, rsem, starting):
        x, y, c = _place()
        sends, recvs = [], []
        for a in range(n):
            for k, (px, py) in enumerate(_chip_peers(x, y)):
                mine, got = outs[a].at[2 * x + y, c], outs[a].at[2 * px + py, c]
                sends.append(_remote(mine, mine, ssem.at[3 * a + k], rsem.at[3 * a + k], (px, py, c)))
                if not starting:
                    recvs.append(_remote(got, got, ssem.at[3 * a + k], rsem.at[3 * a + k], (px, py, c)))
        return sends if starting else (sends, recvs)

    return _Comm(list(bufs), [_sds(b) for b in bufs], {a: a for a in range(n)}, 3 * n, plan)


def _gather_d2d(bufs):
    n = len(bufs)

    def plan(ins, outs, ssem, rsem, starting):
        x, y, c = _place()
        sends, recvs = [], []
        for a in range(n):
            for k, (px, py) in enumerate(_chip_peers(x, y)):
                got, other = outs[a].at[2 * px + py, c], outs[a].at[2 * px + py, 1 - c]
                sends.append(_remote(got, got, ssem.at[3 * a + k], rsem.at[3 * a + k], (x, y, 1 - c)))
                if not starting:
                    recvs.append(_remote(other, other, ssem.at[3 * a + k], rsem.at[3 * a + k], (x, y, 1 - c)))
        return sends if starting else (sends, recvs)

    return _Comm(list(bufs), [_sds(b) for b in bufs], {a: a for a in range(n)}, 3 * n, plan)


def _swap_halves(grads):
    n = len(grads)

    def plan(ins, outs, ssem, rsem, starting):
        x, y, c = _place()
        cps = [_remote(ins[a].at[:, 1 - c], outs[a], ssem.at[a], rsem.at[a], (x, y, 1 - c)) for a in range(n)]
        return cps if starting else (cps, cps)

    return _Comm(list(grads), [jax.ShapeDtypeStruct((4,) + g.shape[2:], g.dtype) for g in grads], {}, n, plan)


def _scatter_chips(halves):
    n = len(halves)

    def plan(ins, outs, ssem, rsem, starting):
        x, y, c = _place()
        cps = []
        for a in range(n):
            for k, (px, py) in enumerate(_chip_peers(x, y)):
                cps.append(_remote(ins[a].at[2 * px + py], outs[a].at[k], ssem.at[3 * a + k], rsem.at[3 * a + k], (px, py, c)))
        return cps if starting else (cps, cps)

    return _Comm(list(halves), [jax.ShapeDtypeStruct((3,) + h.shape[1:], h.dtype) for h in halves], {}, 3 * n, plan)


def _join_halves(parts, layer):
    n = len(parts)

    def plan(ins, outs, ssem, rsem, starting):
        x, y, c = _place()
        sends = [_remote(outs[a].at[layer, c], outs[a].at[layer, c], ssem.at[a], rsem.at[a], (x, y, 1 - c)) for a in range(n)]
        if starting:
            return sends
        return sends, [_remote(outs[a].at[layer, 1 - c], outs[a].at[layer, 1 - c], ssem.at[a], rsem.at[a], (x, y, 1 - c))
                       for a in range(n)]

    return _Comm(list(parts), [_sds(p) for p in parts], {a: a for a in range(n)}, n, plan)


def _add_half(name, idx, g, r):
    _, _, Rh, Cc = g.shape
    tr = _pick(Rh, 512, 16)

    def body(i_ref, g_ref, r_ref, o_ref):
        o_ref[...] = (g_ref[0].astype(F32) + r_ref[...].astype(F32)).astype(o_ref.dtype)

    return pl.pallas_call(
        body, name=name,
        grid_spec=pltpu.PrefetchScalarGridSpec(
            num_scalar_prefetch=1, grid=(4, Rh // tr),
            in_specs=[pl.BlockSpec((1, 1, tr, Cc), lambda s, i, ix: (s, ix[1], i, 0)),
                      pl.BlockSpec((1, tr, Cc), lambda s, i, ix: (s, i, 0))],
            out_specs=pl.BlockSpec((1, tr, Cc), lambda s, i, ix: (s, i, 0))),
        out_shape=jax.ShapeDtypeStruct((4, Rh, Cc), g.dtype),
        compiler_params=_params(("arbitrary", "arbitrary")),
    )(idx, g, r)


def _add_chips(name, idx, h, r, layer, nlayers, acc):
    _, Rh, Cc = h.shape
    tr = _pick(Rh, 256, 16)

    def body(i_ref, h_ref, r0, r1, r2, *rest):
        rest[-1][0, 0] = ((h_ref[0].astype(F32) + r0[0].astype(F32)) + r1[0].astype(F32)) + r2[0].astype(F32)

    taken = [] if acc is None else [acc]
    return pl.pallas_call(
        body, name=name,
        grid_spec=pltpu.PrefetchScalarGridSpec(
            num_scalar_prefetch=1, grid=(Rh // tr,),
            in_specs=[pl.BlockSpec((1, tr, Cc), lambda i, ix: (ix[0], i, 0))]
            + [pl.BlockSpec((1, tr, Cc), functools.partial(lambda i, ix, k: (k, i, 0), k=k)) for k in range(3)]
            + [ANY] * len(taken),
            out_specs=pl.BlockSpec((1, 1, tr, Cc), lambda i, ix: (layer, ix[1], i, 0))),
        out_shape=jax.ShapeDtypeStruct((nlayers, 2, Rh, Cc), F32),
        input_output_aliases={5: 0} if taken else {},
        compiler_params=_params(("arbitrary",)),
    )(idx, h, r, r, r, *taken)


def kernel(x, c, w_ada, b_ada, norm1_w, w_in, conv_dw_w, conv_dw_b, conv_ln_w, conv_ln_b, w_pw2, conv_out_norm_w, qkv_conv_w, a_log, dt_bias, dn_norm_w, w_out, norm2_w, w_up, w_down, final_ada_w, final_ada_b, final_norm_w, loss_target, m_w_ada, m_b_ada, m_norm1_w, m_w_in, m_conv_dw_w, m_conv_dw_b, m_conv_ln_w, m_conv_ln_b, m_w_pw2, m_conv_out_norm_w, m_qkv_conv_w, m_a_log, m_dt_bias, m_dn_norm_w, m_w_out, m_norm2_w, m_w_up, m_w_down, m_final_ada_w, m_final_ada_b, m_final_norm_w, v_w_ada, v_b_ada, v_norm1_w, v_w_in, v_conv_dw_w, v_conv_dw_b, v_conv_ln_w, v_conv_ln_b, v_w_pw2, v_conv_out_norm_w, v_qkv_conv_w, v_a_log, v_dt_bias, v_dn_norm_w, v_w_out, v_norm2_w, v_w_up, v_w_down, v_final_ada_w, v_final_ada_b, v_final_norm_w):
    names = ["w_ada", "b_ada", "norm1_w", "w_in", "conv_dw_w", "conv_dw_b", "conv_ln_w", "conv_ln_b", "w_pw2",
             "conv_out_norm_w", "qkv_conv_w", "a_log", "dt_bias", "dn_norm_w", "w_out", "norm2_w", "w_up", "w_down",
             "final_ada_w", "final_ada_b", "final_norm_w"]
    weights = dict(zip(names, [w_ada, b_ada, norm1_w, w_in, conv_dw_w, conv_dw_b, conv_ln_w, conv_ln_b, w_pw2,
                               conv_out_norm_w, qkv_conv_w, a_log, dt_bias, dn_norm_w, w_out, norm2_w, w_up, w_down,
                               final_ada_w, final_ada_b, final_norm_w]))
    mom1 = dict(zip(names, [m_w_ada, m_b_ada, m_norm1_w, m_w_in, m_conv_dw_w, m_conv_dw_b, m_conv_ln_w, m_conv_ln_b,
                            m_w_pw2, m_conv_out_norm_w, m_qkv_conv_w, m_a_log, m_dt_bias, m_dn_norm_w, m_w_out,
                            m_norm2_w, m_w_up, m_w_down, m_final_ada_w, m_final_ada_b, m_final_norm_w]))
    mom2 = dict(zip(names, [v_w_ada, v_b_ada, v_norm1_w, v_w_in, v_conv_dw_w, v_conv_dw_b, v_conv_ln_w, v_conv_ln_b,
                            v_w_pw2, v_conv_out_norm_w, v_qkv_conv_w, v_a_log, v_dt_bias, v_dn_norm_w, v_w_out,
                            v_norm2_w, v_w_up, v_w_down, v_final_ada_w, v_final_ada_b, v_final_norm_w]))

    B, S, D = x.shape
    T = B * S
    L = w_in.shape[0]
    C1 = conv_ln_w.shape[-1]
    NH, DH = a_log.shape[-1], dn_norm_w.shape[-1]
    DN = NH * DH
    FF = w_down.shape[1] * 4
    IN = w_in.shape[-1] * 4
    INP = 6 * C1 + LANE
    KC, KQ = conv_dw_w.shape[1], qkv_conv_w.shape[1]
    NW, NF = w_ada.shape[-1], final_ada_w.shape[-1]
    assert DH == LANE and DN == C1 and IN == 6 * C1 + 2 * NH and S % PAIR == 0
    xi, yi, ci = _place()
    s_me, me = 2 * xi + yi, 4 * xi + 2 * yi + ci
    idx = jnp.stack([s_me, ci]).astype(jnp.int32)
    tmf, tmb = _pick(S, 256, SUB), _pick(S, 128, SUB)
    ts, tsq = _pick(S, 256, _halo(KC)), _pick(S, 512, _halo(KQ))

    shapes1 = [(B, D), conv_dw_w.shape, qkv_conv_w.shape]
    g1 = _pack([c, conv_dw_w, qkv_conv_w])
    g1 = _all_gather8("gather_cond", g1).reshape(8, -1)
    c_all, cw_all, qw_all = _unpack(g1, shapes1)
    c_all = c_all.reshape(8 * B, D)
    conv_w_full = jnp.moveaxis(cw_all[0::2], 0, 2).reshape(L, KC, C1)
    qkv_w_full = jnp.moveaxis(qw_all[0::2], 0, 2).reshape(L, KQ, 3 * DN)
    (c_act,) = _rows_fwd("cond_silu", _silu_row, [_whole(c_all)], [], [], [(D, F32)], tm=8 * B)

    mods = []
    for l in range(L):
        bsh = lax.dynamic_slice(b_ada[l], (s_me * NW,), (NW,)).reshape(1, NW)
        mods.append(_mm(f"mod_{l}", c_act, w_ada, "nn", bias=bsh, b_layer=l))
    bsh = lax.dynamic_slice(final_ada_b, (s_me * NF,), (NF,)).reshape(1, NF)
    mods.append(_mm("mod_final", c_act, final_ada_w, "nn", bias=bsh))
    shapes2 = [(8 * B, NW)] * L + [(8 * B, NF)]
    g2 = _all_gather8("gather_mod", _pack(mods)).reshape(8, -1)[0::2]
    mod_all = [jnp.moveaxis(t, 0, 1).reshape(8 * B, -1) for t in _unpack(g2, shapes2)]
    mod_me = [lax.dynamic_slice(t, (B * me, 0), (B, t.shape[1])) for t in mod_all]

    def split_mod(t, n):
        return [t[:, k * D:(k + 1) * D].reshape(B, 1, D) for k in range(n)]

    def cast_weights(l):
        bufs = [_cast_into_slot(f"cast_{l}_{k}", idx, a, l, MXU_DTYPE) for k, a in enumerate([w_in, w_pw2, w_out, w_up, w_down])]
        return [b.reshape(4, 2, b.shape[1] // 2, b.shape[2]) for b in bufs]

    def natural(got):
        gi, gp, go, gu, gd = got
        wi = jnp.pad(jnp.moveaxis(gi.reshape(4, D, IN // 4), 0, 1).reshape(D, IN), ((0, 0), (0, INP - IN)))
        return dict(w_in=wi, w_pw2=gp.reshape(C1, C1), w_out=go.reshape(D, D),
                    w_up=jnp.moveaxis(gu.reshape(4, D, FF // 4), 0, 1).reshape(D, FF), w_down=gd.reshape(FF, D))

    first = _run_comm("gather_w_ici_0", _gather_ici(cast_weights(0)))
    wfull = {0: natural(_run_comm("gather_w_d2d_0", _gather_d2d(first)))}

    pad_row = lambda v: jnp.pad(v.reshape(1, -1), ((0, 0), (NH, LANE - 2 * NH)))
    row = lambda v: v.reshape(1, -1)

    saved = []
    xcur = x.reshape(T, D)
    ycur = jnp.zeros((T, D), F32)
    gate_prev = jnp.zeros((B, 1, D), F32)
    for l in range(L):
        sh1, sc1, g1_, sh2, sc2, g2_ = split_mod(mod_me[l], 6)
        sv = dict(x_in=xcur, y_in=ycur, gate_in=gate_prev, sh1=sh1, sc1=sc1, g1=g1_, sh2=sh2, sc2=sc2, g2=g2_)
        junction1 = functools.partial(_rows_fwd, f"junction1_{l}", _junction, [_whole(xcur), _whole(ycur)],
                                      [gate_prev, sh1, sc1], [row(norm1_w[l])], [(D, F32), (D, MXU_DTYPE)],
                                      tm=tmf, tpe=S // tmf)
        if l == 0:
            x0, h1 = junction1()
        else:
            (x0, h1), got = junction1(comm=_gather_d2d(travelling))
            wfull[l] = natural(got)
        W = wfull[l]
        nxt = cast_weights(l + 1) if l + 1 < L else None
        if nxt is None:
            proj = _mm(f"proj_{l}", h1, W["w_in"], "nn", tn=896)
        else:
            proj, got_in = _mm(f"proj_{l}", h1, W["w_in"], "nn", tn=896, comm=_gather_ici(nxt[:1]))
        (u0,) = _rows_fwd(f"glu_{l}", _glu, [(proj, C1, 0), (proj, C1, 1)], [], [], [(C1, F32)], tm=tmf)
        u1 = _conv_fwd(f"conv_{l}", u0, 0, conv_w_full[l], row(conv_dw_b[l]), ts=ts, tpe=S // ts)
        (u2,) = _rows_fwd(f"ln_silu_{l}", _ln_silu, [_whole(u1)], [], [row(conv_ln_w[l]), row(conv_ln_b[l])],
                          [(C1, MXU_DTYPE)], tm=tmf)
        u3 = _mm(f"pw2_{l}", u2, W["w_pw2"], "nn")
        (y_conv,) = _rows_fwd(f"conv_out_norm_{l}", _rms, [_whole(u3)], [], [row(conv_out_norm_w[l])],
                              [(C1, MXU_DTYPE)], tm=tmf)
        qkv = _conv_fwd(f"qkv_conv_{l}", proj, 2 * C1, qkv_w_full[l], None, ts=tsq, tpe=S // tsq)
        al, dtb = pad_row(a_log[l]), pad_row(dt_bias[l])
        (u, w, qd, kd, gl, attn), got_po = _dn_prep_fwd(f"dn_prep_{l}", qkv, proj, 6 * C1, al, dtb, NH=NH,
                                                        comm=None if nxt is None else _gather_ici(nxt[1:3]))
        o, st = _dn_scan_fwd(f"dn_scan_{l}", u, w, qd, kd, gl, attn, NH=NH, B=B)
        (y_dn,) = _rows_fwd(f"dn_out_norm_{l}", _gated_rms, [_whole(o), (proj, DN, 5)], [], [row(dn_norm_w[l])],
                            [(DN, MXU_DTYPE)], tm=tmf, ngroups=NH)
        ycat = jnp.concatenate([y_conv, y_dn], axis=1)
        y = _mm(f"out_{l}", ycat, W["w_out"], "nn")
        x1, h2 = _rows_fwd(f"junction2_{l}", _junction, [_whole(x0), _whole(y)], [g1_, sh2, sc2],
                           [row(norm2_w[l])], [(D, F32), (D, MXU_DTYPE)], tm=tmf, tpe=S // tmf)
        relu2 = (lambda t: (t, _relu2(t)[0]), [], [F32, MXU_DTYPE])
        if nxt is not None:
            (up, act), got_up = _mm(f"up_{l}", h2, W["w_up"], "nn", post=relu2, comm=_gather_ici(nxt[3:4]))
            mlp, got_down = _mm(f"down_{l}", act, W["w_down"], "nn", comm=_gather_ici(nxt[4:]))
            travelling = list(got_in) + list(got_po) + list(got_up) + list(got_down)
        else:
            up, act = _mm(f"up_{l}", h2, W["w_up"], "nn", post=relu2)
            mlp = _mm(f"down_{l}", act, W["w_down"], "nn")
        sv.update(x0=x0, h1=h1, proj=proj, u0=u0, u1=u1, u2=u2, u3=u3, qkv=qkv, al=al, dtb=dtb, u=u, w=w, qd=qd, kd=kd,
                  gl=gl, attn=attn, o=o, st=st, ycat=ycat, y=y, x1=x1, h2=h2, up=up, act=act)
        saved.append(sv)
        xcur, ycur, gate_prev = x1, mlp, g2_

    shf, scf = split_mod(mod_me[L], 2)
    tgt = loss_target.reshape(T, D)
    rowloss, dx, dy, dgate, dshf, dscf, dfinal_norm = _rows_vjp(
        "loss_head", _final_loss, [_whole(xcur), _whole(ycur), _whole(tgt)], [gate_prev, shf, scf], [row(final_norm_w)],
        None, [F32, MXU_DTYPE, None], tm=tmb, tpe=S // tmb, primal=[(1, F32)])
    loss = lax.psum(jnp.sum(rowloss), ("x", "y", "c"))

    dmods = [None] * L
    small = [None] * L
    stacked = [None] * 5
    pending = None
    for l in reversed(range(L)):
        W, sv = wfull[l], saved[l]
        d_relu2 = (lambda t, u_: (t * (2.0 * jnp.maximum(u_, 0.0)),), [sv["up"]], [MXU_DTYPE])
        if pending is None:
            dup = _mm(f"d_down_x_{l}", dy, W["w_down"], "nt", post=d_relu2)
            gw_down = _mm(f"d_down_w_{l}", sv["act"], dy, "tn", COMM_DTYPE)
            dh2 = _mm(f"d_up_x_{l}", dup, W["w_up"], "nt")
            gw_up = _mm(f"d_up_w_{l}", sv["h2"], dup, "tn", COMM_DTYPE)
        else:
            dup, got = _mm(f"d_down_x_{l}", dy, W["w_down"], "nt", post=d_relu2, comm=_swap_halves(pending))
            halves = [_add_half(f"rs_add_half_{l + 1}_{a}", idx, g, r) for a, (g, r) in enumerate(zip(pending, got))]
            gw_down, got_down = _mm(f"d_down_w_{l}", sv["act"], dy, "tn", COMM_DTYPE, comm=_scatter_chips(halves[4:]))
            dh2, got_in = _mm(f"d_up_x_{l}", dup, W["w_up"], "nt", comm=_scatter_chips(halves[:1]))
            gw_up, got_po = _mm(f"d_up_w_{l}", sv["h2"], dup, "tn", COMM_DTYPE, comm=_scatter_chips(halves[1:3]))
        dx0, dyo, dg1, dsh2, dsc2, dn2 = _rows_vjp(
            f"d_junction2_{l}", _junction, [_whole(sv["x0"]), _whole(sv["y"])], [sv["g1"], sv["sh2"], sv["sc2"]],
            [row(norm2_w[l])], [_whole(dx), _whole(dh2)], [F32, MXU_DTYPE], tm=tmb, tpe=S // tmb)
        dycat = _mm(f"d_out_x_{l}", dyo, W["w_out"], "nt")
        gw_out = _mm(f"d_out_w_{l}", sv["ycat"], dyo, "tn", COMM_DTYPE)
        proj = sv["proj"]
        do, dz, ddn = _rows_vjp(f"d_dn_out_norm_{l}", _gated_rms, [_whole(sv["o"]), (proj, DN, 5)], [],
                                [row(dn_norm_w[l])], [(dycat, DN, 1)], [F32, MXU_DTYPE], tm=tmb, ngroups=NH)
        cts = _dn_scan_bwd(f"d_dn_scan_{l}", do, sv["u"], sv["w"], sv["qd"], sv["kd"], sv["gl"], sv["attn"], sv["st"],
                           NH=NH, B=B)
        (dq, dk, dv, dba, dal, ddt), got_up = _dn_prep_bwd(
            f"d_dn_prep_{l}", sv["qkv"], proj, 6 * C1, sv["al"], sv["dtb"], list(cts), F32, NH=NH,
            comm=None if pending is None else _scatter_chips(halves[3:4]))
        if pending is not None:
            got = list(got_in) + list(got_po) + list(got_up) + list(got_down)
            parts = [_add_chips(f"rs_add_chips_{l + 1}_{a}", idx, h, r, l + 1, L, stacked[a])
                     for a, (h, r) in enumerate(zip(halves, got))]
        dqkv, gqw = [], []
        for k, dpart in enumerate((dq, dk, dv)):
            dxp, dwp, _ = _conv_bwd(f"d_qkv_conv_{l}_{k}", proj, (2 + k) * C1, dpart,
                                    qkv_w_full[l][:, k * DN:(k + 1) * DN], MXU_DTYPE, ts=tsq, tpe=S // tsq)
            dqkv.append(dxp)
            gqw.append(dwp)
        (du3, dcon) = _rows_vjp(f"d_conv_out_norm_{l}", _rms, [_whole(sv["u3"])], [], [row(conv_out_norm_w[l])],
                                [(dycat, C1, 0)], [MXU_DTYPE], tm=tmb)
        du2 = _mm(f"d_pw2_x_{l}", du3, W["w_pw2"], "nt")
        gw_pw2 = _mm(f"d_pw2_w_{l}", sv["u2"], du3, "tn", COMM_DTYPE)
        du1, dlnw, dlnb = _rows_vjp(f"d_ln_silu_{l}", _ln_silu, [_whole(sv["u1"])], [],
                                    [row(conv_ln_w[l]), row(conv_ln_b[l])], [_whole(du2)], [F32], tm=tmb)
        du0, gcw, gcb = _conv_bwd(f"d_conv_{l}", sv["u0"], 0, du1, conv_w_full[l], F32, ts=ts, tpe=S // ts)
        dval, dgate_c = _rows_vjp(f"d_glu_{l}", _glu, [(proj, C1, 0), (proj, C1, 1)], [], [], [_whole(du0)],
                                  [MXU_DTYPE, MXU_DTYPE], tm=tmb)
        dproj = jnp.concatenate([dval, dgate_c] + dqkv + [dz, dba.astype(MXU_DTYPE)], axis=1)
        if pending is None:
            dh1 = _mm(f"d_proj_x_{l}", dproj, W["w_in"], "nt", tk=896)
        else:
            dh1, stacked = _mm(f"d_proj_x_{l}", dproj, W["w_in"], "nt", tk=896, comm=_join_halves(parts, l + 1))
        gw_in = _mm(f"d_proj_w_{l}", sv["h1"], dproj, "tn", COMM_DTYPE, tn=896)[:, :IN]
        dxn, dyn, dg2p, dsh1, dsc1, dn1 = _rows_vjp(
            f"d_junction1_{l}", _junction, [_whole(sv["x_in"]), _whole(sv["y_in"])], [sv["gate_in"], sv["sh1"], sv["sc1"]],
            [row(norm1_w[l])], [_whole(dx0), _whole(dh1)], [F32, MXU_DTYPE], tm=tmb, tpe=S // tmb)
        dmods[l] = [dsh1, dsc1, dg1, dsh2, dsc2, dgate]
        small[l] = dict(norm1_w=dn1, conv_dw_w=gcw, conv_dw_b=gcb, conv_ln_w=dlnw, conv_ln_b=dlnb, conv_out_norm_w=dcon,
                        qkv_conv_w=jnp.concatenate(gqw, axis=1), a_log=dal[:, NH:2 * NH], dt_bias=ddt[:, NH:2 * NH],
                        dn_norm_w=ddn, norm2_w=dn2)
        shard_major = [jnp.moveaxis(gw_in.reshape(D, 4, IN // 4), 1, 0), gw_pw2.reshape(4, C1 // 4, C1),
                       gw_out.reshape(4, D // 4, D), jnp.moveaxis(gw_up.reshape(D, 4, FF // 4), 1, 0),
                       gw_down.reshape(4, FF // 4, D)]
        pending = [g.reshape(4, 2, g.shape[1] // 2, g.shape[2]) for g in shard_major]
        dx, dy, dgate = dxn, dyn, dg2p

    got = _run_comm("rs_swap_0", _swap_halves(pending))
    halves = [_add_half(f"rs_add_half_0_{a}", idx, g, r) for a, (g, r) in enumerate(zip(pending, got))]
    got = _run_comm("rs_scatter_0", _scatter_chips(halves))
    parts = [_add_chips(f"rs_add_chips_0_{a}", idx, h, r, 0, L, stacked[a]) for a, (h, r) in enumerate(zip(halves, got))]
    stacked = _run_comm("rs_join_0", _join_halves(parts, 0))

    grad_x = dx.reshape(B, S, D)

    small_names = ["norm1_w", "conv_dw_w", "conv_dw_b", "conv_ln_w", "conv_ln_b", "conv_out_norm_w", "qkv_conv_w", "a_log",
                   "dt_bias", "dn_norm_w", "norm2_w"]
    dmod_flat = jnp.concatenate([jnp.concatenate([t.reshape(B, D) for t in dmods[l]], axis=1) for l in range(L)]
                                + [dshf.reshape(B, D), dscf.reshape(B, D)], axis=1)
    small_list = [small[l][n] for l in range(L) for n in small_names] + [dfinal_norm]
    shapes3 = [dmod_flat.shape] + [t.shape for t in small_list]
    g3 = _all_gather8("gather_grads", _pack([dmod_flat] + small_list))
    rows3 = g3.shape[0] // 8
    g3 = g3.reshape(8, rows3, PACK_LANES)
    summed = _unpack(_sum0("sum_small_grads", g3).reshape(-1), shapes3)[1:]
    dmod_all = _unpack(g3.reshape(8, -1), shapes3[:1])[0].reshape(8 * B, -1)
    nm = dmod_all.shape[1]
    grad_b_all = _sum0("sum_mod_grads", dmod_all.reshape(8 * B, nm // PACK_LANES, PACK_LANES)).reshape(-1)
    grads = {}
    dm = jnp.concatenate([lax.dynamic_slice(dmod_all, (0, l * 6 * D + s_me * NW), (8 * B, NW)) for l in range(L)], axis=1)
    grads["w_ada"] = _mm("d_ada_w", c_act, dm, "tn", stack=L)
    grads["b_ada"] = grad_b_all[:L * 6 * D].reshape(L, 6 * D)
    dm = lax.dynamic_slice(dmod_all, (0, L * 6 * D + s_me * NF), (8 * B, NF))
    grads["final_ada_w"] = _mm("d_final_ada_w", c_act, dm, "tn")
    grads["final_ada_b"] = grad_b_all[L * 6 * D:]
    per_layer = {n: [] for n in small_names}
    for l in range(L):
        for k, n in enumerate(small_names):
            per_layer[n].append(summed[l * len(small_names) + k])
    for n in small_names:
        t = jnp.stack(per_layer[n])
        if n == "conv_dw_w":
            t = lax.dynamic_slice(t, (0, 0, s_me * (C1 // 4)), (L, KC, C1 // 4))
        elif n == "qkv_conv_w":
            t = lax.dynamic_slice(t, (0, 0, s_me * (3 * DN // 4)), (L, KQ, 3 * DN // 4))
        grads[n] = t.reshape(weights[n].shape)
    grads["final_norm_w"] = summed[-1].reshape(final_norm_w.shape)
    for k, n in enumerate(["w_in", "w_pw2", "w_out", "w_up", "w_down"]):
        grads[n] = stacked[k].reshape(weights[n].shape)

    delta, new_m, new_v = {}, {}, {}
    big_names = ["w_ada", "w_in", "w_pw2", "w_out", "w_up", "w_down", "final_ada_w"]
    for n in big_names:
        shp = weights[n].shape
        three = lambda t: t.reshape((-1,) + shp[-2:])
        d_, m_, v_ = _adamw(f"adamw_{n}", three(weights[n]), three(grads[n]), three(mom1[n]), three(mom2[n]))
        delta[n], new_m[n], new_v[n] = d_.reshape(shp), m_.reshape(shp), v_.reshape(shp)
    rest = [n for n in names if n not in big_names]
    rshapes = [weights[n].shape for n in rest]
    packed = [_pack([d[n] for n in rest])[None] for d in (weights, grads, mom1, mom2)]
    outs = _adamw("adamw_small", *packed)
    for dst, arr in zip((delta, new_m, new_v), outs):
        for n, t in zip(rest, _unpack(arr.reshape(-1), rshapes)):
            dst[n] = t

    return (loss, grad_x, *[grads[n] for n in names], *[delta[n] for n in names], *[new_m[n] for n in names],
            *[new_v[n] for n in names])
```

```python
import functools
import math
import typing

import jax
import jax.numpy as jnp
from jax import lax
from jax.experimental import pallas as pl
from jax.experimental.pallas import tpu as pltpu

F32 = jnp.float32
MXU_DTYPE = jnp.bfloat16
COMM_DTYPE = jnp.bfloat16
HI = lax.Precision.HIGHEST
CHUNK = 64
PAIR = 2 * CHUNK
EPS = 1e-6
LANE = 128
SUB = 8
PACK_LANES = 1024
VMEM_LIMIT = 56 * 1024 * 1024
ADAM_LR, ADAM_B1, ADAM_B2, ADAM_EPS, ADAM_WD, ADAM_STEP = 0.001, 0.9, 0.999, 1e-08, 0.01, 10
MESH = pl.DeviceIdType.MESH
ANY = pl.BlockSpec(memory_space=pl.ANY)
NN, NT, TN = ((1,), (0,)), ((1,), (1,)), ((0,), (0,))


def _pick(dim, pref, mult):
    for t in range(min(pref, dim), 0, -1):
        if dim % t == 0 and t % mult == 0:
            return t
    return dim


def _params(sem):
    return pltpu.CompilerParams(dimension_semantics=sem, vmem_limit_bytes=VMEM_LIMIT)


def _sigmoid(v):
    return 1.0 / (1.0 + jnp.exp(-v))


def _silu(v):
    return v * _sigmoid(v)


def _row_specs(rows, exps, gls, tm, tpe):
    specs = [pl.BlockSpec((tm, w), functools.partial(lambda i, j: (i, j), j=cb)) for _, w, cb in rows]
    specs += [pl.BlockSpec((1, 1, e.shape[-1]), lambda i: (i // tpe, 0, 0)) for e in exps]
    specs += [pl.BlockSpec((1, g.shape[-1]), lambda i: (0, 0)) for g in gls]
    return specs


def _rows_fwd(name, fn, rows, exps, gls, outs, *, tm, tpe=1, ngroups=1, comm=None):
    T = rows[0][0].shape[0]
    nr, ne, ng = len(rows), len(exps), len(gls)
    nsteps = T // tm

    def body(*refs):
        r, e, g, o = refs[:nr], refs[nr:nr + ne], refs[nr + ne:nr + ne + ng], refs[nr + ne + ng:]
        ev = [t[0].astype(F32) for t in e]
        gv = [t[...].astype(F32) for t in g]
        for k in range(ngroups):
            rv = [t[:, k * (w // ngroups):(k + 1) * (w // ngroups)].astype(F32) for t, (_, w, _) in zip(r, rows)]
            res = fn(*rv, *ev, *gv)
            for oref, val, (w, dt) in zip(o, res, outs):
                gw = w // ngroups
                oref[:, k * gw:(k + 1) * gw] = val.astype(dt)

    kw = dict(in_specs=_row_specs(rows, exps, gls, tm, tpe),
              out_specs=[pl.BlockSpec((tm, w), lambda i: (i, 0)) for w, _ in outs],
              out_shape=[jax.ShapeDtypeStruct((T, w), dt) for w, dt in outs], scratch_shapes=[])
    body, kw, args = _attach(comm, body, kw, [a for a, _, _ in rows] + list(exps) + list(gls),
                             lambda: (pl.program_id(0) == 0, pl.program_id(0) == nsteps - 1))
    res = pl.pallas_call(body, name=name, grid=(nsteps,), compiler_params=_params(("arbitrary",)), **kw)(*args)
    return res if comm is None else (res[:len(outs)], res[len(outs):])


def _rows_vjp(name, fn, rows, exps, gls, cts, row_dtypes, *, tm, tpe=1, ngroups=1, primal=None, comm=None):
    T = rows[0][0].shape[0]
    nr, ne, ng = len(rows), len(exps), len(gls)
    nc = 0 if cts is None else len(cts)
    keep = [k for k, dt in enumerate(row_dtypes) if dt is not None]
    npr = 0 if primal is None else len(primal)

    def body(*refs):
        r, e, g = refs[:nr], refs[nr:nr + ne], refs[nr + ne:nr + ne + ng]
        c = refs[nr + ne + ng:nr + ne + ng + nc]
        o = refs[nr + ne + ng + nc:]
        po, ro, eo, go = o[:npr], o[npr:npr + len(keep)], o[npr + len(keep):npr + len(keep) + ne], o[npr + len(keep) + ne:]
        i = pl.program_id(0)
        ev = [t[0].astype(F32) for t in e]
        gv = [t[...].astype(F32) for t in g]
        esum = [jnp.zeros_like(v) for v in ev]
        gsum = [jnp.zeros_like(v) for v in gv]
        for k in range(ngroups):
            rv = [t[:, k * (w // ngroups):(k + 1) * (w // ngroups)].astype(F32) for t, (_, w, _) in zip(r, rows)]
            res, pull = jax.vjp(fn, *rv, *ev, *gv)
            if cts is None:
                ct = tuple(jnp.ones_like(v) for v in res)
            else:
                ct = tuple(t[:, k * (w // ngroups):(k + 1) * (w // ngroups)].astype(F32) for t, (_, w, _) in zip(c, cts))
            grads = pull(ct)
            for oref, val, (w, dt) in zip(po, res, primal or ()):
                gw = w // ngroups
                oref[:, k * gw:(k + 1) * gw] = val.astype(dt)
            for oref, idx in zip(ro, keep):
                gw = rows[idx][1] // ngroups
                oref[:, k * gw:(k + 1) * gw] = grads[idx].astype(row_dtypes[idx])
            esum = [s + d for s, d in zip(esum, grads[nr:nr + ne])]
            gsum = [s + d for s, d in zip(gsum, grads[nr + ne:])]

        if ne:
            @pl.when(i % tpe == 0)
            def _():
                for oref in eo:
                    oref[...] = jnp.zeros_like(oref)
            for oref, s in zip(eo, esum):
                oref[0] += s
        if ng:
            @pl.when(i == 0)
            def _():
                for oref in go:
                    oref[...] = jnp.zeros_like(oref)
            for oref, s in zip(go, gsum):
                oref[...] += s

    out_specs = [pl.BlockSpec((tm, w), lambda i: (i, 0)) for w, _ in (primal or ())]
    out_shape = [jax.ShapeDtypeStruct((T, w), dt) for w, dt in (primal or ())]
    out_specs += [pl.BlockSpec((tm, rows[k][1]), lambda i: (i, 0)) for k in keep]
    out_shape += [jax.ShapeDtypeStruct((T, rows[k][1]), row_dtypes[k]) for k in keep]
    out_specs += [pl.BlockSpec((1, 1, e.shape[-1]), lambda i: (i // tpe, 0, 0)) for e in exps]
    out_shape += [jax.ShapeDtypeStruct(e.shape, F32) for e in exps]
    out_specs += [pl.BlockSpec((1, g.shape[-1]), lambda i: (0, 0)) for g in gls]
    out_shape += [jax.ShapeDtypeStruct(g.shape, F32) for g in gls]
    ct_specs = [] if cts is None else [pl.BlockSpec((tm, w), functools.partial(lambda i, j: (i, j), j=cb)) for _, w, cb in cts]
    ct_arrs = [] if cts is None else [a for a, _, _ in cts]
    nsteps, nown = T // tm, len(out_shape)
    kw = dict(in_specs=_row_specs(rows, exps, gls, tm, tpe) + ct_specs, out_specs=out_specs, out_shape=out_shape,
              scratch_shapes=[])
    body, kw, args = _attach(comm, body, kw, [a for a, _, _ in rows] + list(exps) + list(gls) + ct_arrs,
                             lambda: (pl.program_id(0) == 0, pl.program_id(0) == nsteps - 1))
    res = pl.pallas_call(body, name=name, grid=(nsteps,), compiler_params=_params(("arbitrary",)), **kw)(*args)
    return res if comm is None else (res[:nown], res[nown:])


def _whole(a):
    return (a, a.shape[-1], 0)


def _junction(x, y, gate, shift, scale, w):
    xn = x + gate * y
    r = lax.rsqrt(jnp.mean(xn * xn, axis=-1, keepdims=True) + EPS)
    return xn, (xn * r * w) * (1.0 + scale) + shift


def _final_loss(x, y, tgt, gate, shift, scale, w):
    _, out = _junction(x, y, gate, shift, scale, w)
    err = out - tgt
    return (0.5 * jnp.mean(err * err, axis=-1, keepdims=True),)


def _glu(val, gate):
    return (val * _sigmoid(gate),)


def _ln_silu(u, w, b):
    xc = u - jnp.mean(u, axis=-1, keepdims=True)
    y = xc * lax.rsqrt(jnp.mean(xc * xc, axis=-1, keepdims=True) + EPS) * w + b
    return (_silu(y),)


def _rms(u, w):
    return (u * lax.rsqrt(jnp.mean(u * u, axis=-1, keepdims=True) + EPS) * w,)


def _gated_rms(o, z, w):
    return (o * lax.rsqrt(jnp.mean(o * o, axis=-1, keepdims=True) + EPS) * w * _silu(z),)


def _relu2(u):
    r = jnp.maximum(u, 0.0)
    return (r * r,)


def _silu_row(u):
    return (_silu(u),)


class _Comm(typing.NamedTuple):
    ins: list
    outs: list
    aliases: dict
    nsem: int
    plan: typing.Callable


def _attach(comm, body, kw, args, first_last):
    if comm is None:
        return body, kw, args
    ni0, no0, ns0 = len(kw["in_specs"]), len(kw["out_specs"]), len(kw["scratch_shapes"])
    ni, no = len(comm.ins), len(comm.outs)
    kw = dict(kw, in_specs=kw["in_specs"] + [ANY] * ni, out_specs=kw["out_specs"] + [ANY] * no,
              out_shape=kw["out_shape"] + comm.outs,
              scratch_shapes=kw["scratch_shapes"] + [pltpu.SemaphoreType.DMA((comm.nsem,)), pltpu.SemaphoreType.DMA((comm.nsem,))],
              input_output_aliases={ni0 + k: no0 + v for k, v in comm.aliases.items()})

    def carrying(*refs):
        own_in, c_in = refs[:ni0], refs[ni0:ni0 + ni]
        own_out, c_out = refs[ni0 + ni:ni0 + ni + no0], refs[ni0 + ni + no0:ni0 + ni + no0 + no]
        scratch = refs[ni0 + ni + no0 + no:]
        ssem, rsem = scratch[ns0], scratch[ns0 + 1]
        first, last = first_last()

        @pl.when(first)
        def _():
            for cp in comm.plan(c_in, c_out, ssem, rsem, True):
                cp.start()

        body(*own_in, *own_out, *scratch[:ns0])

        @pl.when(last)
        def _():
            sends, recvs = comm.plan(c_in, c_out, ssem, rsem, False)
            for cp in sends:
                cp.wait_send()
            for cp in recvs:
                cp.wait_recv()

    return carrying, kw, args + list(comm.ins)


def _run_comm(name, comm):
    ni, no = len(comm.ins), len(comm.outs)

    def body(*refs):
        for cp in comm.plan(refs[:ni], refs[ni:ni + no], refs[-2], refs[-1], True):
            cp.start()
        sends, recvs = comm.plan(refs[:ni], refs[ni:ni + no], refs[-2], refs[-1], False)
        for cp in sends:
            cp.wait_send()
        for cp in recvs:
            cp.wait_recv()

    return pl.pallas_call(
        body, name=name, in_specs=[ANY] * ni, out_specs=[ANY] * no, out_shape=comm.outs,
        input_output_aliases=comm.aliases,
        scratch_shapes=[pltpu.SemaphoreType.DMA((comm.nsem,)), pltpu.SemaphoreType.DMA((comm.nsem,))],
    )(*comm.ins)


def _mm(name, a, b, mode, out_dtype=F32, bias=None, tm=1024, tn=1024, tk=2048, comm=None, post=None, b_layer=None, stack=1):
    bshape = b.shape[1:] if b_layer is not None else b.shape
    if mode == "nn":
        (M, K), N = a.shape, bshape[1]
    elif mode == "nt":
        (M, K), N = a.shape, bshape[0]
    else:
        (K, M), N = a.shape, bshape[1] // stack
    tm, tn, tk = _pick(M, tm, LANE), _pick(N, tn, LANE), _pick(K, tk, LANE)
    nk, nj = K // tk, N // tn
    grid = (M // tm, stack * nj, nk)
    dn = {"nn": NN, "nt": NT, "tn": TN}[mode]
    a_spec = pl.BlockSpec((tk, tm), lambda i, j, k: (k, i)) if mode == "tn" else pl.BlockSpec((tm, tk), lambda i, j, k: (i, k))
    b_spec = pl.BlockSpec((tn, tk), lambda i, j, k: (j, k)) if mode == "nt" else pl.BlockSpec((tk, tn), lambda i, j, k: (k, j))
    if b_layer is not None:
        b_spec = pl.BlockSpec((1, tk, tn), lambda i, j, k: (b_layer, k, j))
    specs, args = [a_spec, b_spec], [a, b]
    if bias is not None:
        specs.append(pl.BlockSpec((1, tn), lambda i, j, k: (0, j)))
        args.append(bias)

    o_spec = pl.BlockSpec((tm, tn), lambda i, j, k: (i, j))
    if stack > 1:
        o_spec = pl.BlockSpec((1, tm, tn), lambda i, j, k: (j // nj, i, j % nj))
    fn, extra, out_dtypes = post if post is not None else (lambda t: (t,), [], [out_dtype])
    specs += [o_spec] * len(extra)
    args += list(extra)
    nin, nout = len(args), len(out_dtypes)

    def body(*refs):
        a_ref, b_ref = refs[0], refs[1]
        outs, acc = refs[nin:nin + nout], refs[-1]
        k = pl.program_id(2)

        @pl.when(k == 0)
        def _():
            acc[...] = jnp.zeros_like(acc)

        b_tile = b_ref[...] if b_layer is None else b_ref[0]
        acc[...] += lax.dot_general(a_ref[...].astype(MXU_DTYPE), b_tile.astype(MXU_DTYPE), (dn, ((), ())),
                                    preferred_element_type=F32)

        @pl.when(k == nk - 1)
        def _():
            res = acc[...]
            if bias is not None:
                res = res + refs[2][...]
            for o_ref, val in zip(outs, fn(res, *[r[...] for r in refs[nin - len(extra):nin]])):
                o_ref[...] = val.astype(o_ref.dtype).reshape(o_ref.shape)

    def first_last():
        at = [pl.program_id(d) for d in range(3)]
        first = jnp.logical_and(jnp.logical_and(at[0] == 0, at[1] == 0), at[2] == 0)
        last = jnp.logical_and(jnp.logical_and(at[0] == grid[0] - 1, at[1] == grid[1] - 1), at[2] == grid[2] - 1)
        return first, last

    oshape = (M, N) if stack == 1 else (stack, M, N)
    kw = dict(in_specs=specs, out_specs=[o_spec] * nout, out_shape=[jax.ShapeDtypeStruct(oshape, dt) for dt in out_dtypes],
              scratch_shapes=[pltpu.VMEM((tm, tn), F32)])
    body, kw, args = _attach(comm, body, kw, args, first_last)
    sem = ("arbitrary",) * 3 if comm is not None else ("parallel", "parallel", "arbitrary")
    res = pl.pallas_call(body, name=name, grid=grid, compiler_params=_params(sem), **kw)(*args)
    main = res[0] if nout == 1 else res[:nout]
    return main if comm is None else (main, res[nout:])


def _halo(K):
    return SUB * -(-(K - 1) // SUB)


def _conv_fwd(name, x, col0, w, bias, *, ts, tpe):
    T = x.shape[0]
    K, C = w.shape
    H = _halo(K)
    cb = _pick(C, 256, LANE)
    rb = _pick(ts, 64, SUB)
    off = col0 // cb
    specs = [pl.BlockSpec((ts, cb), lambda j, i: (i, off + j)),
             pl.BlockSpec((H, cb), lambda j, i: (jnp.maximum(i * (ts // H) - 1, 0), off + j)),
             pl.BlockSpec((K, cb), lambda j, i: (0, j))]
    args = [x, x, w]
    if bias is not None:
        specs.append(pl.BlockSpec((1, cb), lambda j, i: (0, j)))
        args.append(bias)

    def body(*refs):
        cur, halo, w_ref = refs[:3]
        o_ref, xp = refs[-2], refs[-1]
        i = pl.program_id(1)
        xp[0:H, :] = jnp.where(i % tpe == 0, 0.0, halo[...].astype(F32))
        xp[H:H + ts, :] = cur[...].astype(F32)
        for r0 in range(0, ts, rb):
            acc = jnp.zeros((rb, cb), F32) if bias is None else jnp.zeros((rb, cb), F32) + refs[3][...]
            for j in range(K):
                lo = H - (K - 1) + j + r0
                acc = acc + w_ref[j:j + 1, :] * xp[lo:lo + rb, :]
            o_ref[r0:r0 + rb, :] = acc

    return pl.pallas_call(
        body, name=name, grid=(C // cb, T // ts), in_specs=specs,
        out_specs=pl.BlockSpec((ts, cb), lambda j, i: (i, j)),
        out_shape=jax.ShapeDtypeStruct((T, C), F32),
        scratch_shapes=[pltpu.VMEM((H + ts, cb), F32)],
        compiler_params=_params(("parallel", "arbitrary")),
    )(*args)


def _conv_bwd(name, x, col0, dy, w, out_dtype, *, ts, tpe, comm=None):
    T = x.shape[0]
    K, C = w.shape
    H = _halo(K)
    cb = _pick(C, 256, LANE)
    rb = _pick(ts, 64, SUB)
    off = col0 // cb
    nt = T // ts

    def body(cur, halo, dyc, dyn, w_ref, dx_ref, dw_ref, db_ref, xp, dyp):
        i = pl.program_id(1)
        xp[0:H, :] = jnp.where(i % tpe == 0, 0.0, halo[...].astype(F32))
        xp[H:H + ts, :] = cur[...].astype(F32)
        dyp[0:ts, :] = dyc[...]
        dyp[ts:ts + H, :] = jnp.where(i % tpe == tpe - 1, 0.0, dyn[...])
        for r0 in range(0, ts, rb):
            acc = jnp.zeros((rb, cb), F32)
            for j in range(K):
                lo = K - 1 - j + r0
                acc = acc + w_ref[j:j + 1, :] * dyp[lo:lo + rb, :]
            dx_ref[r0:r0 + rb, :] = acc.astype(out_dtype)

        @pl.when(i == 0)
        def _():
            dw_ref[...] = jnp.zeros_like(dw_ref)
            db_ref[...] = jnp.zeros_like(db_ref)

        for j in range(K):
            part = jnp.zeros((1, cb), F32)
            for r0 in range(0, ts, rb):
                lo = H - (K - 1) + j + r0
                part = part + jnp.sum(dyp[r0:r0 + rb, :] * xp[lo:lo + rb, :], axis=0, keepdims=True)
            dw_ref[j:j + 1, :] += part
        db_ref[...] += jnp.sum(dyc[...], axis=0, keepdims=True)

    nc = C // cb
    kw = dict(in_specs=[pl.BlockSpec((ts, cb), lambda j, i: (i, off + j)),
                        pl.BlockSpec((H, cb), lambda j, i: (jnp.maximum(i * (ts // H) - 1, 0), off + j)),
                        pl.BlockSpec((ts, cb), lambda j, i: (i, j)),
                        pl.BlockSpec((H, cb), lambda j, i: (jnp.minimum((i + 1) * (ts // H), T // H - 1), j)),
                        pl.BlockSpec((K, cb), lambda j, i: (0, j))],
              out_specs=[pl.BlockSpec((ts, cb), lambda j, i: (i, j)),
                         pl.BlockSpec((K, cb), lambda j, i: (0, j)),
                         pl.BlockSpec((1, cb), lambda j, i: (0, j))],
              out_shape=[jax.ShapeDtypeStruct((T, C), out_dtype), jax.ShapeDtypeStruct((K, C), F32),
                         jax.ShapeDtypeStruct((1, C), F32)],
              scratch_shapes=[pltpu.VMEM((H + ts, cb), F32), pltpu.VMEM((ts + H, cb), F32)])

    def first_last():
        j, i = pl.program_id(0), pl.program_id(1)
        return jnp.logical_and(j == 0, i == 0), jnp.logical_and(j == nc - 1, i == nt - 1)

    body, kw, args = _attach(comm, body, kw, [x, x, dy, dy, w], first_last)
    res = pl.pallas_call(body, name=name, grid=(nc, nt), compiler_params=_params(("arbitrary", "arbitrary")), **kw)(*args)
    return res if comm is None else (res[:3], res[3:])


def _hdot(a, b, dn):
    return lax.dot_general(a, b, (dn, ((), ())), precision=HI, preferred_element_type=F32)


def _bdot(a, b, dn):
    return lax.dot_general(a.astype(MXU_DTYPE), b.astype(MXU_DTYPE), (dn, ((), ())), preferred_element_type=F32)


def _split(a):
    hi = a.astype(MXU_DTYPE)
    return hi, (a - hi.astype(F32)).astype(MXU_DTYPE)


def _dot3_raw(a, b, dn):
    if MXU_DTYPE == F32:
        return _hdot(a, b, dn)
    (ah, al), (bh, bl) = _split(a), _split(b)
    d = lambda p, q: lax.dot_general(p, q, (dn, ((), ())), preferred_element_type=F32)
    return d(ah, bh) + (d(ah, bl) + d(al, bh))


def _with_vjp(raw):
    dn = {"nn": NN, "nt": NT, "tn": TN}

    @functools.partial(jax.custom_vjp, nondiff_argnums=(2,))
    def dot(a, b, mode):
        return raw(a, b, dn[mode])

    def fwd(a, b, mode):
        return raw(a, b, dn[mode]), (a, b)

    def bwd(mode, res, ct):
        a, b = res
        if mode == "nn":
            return raw(ct, b, NT), raw(a, ct, TN)
        if mode == "nt":
            return raw(ct, b, NN), raw(ct, a, TN)
        return raw(b, ct, NT), raw(a, ct, NN)

    dot.defvjp(fwd, bwd)
    return dot


_dot_exact = _with_vjp(_hdot)
_dot3 = _with_vjp(_dot3_raw)
_dot1 = _with_vjp(_bdot)


@jax.custom_vjp
def _tri_inv(ns):
    C = ns[0].shape[0]
    eye = (lax.broadcasted_iota(jnp.int32, (C, C), 0) == lax.broadcasted_iota(jnp.int32, (C, C), 1)).astype(F32)
    ts = [eye + n for n in ns]
    ps = list(ns)
    for _ in range(int(math.log2(CHUNK)) - 1):
        ps = [_dot3_raw(p, p, NN) for p in ps]
        ts = [t + _dot3_raw(t, p, NN) for t, p in zip(ts, ps)]
    return tuple(ts)


def _tri_inv_fwd(ns):
    ts = _tri_inv(ns)
    return ts, ts


def _tri_inv_bwd(ts, cts):
    xs = [_dot3_raw(t, ct, TN) for t, ct in zip(ts, cts)]
    return (tuple(_dot3_raw(x, t, NT) for x, t in zip(xs, ts)),)


_tri_inv.defvjp(_tri_inv_fwd, _tri_inv_bwd)


def _dn_prep(qs, ks, vs, ba, alog, dtb, *, nh):
    C2, Dk = qs[0].shape
    z = ba + dtb
    g_all = -jnp.exp(alog) * (jnp.maximum(z, 0.0) + jnp.log(1.0 + jnp.exp(-jnp.abs(z))))
    ri = lax.broadcasted_iota(jnp.int32, (C2, C2), 0)
    cj = lax.broadcasted_iota(jnp.int32, (C2, C2), 1)
    same = jnp.logical_not(jnp.logical_xor(ri >= CHUNK, cj >= CHUNK))
    causal, strict = jnp.logical_and(ri >= cj, same), jnp.logical_and(ri > cj, same)
    gc_all = _dot_exact(causal.astype(F32), g_all, "nn")
    gc_rows = _dot_exact(g_all, jnp.logical_and(ri <= cj, same).astype(F32), "tn")
    gl_all = _dot_exact(same.astype(F32), g_all, "nn")
    lane = lax.broadcasted_iota(jnp.int32, (1, ba.shape[1]), 1)
    subl = lax.broadcasted_iota(jnp.int32, (ba.shape[1], 1), 0)
    heads = range(nh)
    sel = [(lane == nh + h).astype(F32) for h in heads]
    gc = [jnp.sum(gc_all * s, axis=1, keepdims=True) for s in sel]
    gl = [jnp.sum(gl_all * s, axis=1, keepdims=True) for s in sel]
    gcr = [jnp.sum(gc_rows * (subl == nh + h).astype(F32), axis=0, keepdims=True) for h in heads]
    decay = [jnp.where(causal, jnp.exp(jnp.where(causal, a - b, 0.0)), 0.0) for a, b in zip(gc, gcr)]
    beta = [_sigmoid(jnp.sum(ba * (lane == h).astype(F32), axis=1, keepdims=True)) for h in heads]
    q = [_silu(t) for t in qs]
    q = [t * lax.rsqrt(jnp.sum(t * t, axis=-1, keepdims=True) + EPS) * (Dk ** -0.5) for t in q]
    k = [_silu(t) for t in ks]
    k = [t * lax.rsqrt(jnp.sum(t * t, axis=-1, keepdims=True) + EPS) for t in k]
    kb = [a * b for a, b in zip(k, beta)]
    vb = [_silu(a) * b for a, b in zip(vs, beta)]
    kk = [_dot1(a, b, "nt") for a, b in zip(kb, k)]
    qk = [_dot1(a, b, "nt") for a, b in zip(q, k)]
    t = _tri_inv(tuple(-jnp.where(strict, a * d, 0.0) for a, d in zip(kk, decay)))
    egc = [jnp.exp(a) for a in gc]
    u = [_dot3(a, b, "nn") for a, b in zip(t, vb)]
    w = [_dot3(a, b * e, "nn") for a, b, e in zip(t, kb, egc)]
    attn = [jnp.where(causal, a * d, 0.0) for a, d in zip(qk, decay)]
    qd = [a * e for a, e in zip(q, egc)]
    kd = [a * jnp.exp(b - c) for a, b, c in zip(k, gl, gc)]
    glb = [jnp.exp(a) * jnp.ones((1, Dk), F32) for a in gl]
    return tuple(u), tuple(w), tuple(qd), tuple(kd), tuple(attn), tuple(glb)


def _dn_prep_specs(DN, col_q):
    qs = [pl.BlockSpec((PAIR, DN), functools.partial(lambda i, j: (i, j), j=j)) for j in range(3)]
    return qs + [pl.BlockSpec((PAIR, LANE), lambda i: (i, col_q)), pl.BlockSpec((1, LANE), lambda i: (0, 0)),
                 pl.BlockSpec((1, LANE), lambda i: (0, 0))]


def _heads(ref, NH, Dk):
    return tuple(ref[:, h * Dk:(h + 1) * Dk] for h in range(NH))


def _dn_prep_fwd(name, qkv, proj, ba_col, alog, dtb, *, NH, comm=None):
    T = qkv.shape[0]
    DN = qkv.shape[1] // 3
    Dk = DN // NH
    nsteps = T // PAIR

    def body(q_ref, k_ref, v_ref, ba_ref, al_ref, dt_ref, u_ref, w_ref, qd_ref, kd_ref, gl_ref, at_ref):
        res = _dn_prep(_heads(q_ref, NH, Dk), _heads(k_ref, NH, Dk), _heads(v_ref, NH, Dk), ba_ref[...], al_ref[...],
                       dt_ref[...], nh=NH)
        for h in range(NH):
            sl = slice(h * Dk, (h + 1) * Dk)
            u_ref[:, sl], w_ref[:, sl], qd_ref[:, sl], kd_ref[:, sl] = res[0][h], res[1][h], res[2][h], res[3][h]
            at_ref[h] = res[4][h]
            gl_ref[:, sl] = res[5][h]

    big = pl.BlockSpec((PAIR, DN), lambda i: (i, 0))
    kw = dict(in_specs=_dn_prep_specs(DN, ba_col // LANE),
              out_specs=[big] * 5 + [pl.BlockSpec((NH, PAIR, PAIR), lambda i: (0, i, 0))],
              out_shape=[jax.ShapeDtypeStruct((T, DN), F32)] * 5 + [jax.ShapeDtypeStruct((NH, T, PAIR), F32)],
              scratch_shapes=[])
    body, kw, args = _attach(comm, body, kw, [qkv, qkv, qkv, proj, alog, dtb],
                             lambda: (pl.program_id(0) == 0, pl.program_id(0) == nsteps - 1))
    res = pl.pallas_call(body, name=name, grid=(nsteps,), compiler_params=_params(("arbitrary",)), **kw)(*args)
    return res[:6], res[6:]


def _dn_prep_bwd(name, qkv, proj, ba_col, alog, dtb, cts, out_dtype, *, NH, comm=None):
    T = qkv.shape[0]
    DN = qkv.shape[1] // 3
    Dk = DN // NH
    nsteps = T // PAIR

    def body(q_ref, k_ref, v_ref, ba_ref, al_ref, dt_ref, du, dw, dqd, dkd, dgl, dat,
             dq_ref, dk_ref, dv_ref, dba_ref, dal_ref, ddt_ref):
        i = pl.program_id(0)
        _, pull = jax.vjp(functools.partial(_dn_prep, nh=NH), _heads(q_ref, NH, Dk), _heads(k_ref, NH, Dk),
                          _heads(v_ref, NH, Dk), ba_ref[...], al_ref[...], dt_ref[...])
        gq, gk, gv, gba, gal, gdt = pull((_heads(du, NH, Dk), _heads(dw, NH, Dk), _heads(dqd, NH, Dk),
                                          _heads(dkd, NH, Dk), tuple(dat[h] for h in range(NH)), _heads(dgl, NH, Dk)))
        for h in range(NH):
            sl = slice(h * Dk, (h + 1) * Dk)
            dq_ref[:, sl], dk_ref[:, sl], dv_ref[:, sl] = gq[h].astype(out_dtype), gk[h].astype(out_dtype), gv[h].astype(out_dtype)
        dba_ref[...] = gba.astype(out_dtype)

        @pl.when(i == 0)
        def _():
            dal_ref[...] = jnp.zeros_like(dal_ref)
            ddt_ref[...] = jnp.zeros_like(ddt_ref)

        dal_ref[...] += gal
        ddt_ref[...] += gdt

    big = pl.BlockSpec((PAIR, DN), lambda i: (i, 0))
    row = pl.BlockSpec((1, LANE), lambda i: (0, 0))
    kw = dict(in_specs=_dn_prep_specs(DN, ba_col // LANE) + [big] * 5 + [pl.BlockSpec((NH, PAIR, PAIR), lambda i: (0, i, 0))],
              out_specs=[big] * 3 + [pl.BlockSpec((PAIR, LANE), lambda i: (i, 0)), row, row],
              out_shape=[jax.ShapeDtypeStruct((T, DN), out_dtype)] * 3 + [jax.ShapeDtypeStruct((T, LANE), out_dtype),
                                                                           jax.ShapeDtypeStruct((1, LANE), F32),
                                                                           jax.ShapeDtypeStruct((1, LANE), F32)],
              scratch_shapes=[])
    body, kw, args = _attach(comm, body, kw, [qkv, qkv, qkv, proj, alog, dtb, *cts],
                             lambda: (pl.program_id(0) == 0, pl.program_id(0) == nsteps - 1))
    res = pl.pallas_call(body, name=name, grid=(nsteps,), compiler_params=_params(("arbitrary",)), **kw)(*args)
    return res[:6], res[6:]


def _dn_scan_fwd(name, u, w, qd, kd, gl, attn, *, NH, B):
    T, DN = u.shape
    Dk = DN // NH
    C = CHUNK
    S = T // B
    NP = S // PAIR

    def body(u_ref, w_ref, qd_ref, kd_ref, gl_ref, at_ref, o_ref, st_ref, s_ref):
        @pl.when(pl.program_id(0) == 0)
        def _():
            s_ref[...] = jnp.zeros_like(s_ref)

        zeros = jnp.zeros((C, Dk), F32)
        for sub in range(2):
            rs = slice(sub * C, (sub + 1) * C)
            for h in range(NH):
                sl = slice(h * Dk, (h + 1) * Dk)
                for b in range(B):
                    s = s_ref[b, h]
                    st_ref[b, sub, h] = s
                    vnew = u_ref[b, rs, sl] - _bdot(w_ref[b, rs, sl], s, NN)
                    vext = jnp.concatenate([vnew, zeros] if sub == 0 else [zeros, vnew], axis=0)
                    o_ref[b, rs, sl] = _bdot(qd_ref[b, rs, sl], s, NN) + _bdot(at_ref[h, b, rs, :], vext, NN)
                    s_ref[b, h] = s * gl_ref[b, sub * C:sub * C + 1, sl] + _bdot(kd_ref[b, rs, sl], vnew, TN)

    big = pl.BlockSpec((B, PAIR, DN), lambda n: (0, n, 0))
    o, st = pl.pallas_call(
        body, name=name, grid=(NP,),
        in_specs=[big] * 5 + [pl.BlockSpec((NH, B, PAIR, PAIR), lambda n: (0, 0, n, 0))],
        out_specs=[big, pl.BlockSpec((B, 2, NH, Dk, Dk), lambda n: (0, n, 0, 0, 0))],
        out_shape=[jax.ShapeDtypeStruct((B, S, DN), F32), jax.ShapeDtypeStruct((B, S // C, NH, Dk, Dk), F32)],
        scratch_shapes=[pltpu.VMEM((B, NH, Dk, Dk), F32)],
        compiler_params=_params(("arbitrary",)),
    )(*[t.reshape(B, S, DN) for t in (u, w, qd, kd, gl)], attn.reshape(NH, B, S, PAIR))
    return o.reshape(T, DN), st.reshape(T // C, NH, Dk, Dk)


def _dn_scan_bwd(name, do, u, w, qd, kd, gl, attn, st, *, NH, B, comm=None):
    T, DN = u.shape
    Dk = DN // NH
    C = CHUNK
    S = T // B
    NP = S // PAIR

    def body(do_ref, u_ref, w_ref, qd_ref, kd_ref, gl_ref, at_ref, st_ref,
             du_ref, dw_ref, dqd_ref, dkd_ref, dgl_ref, dat_ref, ds_ref):
        @pl.when(pl.program_id(0) == 0)
        def _():
            ds_ref[...] = jnp.zeros_like(ds_ref)

        row0 = lax.broadcasted_iota(jnp.int32, (C, Dk), 0) == 0
        zeros = jnp.zeros((C, Dk), F32)
        for sub in (1, 0):
            rs = slice(sub * C, (sub + 1) * C)
            for h in range(NH):
                sl = slice(h * Dk, (h + 1) * Dk)
                for b in range(B):
                    s, ds, g = st_ref[b, sub, h], ds_ref[b, h], do_ref[b, rs, sl]
                    wv, at, kdv = w_ref[b, rs, sl], at_ref[h, b, rs, :], kd_ref[b, rs, sl]
                    vnew = u_ref[b, rs, sl] - _bdot(wv, s, NN)
                    vext = jnp.concatenate([vnew, zeros] if sub == 0 else [zeros, vnew], axis=0)
                    dvnew = _bdot(at, g, TN)[rs] + _bdot(kdv, ds, NN)
                    dat_ref[h, b, rs, :] = _bdot(g, vext, NT)
                    dqd_ref[b, rs, sl] = _bdot(g, s, NT)
                    dkd_ref[b, rs, sl] = _bdot(vnew, ds, NT)
                    dgl_ref[b, rs, sl] = jnp.where(row0, jnp.sum(s * ds, axis=0, keepdims=True), 0.0)
                    du_ref[b, rs, sl] = dvnew
                    dw_ref[b, rs, sl] = -_bdot(dvnew, s, NT)
                    ds_ref[b, h] = (_bdot(qd_ref[b, rs, sl], g, TN) + ds * gl_ref[b, sub * C:sub * C + 1, sl]
                                    - _bdot(wv, dvnew, TN))

    big = pl.BlockSpec((B, PAIR, DN), lambda n: (0, NP - 1 - n, 0))
    att = pl.BlockSpec((NH, B, PAIR, PAIR), lambda n: (0, 0, NP - 1 - n, 0))
    kw = dict(in_specs=[big] * 6 + [att, pl.BlockSpec((B, 2, NH, Dk, Dk), lambda n: (0, NP - 1 - n, 0, 0, 0))],
              out_specs=[big] * 5 + [att],
              out_shape=[jax.ShapeDtypeStruct((B, S, DN), F32)] * 5 + [jax.ShapeDtypeStruct((NH, B, S, PAIR), F32)],
              scratch_shapes=[pltpu.VMEM((B, NH, Dk, Dk), F32)])
    args = [t.reshape(B, S, DN) for t in (do, u, w, qd, kd, gl)] + [attn.reshape(NH, B, S, PAIR),
                                                                   st.reshape(B, S // C, NH, Dk, Dk)]
    body, kw, args = _attach(comm, body, kw, args, lambda: (pl.program_id(0) == 0, pl.program_id(0) == NP - 1))
    res = pl.pallas_call(body, name=name, grid=(NP,), compiler_params=_params(("arbitrary",)), **kw)(*args)
    return [t.reshape(T, DN) for t in res[:5]] + [res[5].reshape(NH, T, PAIR)], res[6:]


def _sum0(name, a):
    n, r, ln = a.shape
    tr = _pick(r, 64, SUB)

    def body(a_ref, o_ref):
        acc = a_ref[0]
        for k in range(1, n):
            acc = acc + a_ref[k]
        o_ref[...] = acc

    return pl.pallas_call(
        body, name=name, grid=(r // tr,),
        in_specs=[pl.BlockSpec((n, tr, ln), lambda i: (0, i, 0))],
        out_specs=pl.BlockSpec((tr, ln), lambda i: (i, 0)),
        out_shape=jax.ShapeDtypeStruct((r, ln), F32),
        compiler_params=_params(("arbitrary",)),
    )(a)


def _adamw(name, w, g, m, v):
    Lw, R, Cc = w.shape
    tr = _pick(R, max(SUB, (1 << 18) // Cc // SUB * SUB), SUB)
    c1 = 1.0 - ADAM_B1 ** ADAM_STEP
    c2 = 1.0 - ADAM_B2 ** ADAM_STEP

    def body(w_ref, g_ref, m_ref, v_ref, d_ref, mo_ref, vo_ref):
        gv = g_ref[...]
        mn = ADAM_B1 * m_ref[...] + (1.0 - ADAM_B1) * gv
        vn = ADAM_B2 * v_ref[...] + (1.0 - ADAM_B2) * (gv * gv)
        mo_ref[...] = mn
        vo_ref[...] = vn
        d_ref[...] = -ADAM_LR * ((mn / c1) / (jnp.sqrt(vn / c2) + ADAM_EPS) + ADAM_WD * w_ref[...])

    spec = pl.BlockSpec((1, tr, Cc), lambda l, i: (l, i, 0))
    return pl.pallas_call(
        body, name=name, grid=(Lw, R // tr), in_specs=[spec] * 4, out_specs=[spec] * 3,
        out_shape=[jax.ShapeDtypeStruct((Lw, R, Cc), F32)] * 3,
        compiler_params=_params(("arbitrary", "arbitrary")),
    )(w, g, m, v)


def _pack(arrs):
    flat = jnp.concatenate([a.reshape(-1).astype(F32) for a in arrs])
    pad = (-flat.shape[0]) % (SUB * PACK_LANES)
    return jnp.pad(flat, (0, pad)).reshape(-1, PACK_LANES)


def _unpack(flat, shapes):
    out, pos = [], 0
    for shp in shapes:
        n = math.prod(shp)
        out.append(flat[..., pos:pos + n].reshape(flat.shape[:-1] + tuple(shp)))
        pos += n
    return out


def _remote(src, dst, ssem, rsem, dev):
    return pltpu.make_async_remote_copy(src_ref=src, dst_ref=dst, send_sem=ssem, recv_sem=rsem, device_id=dev,
                                        device_id_type=MESH)


def _place():
    return lax.axis_index("x"), lax.axis_index("y"), lax.axis_index("c")


def _all_gather8(name, a):
    m, n = a.shape

    def body(x_ref, out_ref, send_sems, recv_sems, local_sem):
        x, y, c = _place()
        me, sibling = (x, y, c), (x, y, 1 - c)
        chips = [(1 - x, y), (x, 1 - y), (1 - x, 1 - y)]

        def rows(px, py, pc):
            return out_ref.at[pl.ds((4 * px + 2 * py + pc) * m, m), :]

        def copy(k, block, to, src=None):
            return _remote(rows(*block) if src is None else src, rows(*block), send_sems.at[k], recv_sems.at[k], to)

        mine = pltpu.make_async_copy(x_ref, rows(*me), local_sem)
        mine.start()
        first = [copy(0, me, sibling, src=x_ref)]
        first += [copy(1 + j, me, (*chip, c), src=x_ref) for j, chip in enumerate(chips)]
        for cp in first:
            cp.start()
        passed = [copy(4 + j, (*chip, c), sibling) for j, chip in enumerate(chips)]
        for j, chip in enumerate(chips):
            copy(1 + j, (*chip, c), me).wait_recv()
            passed[j].start()
        copy(0, sibling, me).wait_recv()
        for j, chip in enumerate(chips):
            copy(4 + j, (*chip, 1 - c), me).wait_recv()
        for cp in first + passed:
            cp.wait_send()
        mine.wait()

    return pl.pallas_call(
        body, name=name,
        out_shape=jax.ShapeDtypeStruct((8 * m, n), a.dtype),
        in_specs=[pl.BlockSpec(memory_space=pltpu.VMEM)],
        out_specs=pl.BlockSpec(memory_space=pltpu.VMEM),
        scratch_shapes=[pltpu.SemaphoreType.DMA((7,)), pltpu.SemaphoreType.DMA((7,)), pltpu.SemaphoreType.DMA],
        compiler_params=pltpu.CompilerParams(vmem_limit_bytes=VMEM_LIMIT),
    )(a)


def _chip_peers(x, y):
    return [(1 - x, y), (x, 1 - y), (1 - x, 1 - y)]


def _sds(a):
    return jax.ShapeDtypeStruct(a.shape, a.dtype)


def _cast_into_slot(name, idx, a, layer, dtype):
    _, R, Cc = a.shape
    tr = _pick(R, 256, 16)

    def body(i_ref, a_ref, o_ref):
        o_ref[0] = a_ref[0].astype(dtype)

    return pl.pallas_call(
        body, name=name,
        grid_spec=pltpu.PrefetchScalarGridSpec(
            num_scalar_prefetch=1, grid=(R // tr,),
            in_specs=[pl.BlockSpec((1, tr, Cc), lambda i, ix: (layer, i, 0))],
            out_specs=pl.BlockSpec((1, tr, Cc), lambda i, ix: (ix[0], i, 0))),
        out_shape=jax.ShapeDtypeStruct((4, R, Cc), dtype),
        compiler_params=_params(("arbitrary",)),
    )(idx, a)


def _gather_ici(bufs):
    n = len(bufs)

    def plan(ins, outs, ssem, rsem, starting):
        x, y, c = _place()
        sends, recvs = [], []
        for a in range(n):
            for k, (px, py) in enumerate(_chip_peers(x, y)):
                mine, got = outs[a].at[2 * x + y, c], outs[a].at[2 * px + py, c]
                sends.append(_remote(mine, mine, ssem.at[3 * a + k], rsem.at[3 * a + k], (px, py, c)))
                if not starting:
                    recvs.append(_remote(got, got, ssem.at[3 * a + k], rsem.at[3 * a + k], (px, py, c)))
        return sends if starting else (sends, recvs)

    return _Comm(list(bufs), [_sds(b) for b in bufs], {a: a for a in range(n)}, 3 * n, plan)


def _gather_d2d(bufs):
    n = len(bufs)

    def plan(ins, outs, ssem, rsem, starting):
        x, y, c = _place()
        sends, recvs = [], []
        for a in range(n):
            for k, (px, py) in enumerate(_chip_peers(x, y)):
                got, other = outs[a].at[2 * px + py, c], outs[a].at[2 * px + py, 1 - c]
                sends.append(_remote(got, got, ssem.at[3 * a + k], rsem.at[3 * a + k], (x, y, 1 - c)))
                if not starting:
                    recvs.append(_remote(other, other, ssem.at[3 * a + k], rsem.at[3 * a + k], (x, y, 1 - c)))
        return sends if starting else (sends, recvs)

    return _Comm(list(bufs), [_sds(b) for b in bufs], {a: a for a in range(n)}, 3 * n, plan)


def _swap_halves(grads):
    n = len(grads)

    def plan(ins, outs, ssem, rsem, starting):
        x, y, c = _place()
        cps = [_remote(ins[a].at[:, 1 - c], outs[a], ssem.at[a], rsem.at[a], (x, y, 1 - c)) for a in range(n)]
        return cps if starting else (cps, cps)

    return _Comm(list(grads), [jax.ShapeDtypeStruct((4,) + g.shape[2:], g.dtype) for g in grads], {}, n, plan)


def _scatter_chips(halves):
    n = len(halves)

    def plan(ins, outs, ssem, rsem, starting):
        x, y, c = _place()
        cps = []
        for a in range(n):
            for k, (px, py) in enumerate(_chip_peers(x, y)):
                cps.append(_remote(ins[a].at[2 * px + py], outs[a].at[k], ssem.at[3 * a + k], rsem.at[3 * a + k], (px, py, c)))
        return cps if starting else (cps, cps)

    return _Comm(list(halves), [jax.ShapeDtypeStruct((3,) + h.shape[1:], h.dtype) for h in halves], {}, 3 * n, plan)


def _join_halves(parts, layer):
    n = len(parts)

    def plan(ins, outs, ssem, rsem, starting):
        x, y, c = _place()
        sends = [_remote(outs[a].at[layer, c], outs[a].at[layer, c], ssem.at[a], rsem.at[a], (x, y, 1 - c)) for a in range(n)]
        if starting:
            return sends
        return sends, [_remote(outs[a].at[layer, 1 - c], outs[a].at[layer, 1 - c], ssem.at[a], rsem.at[a], (x, y, 1 - c))
                       for a in range(n)]

    return _Comm(list(parts), [_sds(p) for p in parts], {a: a for a in range(n)}, n, plan)


def _add_half(name, idx, g, r):
    _, _, Rh, Cc = g.shape
    tr = _pick(Rh, 512, 16)

    def body(i_ref, g_ref, r_ref, o_ref):
        o_ref[...] = (g_ref[0].astype(F32) + r_ref[...].astype(F32)).astype(o_ref.dtype)

    return pl.pallas_call(
        body, name=name,
        grid_spec=pltpu.PrefetchScalarGridSpec(
            num_scalar_prefetch=1, grid=(4, Rh // tr),
            in_specs=[pl.BlockSpec((1, 1, tr, Cc), lambda s, i, ix: (s, ix[1], i, 0)),
                      pl.BlockSpec((1, tr, Cc), lambda s, i, ix: (s, i, 0))],
            out_specs=pl.BlockSpec((1, tr, Cc), lambda s, i, ix: (s, i, 0))),
        out_shape=jax.ShapeDtypeStruct((4, Rh, Cc), g.dtype),
        compiler_params=_params(("arbitrary", "arbitrary")),
    )(idx, g, r)


def _add_chips(name, idx, h, r, layer, nlayers, acc):
    _, Rh, Cc = h.shape
    tr = _pick(Rh, 256, 16)

    def body(i_ref, h_ref, r0, r1, r2, *rest):
        rest[-1][0, 0] = ((h_ref[0].astype(F32) + r0[0].astype(F32)) + r1[0].astype(F32)) + r2[0].astype(F32)

    taken = [] if acc is None else [acc]
    return pl.pallas_call(
        body, name=name,
        grid_spec=pltpu.PrefetchScalarGridSpec(
            num_scalar_prefetch=1, grid=(Rh // tr,),
            in_specs=[pl.BlockSpec((1, tr, Cc), lambda i, ix: (ix[0], i, 0))]
            + [pl.BlockSpec((1, tr, Cc), functools.partial(lambda i, ix, k: (k, i, 0), k=k)) for k in range(3)]
            + [ANY] * len(taken),
            out_specs=pl.BlockSpec((1, 1, tr, Cc), lambda i, ix: (layer, ix[1], i, 0))),
        out_shape=jax.ShapeDtypeStruct((nlayers, 2, Rh, Cc), F32),
        input_output_aliases={5: 0} if taken else {},
        compiler_params=_params(("arbitrary",)),
    )(idx, h, r, r, r, *taken)


def kernel(x, c, w_ada, b_ada, norm1_w, w_in, conv_dw_w, conv_dw_b, conv_ln_w, conv_ln_b, w_pw2, conv_out_norm_w, qkv_conv_w, a_log, dt_bias, dn_norm_w, w_out, norm2_w, w_up, w_down, final_ada_w, final_ada_b, final_norm_w, loss_target, m_w_ada, m_b_ada, m_norm1_w, m_w_in, m_conv_dw_w, m_conv_dw_b, m_conv_ln_w, m_conv_ln_b, m_w_pw2, m_conv_out_norm_w, m_qkv_conv_w, m_a_log, m_dt_bias, m_dn_norm_w, m_w_out, m_norm2_w, m_w_up, m_w_down, m_final_ada_w, m_final_ada_b, m_final_norm_w, v_w_ada, v_b_ada, v_norm1_w, v_w_in, v_conv_dw_w, v_conv_dw_b, v_conv_ln_w, v_conv_ln_b, v_w_pw2, v_conv_out_norm_w, v_qkv_conv_w, v_a_log, v_dt_bias, v_dn_norm_w, v_w_out, v_norm2_w, v_w_up, v_w_down, v_final_ada_w, v_final_ada_b, v_final_norm_w):
    names = ["w_ada", "b_ada", "norm1_w", "w_in", "conv_dw_w", "conv_dw_b", "conv_ln_w", "conv_ln_b", "w_pw2",
             "conv_out_norm_w", "qkv_conv_w", "a_log", "dt_bias", "dn_norm_w", "w_out", "norm2_w", "w_up", "w_down",
             "final_ada_w", "final_ada_b", "final_norm_w"]
    weights = dict(zip(names, [w_ada, b_ada, norm1_w, w_in, conv_dw_w, conv_dw_b, conv_ln_w, conv_ln_b, w_pw2,
                               conv_out_norm_w, qkv_conv_w, a_log, dt_bias, dn_norm_w, w_out, norm2_w, w_up, w_down,
                               final_ada_w, final_ada_b, final_norm_w]))
    mom1 = dict(zip(names, [m_w_ada, m_b_ada, m_norm1_w, m_w_in, m_conv_dw_w, m_conv_dw_b, m_conv_ln_w, m_conv_ln_b,
                            m_w_pw2, m_conv_out_norm_w, m_qkv_conv_w, m_a_log, m_dt_bias, m_dn_norm_w, m_w_out,
                            m_norm2_w, m_w_up, m_w_down, m_final_ada_w, m_final_ada_b, m_final_norm_w]))
    mom2 = dict(zip(names, [v_w_ada, v_b_ada, v_norm1_w, v_w_in, v_conv_dw_w, v_conv_dw_b, v_conv_ln_w, v_conv_ln_b,
                            v_w_pw2, v_conv_out_norm_w, v_qkv_conv_w, v_a_log, v_dt_bias, v_dn_norm_w, v_w_out,
                            v_norm2_w, v_w_up, v_w_down, v_final_ada_w, v_final_ada_b, v_final_norm_w]))

    B, S, D = x.shape
    T = B * S
    L = w_in.shape[0]
    C1 = conv_ln_w.shape[-1]
    NH, DH = a_log.shape[-1], dn_norm_w.shape[-1]
    DN = NH * DH
    FF = w_down.shape[1] * 4
    IN = w_in.shape[-1] * 4
    INP = 6 * C1 + LANE
    KC, KQ = conv_dw_w.shape[1], qkv_conv_w.shape[1]
    NW, NF = w_ada.shape[-1], final_ada_w.shape[-1]
    assert DH == LANE and DN == C1 and IN == 6 * C1 + 2 * NH and S % PAIR == 0
    xi, yi, ci = _place()
    s_me, me = 2 * xi + yi, 4 * xi + 2 * yi + ci
    idx = jnp.stack([s_me, ci]).astype(jnp.int32)
    tmf, tmb = _pick(S, 256, SUB), _pick(S, 128, SUB)
    ts, tsq = _pick(S, 256, _halo(KC)), _pick(S, 512, _halo(KQ))

    shapes1 = [(B, D), conv_dw_w.shape, qkv_conv_w.shape]
    g1 = _pack([c, conv_dw_w, qkv_conv_w])
    g1 = _all_gather8("gather_cond", g1).reshape(8, -1)
    c_all, cw_all, qw_all = _unpack(g1, shapes1)
    c_all = c_all.reshape(8 * B, D)
    conv_w_full = jnp.moveaxis(cw_all[0::2], 0, 2).reshape(L, KC, C1)
    qkv_w_full = jnp.moveaxis(qw_all[0::2], 0, 2).reshape(L, KQ, 3 * DN)
    (c_act,) = _rows_fwd("cond_silu", _silu_row, [_whole(c_all)], [], [], [(D, F32)], tm=8 * B)

    mods = []
    for l in range(L):
        bsh = lax.dynamic_slice(b_ada[l], (s_me * NW,), (NW,)).reshape(1, NW)
        mods.append(_mm(f"mod_{l}", c_act, w_ada, "nn", bias=bsh, b_layer=l))
    bsh = lax.dynamic_slice(final_ada_b, (s_me * NF,), (NF,)).reshape(1, NF)
    mods.append(_mm("mod_final", c_act, final_ada_w, "nn", bias=bsh))
    shapes2 = [(8 * B, NW)] * L + [(8 * B, NF)]
    g2 = _all_gather8("gather_mod", _pack(mods)).reshape(8, -1)[0::2]
    mod_all = [jnp.moveaxis(t, 0, 1).reshape(8 * B, -1) for t in _unpack(g2, shapes2)]
    mod_me = [lax.dynamic_slice(t, (B * me, 0), (B, t.shape[1])) for t in mod_all]

    def split_mod(t, n):
        return [t[:, k * D:(k + 1) * D].reshape(B, 1, D) for k in range(n)]

    def cast_weights(l):
        bufs = [_cast_into_slot(f"cast_{l}_{k}", idx, a, l, MXU_DTYPE) for k, a in enumerate([w_in, w_pw2, w_out, w_up, w_down])]
        return [b.reshape(4, 2, b.shape[1] // 2, b.shape[2]) for b in bufs]

    def natural(got):
        gi, gp, go, gu, gd = got
        wi = jnp.pad(jnp.moveaxis(gi.reshape(4, D, IN // 4), 0, 1).reshape(D, IN), ((0, 0), (0, INP - IN)))
        return dict(w_in=wi, w_pw2=gp.reshape(C1, C1), w_out=go.reshape(D, D),
                    w_up=jnp.moveaxis(gu.reshape(4, D, FF // 4), 0, 1).reshape(D, FF), w_down=gd.reshape(FF, D))

    first = _run_comm("gather_w_ici_0", _gather_ici(cast_weights(0)))
    wfull = {0: natural(_run_comm("gather_w_d2d_0", _gather_d2d(first)))}

    pad_row = lambda v: jnp.pad(v.reshape(1, -1), ((0, 0), (NH, LANE - 2 * NH)))
    row = lambda v: v.reshape(1, -1)

    saved = []
    xcur = x.reshape(T, D)
    ycur = jnp.zeros((T, D), F32)
    gate_prev = jnp.zeros((B, 1, D), F32)
    for l in range(L):
        sh1, sc1, g1_, sh2, sc2, g2_ = split_mod(mod_me[l], 6)
        sv = dict(x_in=xcur, y_in=ycur, gate_in=gate_prev, sh1=sh1, sc1=sc1, g1=g1_, sh2=sh2, sc2=sc2, g2=g2_)
        junction1 = functools.partial(_rows_fwd, f"junction1_{l}", _junction, [_whole(xcur), _whole(ycur)],
                                      [gate_prev, sh1, sc1], [row(norm1_w[l])], [(D, F32), (D, MXU_DTYPE)],
                                      tm=tmf, tpe=S // tmf)
        if l == 0:
            x0, h1 = junction1()
        else:
            (x0, h1), got = junction1(comm=_gather_d2d(travelling))
            wfull[l] = natural(got)
        W = wfull[l]
        nxt = cast_weights(l + 1) if l + 1 < L else None
        if nxt is None:
            proj = _mm(f"proj_{l}", h1, W["w_in"], "nn", tn=896)
        else:
            proj, got_in = _mm(f"proj_{l}", h1, W["w_in"], "nn", tn=896, comm=_gather_ici(nxt[:1]))
        (u0,) = _rows_fwd(f"glu_{l}", _glu, [(proj, C1, 0), (proj, C1, 1)], [], [], [(C1, F32)], tm=tmf)
        u1 = _conv_fwd(f"conv_{l}", u0, 0, conv_w_full[l], row(conv_dw_b[l]), ts=ts, tpe=S // ts)
        (u2,) = _rows_fwd(f"ln_silu_{l}", _ln_silu, [_whole(u1)], [], [row(conv_ln_w[l]), row(conv_ln_b[l])],
                          [(C1, MXU_DTYPE)], tm=tmf)
        u3 = _mm(f"pw2_{l}", u2, W["w_pw2"], "nn")
        (y_conv,) = _rows_fwd(f"conv_out_norm_{l}", _rms, [_whole(u3)], [], [row(conv_out_norm_w[l])],
                              [(C1, MXU_DTYPE)], tm=tmf)
        qkv = _conv_fwd(f"qkv_conv_{l}", proj, 2 * C1, qkv_w_full[l], None, ts=tsq, tpe=S // tsq)
        al, dtb = pad_row(a_log[l]), pad_row(dt_bias[l])
        (u, w, qd, kd, gl, attn), got_po = _dn_prep_fwd(f"dn_prep_{l}", qkv, proj, 6 * C1, al, dtb, NH=NH,
                                                        comm=None if nxt is None else _gather_ici(nxt[1:3]))
        o, st = _dn_scan_fwd(f"dn_scan_{l}", u, w, qd, kd, gl, attn, NH=NH, B=B)
        (y_dn,) = _rows_fwd(f"dn_out_norm_{l}", _gated_rms, [_whole(o), (proj, DN, 5)], [], [row(dn_norm_w[l])],
                            [(DN, MXU_DTYPE)], tm=tmf, ngroups=NH)
        ycat = jnp.concatenate([y_conv, y_dn], axis=1)
        y = _mm(f"out_{l}", ycat, W["w_out"], "nn")
        x1, h2 = _rows_fwd(f"junction2_{l}", _junction, [_whole(x0), _whole(y)], [g1_, sh2, sc2],
                           [row(norm2_w[l])], [(D, F32), (D, MXU_DTYPE)], tm=tmf, tpe=S // tmf)
        relu2 = (lambda t: (t, _relu2(t)[0]), [], [F32, MXU_DTYPE])
        if nxt is not None:
            (up, act), got_up = _mm(f"up_{l}", h2, W["w_up"], "nn", post=relu2, comm=_gather_ici(nxt[3:4]))
            mlp, got_down = _mm(f"down_{l}", act, W["w_down"], "nn", comm=_gather_ici(nxt[4:]))
            travelling = list(got_in) + list(got_po) + list(got_up) + list(got_down)
        else:
            up, act = _mm(f"up_{l}", h2, W["w_up"], "nn", post=relu2)
            mlp = _mm(f"down_{l}", act, W["w_down"], "nn")
        sv.update(x0=x0, h1=h1, proj=proj, u0=u0, u1=u1, u2=u2, u3=u3, qkv=qkv, al=al, dtb=dtb, u=u, w=w, qd=qd, kd=kd,
                  gl=gl, attn=attn, o=o, st=st, ycat=ycat, y=y, x1=x1, h2=h2, up=up, act=act)
        saved.append(sv)
        xcur, ycur, gate_prev = x1, mlp, g2_

    shf, scf = split_mod(mod_me[L], 2)
    tgt = loss_target.reshape(T, D)
    rowloss, dx, dy, dgate, dshf, dscf, dfinal_norm = _rows_vjp(
        "loss_head", _final_loss, [_whole(xcur), _whole(ycur), _whole(tgt)], [gate_prev, shf, scf], [row(final_norm_w)],
        None, [F32, MXU_DTYPE, None], tm=tmb, tpe=S // tmb, primal=[(1, F32)])
    loss = lax.psum(jnp.sum(rowloss), ("x", "y", "c"))

    dmods = [None] * L
    small = [None] * L
    stacked = [None] * 5
    pending = None
    for l in reversed(range(L)):
        W, sv = wfull[l], saved[l]
        d_relu2 = (lambda t, u_: (t * (2.0 * jnp.maximum(u_, 0.0)),), [sv["up"]], [MXU_DTYPE])
        if pending is None:
            dup = _mm(f"d_down_x_{l}", dy, W["w_down"], "nt", post=d_relu2)
            gw_down = _mm(f"d_down_w_{l}", sv["act"], dy, "tn", COMM_DTYPE)
            dh2 = _mm(f"d_up_x_{l}", dup, W["w_up"], "nt")
            gw_up = _mm(f"d_up_w_{l}", sv["h2"], dup, "tn", COMM_DTYPE)
        else:
            dup, got = _mm(f"d_down_x_{l}", dy, W["w_down"], "nt", post=d_relu2, comm=_swap_halves(pending))
            halves = [_add_half(f"rs_add_half_{l + 1}_{a}", idx, g, r) for a, (g, r) in enumerate(zip(pending, got))]
            gw_down, got_down = _mm(f"d_down_w_{l}", sv["act"], dy, "tn", COMM_DTYPE, comm=_scatter_chips(halves[4:]))
            dh2, got_in = _mm(f"d_up_x_{l}", dup, W["w_up"], "nt", comm=_scatter_chips(halves[:1]))
            gw_up, got_po = _mm(f"d_up_w_{l}", sv["h2"], dup, "tn", COMM_DTYPE, comm=_scatter_chips(halves[1:3]))
        shard4 = lambda g: g.reshape(4, 2, g.shape[1] // 2, g.shape[2])
        d_junction2 = functools.partial(
            _rows_vjp, f"d_junction2_{l}", _junction, [_whole(sv["x0"]), _whole(sv["y"])], [sv["g1"], sv["sh2"], sv["sc2"]],
            [row(norm2_w[l])], [_whole(dx), _whole(dh2)], [F32, MXU_DTYPE], tm=tmb, tpe=S // tmb)
        early = None
        if l > 0:
            dx0, dyo, dg1, dsh2, dsc2, dn2 = d_junction2()
        else:
            early = [shard4(jnp.moveaxis(gw_up.reshape(D, 4, FF // 4), 1, 0)), shard4(gw_down.reshape(4, FF // 4, D))]
            (dx0, dyo, dg1, dsh2, dsc2, dn2), got = d_junction2(comm=_swap_halves(early))
            halves_e = [_add_half(f"rs_add_half_0_{3 + a}", idx, g, r) for a, (g, r) in enumerate(zip(early, got))]
        dycat = _mm(f"d_out_x_{l}", dyo, W["w_out"], "nt")
        gw_out = _mm(f"d_out_w_{l}", sv["ycat"], dyo, "tn", COMM_DTYPE)
        proj = sv["proj"]
        do, dz, ddn = _rows_vjp(f"d_dn_out_norm_{l}", _gated_rms, [_whole(sv["o"]), (proj, DN, 5)], [],
                                [row(dn_norm_w[l])], [(dycat, DN, 1)], [F32, MXU_DTYPE], tm=tmb, ngroups=NH)
        cts, got_eu = _dn_scan_bwd(f"d_dn_scan_{l}", do, sv["u"], sv["w"], sv["qd"], sv["kd"], sv["gl"], sv["attn"], sv["st"],
                                   NH=NH, B=B, comm=None if early is None else _scatter_chips(halves_e[:1]))
        (dq, dk, dv, dba, dal, ddt), got_up = _dn_prep_bwd(
            f"d_dn_prep_{l}", sv["qkv"], proj, 6 * C1, sv["al"], sv["dtb"], list(cts), F32, NH=NH,
            comm=None if pending is None else _scatter_chips(halves[3:4]))
        if pending is not None:
            got = list(got_in) + list(got_po) + list(got_up) + list(got_down)
            parts = [_add_chips(f"rs_add_chips_{l + 1}_{a}", idx, h, r, l + 1, L, stacked[a])
                     for a, (h, r) in enumerate(zip(halves, got))]
        dqkv, gqw = [], []
        for k, dpart in enumerate((dq, dk, dv)):
            dxp, dwp, _ = _conv_bwd(f"d_qkv_conv_{l}_{k}", proj, (2 + k) * C1, dpart,
                                    qkv_w_full[l][:, k * DN:(k + 1) * DN], MXU_DTYPE, ts=tsq, tpe=S // tsq)
            dqkv.append(dxp)
            gqw.append(dwp)
        (du3, dcon) = _rows_vjp(f"d_conv_out_norm_{l}", _rms, [_whole(sv["u3"])], [], [row(conv_out_norm_w[l])],
                                [(dycat, C1, 0)], [MXU_DTYPE], tm=tmb)
        du2 = _mm(f"d_pw2_x_{l}", du3, W["w_pw2"], "nt")
        gw_pw2 = _mm(f"d_pw2_w_{l}", sv["u2"], du3, "tn", COMM_DTYPE)
        du1, dlnw, dlnb = _rows_vjp(f"d_ln_silu_{l}", _ln_silu, [_whole(sv["u1"])], [],
                                    [row(conv_ln_w[l]), row(conv_ln_b[l])], [_whole(du2)], [F32], tm=tmb)
        if early is None:
            du0, gcw, gcb = _conv_bwd(f"d_conv_{l}", sv["u0"], 0, du1, conv_w_full[l], F32, ts=ts, tpe=S // ts)
        else:
            (du0, gcw, gcb), got_ed = _conv_bwd(f"d_conv_{l}", sv["u0"], 0, du1, conv_w_full[l], F32, ts=ts, tpe=S // ts,
                                                comm=_scatter_chips(halves_e[1:]))
        dval, dgate_c = _rows_vjp(f"d_glu_{l}", _glu, [(proj, C1, 0), (proj, C1, 1)], [], [], [_whole(du0)],
                                  [MXU_DTYPE, MXU_DTYPE], tm=tmb)
        dproj = jnp.concatenate([dval, dgate_c] + dqkv + [dz, dba.astype(MXU_DTYPE)], axis=1)
        if pending is None:
            dh1 = _mm(f"d_proj_x_{l}", dproj, W["w_in"], "nt", tk=896)
        else:
            dh1, stacked = _mm(f"d_proj_x_{l}", dproj, W["w_in"], "nt", tk=896, comm=_join_halves(parts, l + 1))
        gw_in = _mm(f"d_proj_w_{l}", sv["h1"], dproj, "tn", COMM_DTYPE, tn=896)[:, :IN]
        d_junction1 = functools.partial(
            _rows_vjp, f"d_junction1_{l}", _junction, [_whole(sv["x_in"]), _whole(sv["y_in"])],
            [sv["gate_in"], sv["sh1"], sv["sc1"]], [row(norm1_w[l])], [_whole(dx0), _whole(dh1)], [F32, MXU_DTYPE],
            tm=tmb, tpe=S // tmb)
        if early is None:
            dxn, dyn, dg2p, dsh1, dsc1, dn1 = d_junction1()
        else:
            parts_e = [_add_chips(f"rs_add_chips_0_{3 + a}", idx, h, r, 0, L, stacked[3 + a])
                       for a, (h, r) in enumerate(zip(halves_e, list(got_eu) + list(got_ed)))]
            (dxn, dyn, dg2p, dsh1, dsc1, dn1), joined = d_junction1(comm=_join_halves(parts_e, 0))
            stacked = list(stacked[:3]) + list(joined)
        dmods[l] = [dsh1, dsc1, dg1, dsh2, dsc2, dgate]
        small[l] = dict(norm1_w=dn1, conv_dw_w=gcw, conv_dw_b=gcb, conv_ln_w=dlnw, conv_ln_b=dlnb, conv_out_norm_w=dcon,
                        qkv_conv_w=jnp.concatenate(gqw, axis=1), a_log=dal[:, NH:2 * NH], dt_bias=ddt[:, NH:2 * NH],
                        dn_norm_w=ddn, norm2_w=dn2)
        pending = [shard4(jnp.moveaxis(gw_in.reshape(D, 4, IN // 4), 1, 0)), shard4(gw_pw2.reshape(4, C1 // 4, C1)),
                   shard4(gw_out.reshape(4, D // 4, D))]
        if early is None:
            pending += [shard4(jnp.moveaxis(gw_up.reshape(D, 4, FF // 4), 1, 0)), shard4(gw_down.reshape(4, FF // 4, D))]
        dx, dy, dgate = dxn, dyn, dg2p

    got = _run_comm("rs_swap_0", _swap_halves(pending))
    halves = [_add_half(f"rs_add_half_0_{a}", idx, g, r) for a, (g, r) in enumerate(zip(pending, got))]
    got = _run_comm("rs_scatter_0", _scatter_chips(halves))
    parts = [_add_chips(f"rs_add_chips_0_{a}", idx, h, r, 0, L, stacked[a]) for a, (h, r) in enumerate(zip(halves, got))]
    stacked = list(_run_comm("rs_join_0", _join_halves(parts, 0))) + list(stacked[3:])

    grad_x = dx.reshape(B, S, D)

    small_names = ["norm1_w", "conv_dw_w", "conv_dw_b", "conv_ln_w", "conv_ln_b", "conv_out_norm_w", "qkv_conv_w", "a_log",
                   "dt_bias", "dn_norm_w", "norm2_w"]
    dmod_flat = jnp.concatenate([jnp.concatenate([t.reshape(B, D) for t in dmods[l]], axis=1) for l in range(L)]
                                + [dshf.reshape(B, D), dscf.reshape(B, D)], axis=1)
    small_list = [small[l][n] for l in range(L) for n in small_names] + [dfinal_norm]
    shapes3 = [dmod_flat.shape] + [t.shape for t in small_list]
    g3 = _all_gather8("gather_grads", _pack([dmod_flat] + small_list))
    rows3 = g3.shape[0] // 8
    g3 = g3.reshape(8, rows3, PACK_LANES)
    summed = _unpack(_sum0("sum_small_grads", g3).reshape(-1), shapes3)[1:]
    dmod_all = _unpack(g3.reshape(8, -1), shapes3[:1])[0].reshape(8 * B, -1)
    nm = dmod_all.shape[1]
    grad_b_all = _sum0("sum_mod_grads", dmod_all.reshape(8 * B, nm // PACK_LANES, PACK_LANES)).reshape(-1)
    grads = {}
    dm = jnp.concatenate([lax.dynamic_slice(dmod_all, (0, l * 6 * D + s_me * NW), (8 * B, NW)) for l in range(L)], axis=1)
    grads["w_ada"] = _mm("d_ada_w", c_act, dm, "tn", stack=L)
    grads["b_ada"] = grad_b_all[:L * 6 * D].reshape(L, 6 * D)
    dm = lax.dynamic_slice(dmod_all, (0, L * 6 * D + s_me * NF), (8 * B, NF))
    grads["final_ada_w"] = _mm("d_final_ada_w", c_act, dm, "tn")
    grads["final_ada_b"] = grad_b_all[L * 6 * D:]
    per_layer = {n: [] for n in small_names}
    for l in range(L):
        for k, n in enumerate(small_names):
            per_layer[n].append(summed[l * len(small_names) + k])
    for n in small_names:
        t = jnp.stack(per_layer[n])
        if n == "conv_dw_w":
            t = lax.dynamic_slice(t, (0, 0, s_me * (C1 // 4)), (L, KC, C1 // 4))
        elif n == "qkv_conv_w":
            t = lax.dynamic_slice(t, (0, 0, s_me * (3 * DN // 4)), (L, KQ, 3 * DN // 4))
        grads[n] = t.reshape(weights[n].shape)
    grads["final_norm_w"] = summed[-1].reshape(final_norm_w.shape)
    for k, n in enumerate(["w_in", "w_pw2", "w_out", "w_up", "w_down"]):
        grads[n] = stacked[k].reshape(weights[n].shape)

    delta, new_m, new_v = {}, {}, {}
    big_names = ["w_ada", "w_in", "w_pw2", "w_out", "w_up", "w_down", "final_ada_w"]
    for n in big_names:
        shp = weights[n].shape
        three = lambda t: t.reshape((-1,) + shp[-2:])
        d_, m_, v_ = _adamw(f"adamw_{n}", three(weights[n]), three(grads[n]), three(mom1[n]), three(mom2[n]))
        delta[n], new_m[n], new_v[n] = d_.reshape(shp), m_.reshape(shp), v_.reshape(shp)
    rest = [n for n in names if n not in big_names]
    rshapes = [weights[n].shape for n in rest]
    packed = [_pack([d[n] for n in rest])[None] for d in (weights, grads, mom1, mom2)]
    outs = _adamw("adamw_small", *packed)
    for dst, arr in zip((delta, new_m, new_v), outs):
        for n, t in zip(rest, _unpack(arr.reshape(-1), rshapes)):
            dst[n] = t

    return (loss, grad_x, *[grads[n] for n in names], *[delta[n] for n in names], *[new_m[n] for n in names],
            *[new_v[n] for n in names])
```

```python
import functools
import math
import typing

import jax
import jax.numpy as jnp
from jax import lax
from jax.experimental import pallas as pl
from jax.experimental.pallas import tpu as pltpu

F32 = jnp.float32
MXU_DTYPE = jnp.bfloat16
COMM_DTYPE = jnp.bfloat16
HI = lax.Precision.HIGHEST
CHUNK = 64
PAIR = 2 * CHUNK
EPS = 1e-6
LANE = 128
SUB = 8
PACK_LANES = 1024
VMEM_LIMIT = 56 * 1024 * 1024
ADAM_LR, ADAM_B1, ADAM_B2, ADAM_EPS, ADAM_WD, ADAM_STEP = 0.001, 0.9, 0.999, 1e-08, 0.01, 10
MESH = pl.DeviceIdType.MESH
ANY = pl.BlockSpec(memory_space=pl.ANY)
NN, NT, TN = ((1,), (0,)), ((1,), (1,)), ((0,), (0,))


def _pick(dim, pref, mult):
    for t in range(min(pref, dim), 0, -1):
        if dim % t == 0 and t % mult == 0:
            return t
    return dim


def _params(sem):
    return pltpu.CompilerParams(dimension_semantics=sem, vmem_limit_bytes=VMEM_LIMIT)


def _sigmoid(v):
    return 1.0 / (1.0 + jnp.exp(-v))


def _silu(v):
    return v * _sigmoid(v)


def _row_specs(rows, exps, gls, tm, tpe):
    specs = [pl.BlockSpec((tm, w), functools.partial(lambda i, j: (i, j), j=cb)) for _, w, cb in rows]
    specs += [pl.BlockSpec((1, 1, e.shape[-1]), lambda i: (i // tpe, 0, 0)) for e in exps]
    specs += [pl.BlockSpec((1, g.shape[-1]), lambda i: (0, 0)) for g in gls]
    return specs


def _rows_fwd(name, fn, rows, exps, gls, outs, *, tm, tpe=1, ngroups=1, comm=None):
    T = rows[0][0].shape[0]
    nr, ne, ng = len(rows), len(exps), len(gls)
    nsteps = T // tm

    def body(*refs):
        r, e, g, o = refs[:nr], refs[nr:nr + ne], refs[nr + ne:nr + ne + ng], refs[nr + ne + ng:]
        ev = [t[0].astype(F32) for t in e]
        gv = [t[...].astype(F32) for t in g]
        for k in range(ngroups):
            rv = [t[:, k * (w // ngroups):(k + 1) * (w // ngroups)].astype(F32) for t, (_, w, _) in zip(r, rows)]
            res = fn(*rv, *ev, *gv)
            for oref, val, (w, dt) in zip(o, res, outs):
                gw = w // ngroups
                oref[:, k * gw:(k + 1) * gw] = val.astype(dt)

    kw = dict(in_specs=_row_specs(rows, exps, gls, tm, tpe),
              out_specs=[pl.BlockSpec((tm, w), lambda i: (i, 0)) for w, _ in outs],
              out_shape=[jax.ShapeDtypeStruct((T, w), dt) for w, dt in outs], scratch_shapes=[])
    body, kw, args = _attach(comm, body, kw, [a for a, _, _ in rows] + list(exps) + list(gls),
                             lambda: (pl.program_id(0) == 0, pl.program_id(0) == nsteps - 1))
    res = pl.pallas_call(body, name=name, grid=(nsteps,), compiler_params=_params(("arbitrary",)), **kw)(*args)
    return res if comm is None else (res[:len(outs)], res[len(outs):])


def _rows_vjp(name, fn, rows, exps, gls, cts, row_dtypes, *, tm, tpe=1, ngroups=1, primal=None, comm=None):
    T = rows[0][0].shape[0]
    nr, ne, ng = len(rows), len(exps), len(gls)
    nc = 0 if cts is None else len(cts)
    keep = [k for k, dt in enumerate(row_dtypes) if dt is not None]
    npr = 0 if primal is None else len(primal)

    def body(*refs):
        r, e, g = refs[:nr], refs[nr:nr + ne], refs[nr + ne:nr + ne + ng]
        c = refs[nr + ne + ng:nr + ne + ng + nc]
        o = refs[nr + ne + ng + nc:]
        po, ro, eo, go = o[:npr], o[npr:npr + len(keep)], o[npr + len(keep):npr + len(keep) + ne], o[npr + len(keep) + ne:]
        i = pl.program_id(0)
        ev = [t[0].astype(F32) for t in e]
        gv = [t[...].astype(F32) for t in g]
        esum = [jnp.zeros_like(v) for v in ev]
        gsum = [jnp.zeros_like(v) for v in gv]
        for k in range(ngroups):
            rv = [t[:, k * (w // ngroups):(k + 1) * (w // ngroups)].astype(F32) for t, (_, w, _) in zip(r, rows)]
            res, pull = jax.vjp(fn, *rv, *ev, *gv)
            if cts is None:
                ct = tuple(jnp.ones_like(v) for v in res)
            else:
                ct = tuple(t[:, k * (w // ngroups):(k + 1) * (w // ngroups)].astype(F32) for t, (_, w, _) in zip(c, cts))
            grads = pull(ct)
            for oref, val, (w, dt) in zip(po, res, primal or ()):
                gw = w // ngroups
                oref[:, k * gw:(k + 1) * gw] = val.astype(dt)
            for oref, idx in zip(ro, keep):
                gw = rows[idx][1] // ngroups
                oref[:, k * gw:(k + 1) * gw] = grads[idx].astype(row_dtypes[idx])
            esum = [s + d for s, d in zip(esum, grads[nr:nr + ne])]
            gsum = [s + d for s, d in zip(gsum, grads[nr + ne:])]

        if ne:
            @pl.when(i % tpe == 0)
            def _():
                for oref in eo:
                    oref[...] = jnp.zeros_like(oref)
            for oref, s in zip(eo, esum):
                oref[0] += s
        if ng:
            @pl.when(i == 0)
            def _():
                for oref in go:
                    oref[...] = jnp.zeros_like(oref)
            for oref, s in zip(go, gsum):
                oref[...] += s

    out_specs = [pl.BlockSpec((tm, w), lambda i: (i, 0)) for w, _ in (primal or ())]
    out_shape = [jax.ShapeDtypeStruct((T, w), dt) for w, dt in (primal or ())]
    out_specs += [pl.BlockSpec((tm, rows[k][1]), lambda i: (i, 0)) for k in keep]
    out_shape += [jax.ShapeDtypeStruct((T, rows[k][1]), row_dtypes[k]) for k in keep]
    out_specs += [pl.BlockSpec((1, 1, e.shape[-1]), lambda i: (i // tpe, 0, 0)) for e in exps]
    out_shape += [jax.ShapeDtypeStruct(e.shape, F32) for e in exps]
    out_specs += [pl.BlockSpec((1, g.shape[-1]), lambda i: (0, 0)) for g in gls]
    out_shape += [jax.ShapeDtypeStruct(g.shape, F32) for g in gls]
    ct_specs = [] if cts is None else [pl.BlockSpec((tm, w), functools.partial(lambda i, j: (i, j), j=cb)) for _, w, cb in cts]
    ct_arrs = [] if cts is None else [a for a, _, _ in cts]
    nsteps, nown = T // tm, len(out_shape)
    kw = dict(in_specs=_row_specs(rows, exps, gls, tm, tpe) + ct_specs, out_specs=out_specs, out_shape=out_shape,
              scratch_shapes=[])
    body, kw, args = _attach(comm, body, kw, [a for a, _, _ in rows] + list(exps) + list(gls) + ct_arrs,
                             lambda: (pl.program_id(0) == 0, pl.program_id(0) == nsteps - 1))
    res = pl.pallas_call(body, name=name, grid=(nsteps,), compiler_params=_params(("arbitrary",)), **kw)(*args)
    return res if comm is None else (res[:nown], res[nown:])


def _whole(a):
    return (a, a.shape[-1], 0)


def _junction(x, y, gate, shift, scale, w):
    xn = x + gate * y
    r = lax.rsqrt(jnp.mean(xn * xn, axis=-1, keepdims=True) + EPS)
    return xn, (xn * r * w) * (1.0 + scale) + shift


def _final_loss(x, y, tgt, gate, shift, scale, w):
    _, out = _junction(x, y, gate, shift, scale, w)
    err = out - tgt
    return (0.5 * jnp.mean(err * err, axis=-1, keepdims=True),)


def _glu(val, gate):
    return (val * _sigmoid(gate),)


def _ln_silu(u, w, b):
    xc = u - jnp.mean(u, axis=-1, keepdims=True)
    y = xc * lax.rsqrt(jnp.mean(xc * xc, axis=-1, keepdims=True) + EPS) * w + b
    return (_silu(y),)


def _rms(u, w):
    return (u * lax.rsqrt(jnp.mean(u * u, axis=-1, keepdims=True) + EPS) * w,)


def _gated_rms(o, z, w):
    return (o * lax.rsqrt(jnp.mean(o * o, axis=-1, keepdims=True) + EPS) * w * _silu(z),)


def _relu2(u):
    r = jnp.maximum(u, 0.0)
    return (r * r,)


def _silu_row(u):
    return (_silu(u),)


class _Comm(typing.NamedTuple):
    ins: list
    outs: list
    aliases: dict
    nsem: int
    plan: typing.Callable


def _attach(comm, body, kw, args, first_last):
    if comm is None:
        return body, kw, args
    ni0, no0, ns0 = len(kw["in_specs"]), len(kw["out_specs"]), len(kw["scratch_shapes"])
    ni, no = len(comm.ins), len(comm.outs)
    kw = dict(kw, in_specs=kw["in_specs"] + [ANY] * ni, out_specs=kw["out_specs"] + [ANY] * no,
              out_shape=kw["out_shape"] + comm.outs,
              scratch_shapes=kw["scratch_shapes"] + [pltpu.SemaphoreType.DMA((comm.nsem,)), pltpu.SemaphoreType.DMA((comm.nsem,))],
              input_output_aliases={ni0 + k: no0 + v for k, v in comm.aliases.items()})

    def carrying(*refs):
        own_in, c_in = refs[:ni0], refs[ni0:ni0 + ni]
        own_out, c_out = refs[ni0 + ni:ni0 + ni + no0], refs[ni0 + ni + no0:ni0 + ni + no0 + no]
        scratch = refs[ni0 + ni + no0 + no:]
        ssem, rsem = scratch[ns0], scratch[ns0 + 1]
        first, last = first_last()

        @pl.when(first)
        def _():
            for cp in comm.plan(c_in, c_out, ssem, rsem, True):
                cp.start()

        body(*own_in, *own_out, *scratch[:ns0])

        @pl.when(last)
        def _():
            sends, recvs = comm.plan(c_in, c_out, ssem, rsem, False)
            for cp in sends:
                cp.wait_send()
            for cp in recvs:
                cp.wait_recv()

    return carrying, kw, args + list(comm.ins)


def _run_comm(name, comm):
    ni, no = len(comm.ins), len(comm.outs)

    def body(*refs):
        for cp in comm.plan(refs[:ni], refs[ni:ni + no], refs[-2], refs[-1], True):
            cp.start()
        sends, recvs = comm.plan(refs[:ni], refs[ni:ni + no], refs[-2], refs[-1], False)
        for cp in sends:
            cp.wait_send()
        for cp in recvs:
            cp.wait_recv()

    return pl.pallas_call(
        body, name=name, in_specs=[ANY] * ni, out_specs=[ANY] * no, out_shape=comm.outs,
        input_output_aliases=comm.aliases,
        scratch_shapes=[pltpu.SemaphoreType.DMA((comm.nsem,)), pltpu.SemaphoreType.DMA((comm.nsem,))],
    )(*comm.ins)


def _mm(name, a, b, mode, out_dtype=F32, bias=None, tm=1024, tn=1024, tk=2048, comm=None, post=None, b_layer=None, stack=1):
    bshape = b.shape[1:] if b_layer is not None else b.shape
    if mode == "nn":
        (M, K), N = a.shape, bshape[1]
    elif mode == "nt":
        (M, K), N = a.shape, bshape[0]
    else:
        (K, M), N = a.shape, bshape[1] // stack
    tm, tn, tk = _pick(M, tm, LANE), _pick(N, tn, LANE), _pick(K, tk, LANE)
    nk, nj = K // tk, N // tn
    grid = (M // tm, stack * nj, nk)
    dn = {"nn": NN, "nt": NT, "tn": TN}[mode]
    a_spec = pl.BlockSpec((tk, tm), lambda i, j, k: (k, i)) if mode == "tn" else pl.BlockSpec((tm, tk), lambda i, j, k: (i, k))
    b_spec = pl.BlockSpec((tn, tk), lambda i, j, k: (j, k)) if mode == "nt" else pl.BlockSpec((tk, tn), lambda i, j, k: (k, j))
    if b_layer is not None:
        b_spec = pl.BlockSpec((1, tk, tn), lambda i, j, k: (b_layer, k, j))
    specs, args = [a_spec, b_spec], [a, b]
    if bias is not None:
        specs.append(pl.BlockSpec((1, tn), lambda i, j, k: (0, j)))
        args.append(bias)

    o_spec = pl.BlockSpec((tm, tn), lambda i, j, k: (i, j))
    if stack > 1:
        o_spec = pl.BlockSpec((1, tm, tn), lambda i, j, k: (j // nj, i, j % nj))
    fn, extra, out_dtypes = post if post is not None else (lambda t: (t,), [], [out_dtype])
    specs += [o_spec] * len(extra)
    args += list(extra)
    nin, nout = len(args), len(out_dtypes)

    def body(*refs):
        a_ref, b_ref = refs[0], refs[1]
        outs, acc = refs[nin:nin + nout], refs[-1]
        k = pl.program_id(2)

        @pl.when(k == 0)
        def _():
            acc[...] = jnp.zeros_like(acc)

        b_tile = b_ref[...] if b_layer is None else b_ref[0]
        acc[...] += lax.dot_general(a_ref[...].astype(MXU_DTYPE), b_tile.astype(MXU_DTYPE), (dn, ((), ())),
                                    preferred_element_type=F32)

        @pl.when(k == nk - 1)
        def _():
            res = acc[...]
            if bias is not None:
                res = res + refs[2][...]
            for o_ref, val in zip(outs, fn(res, *[r[...] for r in refs[nin - len(extra):nin]])):
                o_ref[...] = val.astype(o_ref.dtype).reshape(o_ref.shape)

    def first_last():
        at = [pl.program_id(d) for d in range(3)]
        first = jnp.logical_and(jnp.logical_and(at[0] == 0, at[1] == 0), at[2] == 0)
        last = jnp.logical_and(jnp.logical_and(at[0] == grid[0] - 1, at[1] == grid[1] - 1), at[2] == grid[2] - 1)
        return first, last

    oshape = (M, N) if stack == 1 else (stack, M, N)
    kw = dict(in_specs=specs, out_specs=[o_spec] * nout, out_shape=[jax.ShapeDtypeStruct(oshape, dt) for dt in out_dtypes],
              scratch_shapes=[pltpu.VMEM((tm, tn), F32)])
    body, kw, args = _attach(comm, body, kw, args, first_last)
    sem = ("arbitrary",) * 3 if comm is not None else ("parallel", "parallel", "arbitrary")
    res = pl.pallas_call(body, name=name, grid=grid, compiler_params=_params(sem), **kw)(*args)
    main = res[0] if nout == 1 else res[:nout]
    return main if comm is None else (main, res[nout:])


def _halo(K):
    return SUB * -(-(K - 1) // SUB)


def _conv_cols(K, C, col0):
    return _pick(math.gcd(C, col0), 512 if K <= SUB else 256, LANE)


def _conv_fwd(name, x, col0, w, bias, *, ts, tpe, comm=None):
    T = x.shape[0]
    K, C = w.shape
    H = _halo(K)
    cb = _conv_cols(K, C, col0)
    rb = _pick(ts, 64, SUB)
    off = col0 // cb
    specs = [pl.BlockSpec((ts, cb), lambda j, i: (i, off + j)),
             pl.BlockSpec((H, cb), lambda j, i: (jnp.maximum(i * (ts // H) - 1, 0), off + j)),
             pl.BlockSpec((K, cb), lambda j, i: (0, j))]
    args = [x, x, w]
    if bias is not None:
        specs.append(pl.BlockSpec((1, cb), lambda j, i: (0, j)))
        args.append(bias)

    def body(*refs):
        cur, halo, w_ref = refs[:3]
        o_ref, xp = refs[-2], refs[-1]
        i = pl.program_id(1)
        xp[0:H, :] = jnp.where(i % tpe == 0, 0.0, halo[...].astype(F32))
        xp[H:H + ts, :] = cur[...].astype(F32)
        for r0 in range(0, ts, rb):
            acc = jnp.zeros((rb, cb), F32) if bias is None else jnp.zeros((rb, cb), F32) + refs[3][...]
            for j in range(K):
                lo = H - (K - 1) + j + r0
                acc = acc + w_ref[j:j + 1, :] * xp[lo:lo + rb, :]
            o_ref[r0:r0 + rb, :] = acc

    nc, nt = C // cb, T // ts
    kw = dict(in_specs=specs, out_specs=[pl.BlockSpec((ts, cb), lambda j, i: (i, j))],
              out_shape=[jax.ShapeDtypeStruct((T, C), F32)], scratch_shapes=[pltpu.VMEM((H + ts, cb), F32)])

    def first_last():
        j, i = pl.program_id(0), pl.program_id(1)
        return jnp.logical_and(j == 0, i == 0), jnp.logical_and(j == nc - 1, i == nt - 1)

    body, kw, args = _attach(comm, body, kw, args, first_last)
    res = pl.pallas_call(body, name=name, grid=(nc, nt), compiler_params=_params(("arbitrary", "arbitrary")), **kw)(*args)
    return res[0] if comm is None else (res[0], res[1:])


def _conv_bwd(name, x, col0, dy, w, out_dtype, *, ts, tpe, comm=None):
    T = x.shape[0]
    K, C = w.shape
    H = _halo(K)
    cb = _conv_cols(K, C, col0)
    rb = _pick(ts, 64, SUB)
    off = col0 // cb
    nt = T // ts

    def body(cur, halo, dyc, dyn, w_ref, dx_ref, dw_ref, db_ref, xp, dyp):
        i = pl.program_id(1)
        xp[0:H, :] = jnp.where(i % tpe == 0, 0.0, halo[...].astype(F32))
        xp[H:H + ts, :] = cur[...].astype(F32)
        dyp[0:ts, :] = dyc[...]
        dyp[ts:ts + H, :] = jnp.where(i % tpe == tpe - 1, 0.0, dyn[...])
        for r0 in range(0, ts, rb):
            acc = jnp.zeros((rb, cb), F32)
            for j in range(K):
                lo = K - 1 - j + r0
                acc = acc + w_ref[j:j + 1, :] * dyp[lo:lo + rb, :]
            dx_ref[r0:r0 + rb, :] = acc.astype(out_dtype)

        @pl.when(i == 0)
        def _():
            dw_ref[...] = jnp.zeros_like(dw_ref)
            db_ref[...] = jnp.zeros_like(db_ref)

        for j in range(K):
            part = jnp.zeros((1, cb), F32)
            for r0 in range(0, ts, rb):
                lo = H - (K - 1) + j + r0
                part = part + jnp.sum(dyp[r0:r0 + rb, :] * xp[lo:lo + rb, :], axis=0, keepdims=True)
            dw_ref[j:j + 1, :] += part
        db_ref[...] += jnp.sum(dyc[...], axis=0, keepdims=True)

    nc = C // cb
    kw = dict(in_specs=[pl.BlockSpec((ts, cb), lambda j, i: (i, off + j)),
                        pl.BlockSpec((H, cb), lambda j, i: (jnp.maximum(i * (ts // H) - 1, 0), off + j)),
                        pl.BlockSpec((ts, cb), lambda j, i: (i, j)),
                        pl.BlockSpec((H, cb), lambda j, i: (jnp.minimum((i + 1) * (ts // H), T // H - 1), j)),
                        pl.BlockSpec((K, cb), lambda j, i: (0, j))],
              out_specs=[pl.BlockSpec((ts, cb), lambda j, i: (i, j)),
                         pl.BlockSpec((K, cb), lambda j, i: (0, j)),
                         pl.BlockSpec((1, cb), lambda j, i: (0, j))],
              out_shape=[jax.ShapeDtypeStruct((T, C), out_dtype), jax.ShapeDtypeStruct((K, C), F32),
                         jax.ShapeDtypeStruct((1, C), F32)],
              scratch_shapes=[pltpu.VMEM((H + ts, cb), F32), pltpu.VMEM((ts + H, cb), F32)])

    def first_last():
        j, i = pl.program_id(0), pl.program_id(1)
        return jnp.logical_and(j == 0, i == 0), jnp.logical_and(j == nc - 1, i == nt - 1)

    body, kw, args = _attach(comm, body, kw, [x, x, dy, dy, w], first_last)
    res = pl.pallas_call(body, name=name, grid=(nc, nt), compiler_params=_params(("arbitrary", "arbitrary")), **kw)(*args)
    return res if comm is None else (res[:3], res[3:])


def _hdot(a, b, dn):
    return lax.dot_general(a, b, (dn, ((), ())), precision=HI, preferred_element_type=F32)


def _bdot(a, b, dn):
    return lax.dot_general(a.astype(MXU_DTYPE), b.astype(MXU_DTYPE), (dn, ((), ())), preferred_element_type=F32)


def _split(a):
    hi = a.astype(MXU_DTYPE)
    return hi, (a - hi.astype(F32)).astype(MXU_DTYPE)


def _dot3_raw(a, b, dn):
    if MXU_DTYPE == F32:
        return _hdot(a, b, dn)
    (ah, al), (bh, bl) = _split(a), _split(b)
    d = lambda p, q: lax.dot_general(p, q, (dn, ((), ())), preferred_element_type=F32)
    return d(ah, bh) + (d(ah, bl) + d(al, bh))


def _with_vjp(raw):
    dn = {"nn": NN, "nt": NT, "tn": TN}

    @functools.partial(jax.custom_vjp, nondiff_argnums=(2,))
    def dot(a, b, mode):
        return raw(a, b, dn[mode])

    def fwd(a, b, mode):
        return raw(a, b, dn[mode]), (a, b)

    def bwd(mode, res, ct):
        a, b = res
        if mode == "nn":
            return raw(ct, b, NT), raw(a, ct, TN)
        if mode == "nt":
            return raw(ct, b, NN), raw(ct, a, TN)
        return raw(b, ct, NT), raw(a, ct, NN)

    dot.defvjp(fwd, bwd)
    return dot


_dot_exact = _with_vjp(_hdot)
_dot3 = _with_vjp(_dot3_raw)
_dot1 = _with_vjp(_bdot)


@jax.custom_vjp
def _tri_inv(ns):
    C = ns[0].shape[0]
    eye = (lax.broadcasted_iota(jnp.int32, (C, C), 0) == lax.broadcasted_iota(jnp.int32, (C, C), 1)).astype(F32)
    ts = [eye + n for n in ns]
    ps = list(ns)
    for _ in range(int(math.log2(CHUNK)) - 1):
        ps = [_dot3_raw(p, p, NN) for p in ps]
        ts = [t + _dot3_raw(t, p, NN) for t, p in zip(ts, ps)]
    return tuple(ts)


def _tri_inv_fwd(ns):
    ts = _tri_inv(ns)
    return ts, ts


def _tri_inv_bwd(ts, cts):
    xs = [_dot3_raw(t, ct, TN) for t, ct in zip(ts, cts)]
    return (tuple(_dot3_raw(x, t, NT) for x, t in zip(xs, ts)),)


_tri_inv.defvjp(_tri_inv_fwd, _tri_inv_bwd)


def _dn_prep(qs, ks, vs, ba, alog, dtb, *, nh):
    C2, Dk = qs[0].shape
    z = ba + dtb
    g_all = -jnp.exp(alog) * (jnp.maximum(z, 0.0) + jnp.log(1.0 + jnp.exp(-jnp.abs(z))))
    ri = lax.broadcasted_iota(jnp.int32, (C2, C2), 0)
    cj = lax.broadcasted_iota(jnp.int32, (C2, C2), 1)
    same = jnp.logical_not(jnp.logical_xor(ri >= CHUNK, cj >= CHUNK))
    causal, strict = jnp.logical_and(ri >= cj, same), jnp.logical_and(ri > cj, same)
    gc_all = _dot_exact(causal.astype(F32), g_all, "nn")
    gc_rows = _dot_exact(g_all, jnp.logical_and(ri <= cj, same).astype(F32), "tn")
    gl_all = _dot_exact(same.astype(F32), g_all, "nn")
    lane = lax.broadcasted_iota(jnp.int32, (1, ba.shape[1]), 1)
    subl = lax.broadcasted_iota(jnp.int32, (ba.shape[1], 1), 0)
    heads = range(nh)
    sel = [(lane == nh + h).astype(F32) for h in heads]
    gc = [jnp.sum(gc_all * s, axis=1, keepdims=True) for s in sel]
    gl = [jnp.sum(gl_all * s, axis=1, keepdims=True) for s in sel]
    gcr = [jnp.sum(gc_rows * (subl == nh + h).astype(F32), axis=0, keepdims=True) for h in heads]
    decay = [jnp.where(causal, jnp.exp(jnp.where(causal, a - b, 0.0)), 0.0) for a, b in zip(gc, gcr)]
    beta = [_sigmoid(jnp.sum(ba * (lane == h).astype(F32), axis=1, keepdims=True)) for h in heads]
    q = [_silu(t) for t in qs]
    q = [t * lax.rsqrt(jnp.sum(t * t, axis=-1, keepdims=True) + EPS) * (Dk ** -0.5) for t in q]
    k = [_silu(t) for t in ks]
    k = [t * lax.rsqrt(jnp.sum(t * t, axis=-1, keepdims=True) + EPS) for t in k]
    kb = [a * b for a, b in zip(k, beta)]
    vb = [_silu(a) * b for a, b in zip(vs, beta)]
    kk = [_dot1(a, b, "nt") for a, b in zip(kb, k)]
    qk = [_dot1(a, b, "nt") for a, b in zip(q, k)]
    t = _tri_inv(tuple(-jnp.where(strict, a * d, 0.0) for a, d in zip(kk, decay)))
    egc = [jnp.exp(a) for a in gc]
    u = [_dot3(a, b, "nn") for a, b in zip(t, vb)]
    w = [_dot3(a, b * e, "nn") for a, b, e in zip(t, kb, egc)]
    attn = [jnp.where(causal, a * d, 0.0) for a, d in zip(qk, decay)]
    qd = [a * e for a, e in zip(q, egc)]
    kd = [a * jnp.exp(b - c) for a, b, c in zip(k, gl, gc)]
    glb = [jnp.exp(a) * jnp.ones((1, Dk), F32) for a in gl]
    return tuple(u), tuple(w), tuple(qd), tuple(kd), tuple(attn), tuple(glb)


def _dn_prep_specs(DN, col_q):
    qs = [pl.BlockSpec((PAIR, DN), functools.partial(lambda i, j: (i, j), j=j)) for j in range(3)]
    return qs + [pl.BlockSpec((PAIR, LANE), lambda i: (i, col_q)), pl.BlockSpec((1, LANE), lambda i: (0, 0)),
                 pl.BlockSpec((1, LANE), lambda i: (0, 0))]


def _heads(ref, NH, Dk):
    return tuple(ref[:, h * Dk:(h + 1) * Dk] for h in range(NH))


def _dn_prep_fwd(name, qkv, proj, ba_col, alog, dtb, *, NH, comm=None):
    T = qkv.shape[0]
    DN = qkv.shape[1] // 3
    Dk = DN // NH
    nsteps = T // PAIR

    def body(q_ref, k_ref, v_ref, ba_ref, al_ref, dt_ref, u_ref, w_ref, qd_ref, kd_ref, gl_ref, at_ref):
        res = _dn_prep(_heads(q_ref, NH, Dk), _heads(k_ref, NH, Dk), _heads(v_ref, NH, Dk), ba_ref[...], al_ref[...],
                       dt_ref[...], nh=NH)
        for h in range(NH):
            sl = slice(h * Dk, (h + 1) * Dk)
            u_ref[:, sl], w_ref[:, sl], qd_ref[:, sl], kd_ref[:, sl] = res[0][h], res[1][h], res[2][h], res[3][h]
            at_ref[h] = res[4][h]
            gl_ref[:, sl] = res[5][h]

    big = pl.BlockSpec((PAIR, DN), lambda i: (i, 0))
    kw = dict(in_specs=_dn_prep_specs(DN, ba_col // LANE),
              out_specs=[big] * 5 + [pl.BlockSpec((NH, PAIR, PAIR), lambda i: (0, i, 0))],
              out_shape=[jax.ShapeDtypeStruct((T, DN), F32)] * 5 + [jax.ShapeDtypeStruct((NH, T, PAIR), F32)],
              scratch_shapes=[])
    body, kw, args = _attach(comm, body, kw, [qkv, qkv, qkv, proj, alog, dtb],
                             lambda: (pl.program_id(0) == 0, pl.program_id(0) == nsteps - 1))
    res = pl.pallas_call(body, name=name, grid=(nsteps,), compiler_params=_params(("arbitrary",)), **kw)(*args)
    return res[:6], res[6:]


def _dn_prep_bwd(name, qkv, proj, ba_col, alog, dtb, cts, out_dtype, *, NH, comm=None):
    T = qkv.shape[0]
    DN = qkv.shape[1] // 3
    Dk = DN // NH
    nsteps = T // PAIR

    def body(q_ref, k_ref, v_ref, ba_ref, al_ref, dt_ref, du, dw, dqd, dkd, dgl, dat,
             dq_ref, dk_ref, dv_ref, dba_ref, dal_ref, ddt_ref):
        i = pl.program_id(0)
        _, pull = jax.vjp(functools.partial(_dn_prep, nh=NH), _heads(q_ref, NH, Dk), _heads(k_ref, NH, Dk),
                          _heads(v_ref, NH, Dk), ba_ref[...], al_ref[...], dt_ref[...])
        gq, gk, gv, gba, gal, gdt = pull((_heads(du, NH, Dk), _heads(dw, NH, Dk), _heads(dqd, NH, Dk),
                                          _heads(dkd, NH, Dk), tuple(dat[h] for h in range(NH)), _heads(dgl, NH, Dk)))
        for h in range(NH):
            sl = slice(h * Dk, (h + 1) * Dk)
            dq_ref[:, sl], dk_ref[:, sl], dv_ref[:, sl] = gq[h].astype(out_dtype), gk[h].astype(out_dtype), gv[h].astype(out_dtype)
        dba_ref[...] = gba.astype(out_dtype)

        @pl.when(i == 0)
        def _():
            dal_ref[...] = jnp.zeros_like(dal_ref)
            ddt_ref[...] = jnp.zeros_like(ddt_ref)

        dal_ref[...] += gal
        ddt_ref[...] += gdt

    big = pl.BlockSpec((PAIR, DN), lambda i: (i, 0))
    row = pl.BlockSpec((1, LANE), lambda i: (0, 0))
    kw = dict(in_specs=_dn_prep_specs(DN, ba_col // LANE) + [big] * 5 + [pl.BlockSpec((NH, PAIR, PAIR), lambda i: (0, i, 0))],
              out_specs=[big] * 3 + [pl.BlockSpec((PAIR, LANE), lambda i: (i, 0)), row, row],
              out_shape=[jax.ShapeDtypeStruct((T, DN), out_dtype)] * 3 + [jax.ShapeDtypeStruct((T, LANE), out_dtype),
                                                                           jax.ShapeDtypeStruct((1, LANE), F32),
                                                                           jax.ShapeDtypeStruct((1, LANE), F32)],
              scratch_shapes=[])
    body, kw, args = _attach(comm, body, kw, [qkv, qkv, qkv, proj, alog, dtb, *cts],
                             lambda: (pl.program_id(0) == 0, pl.program_id(0) == nsteps - 1))
    res = pl.pallas_call(body, name=name, grid=(nsteps,), compiler_params=_params(("arbitrary",)), **kw)(*args)
    return res[:6], res[6:]


def _dn_scan_fwd(name, u, w, qd, kd, gl, attn, *, NH, B):
    T, DN = u.shape
    Dk = DN // NH
    C = CHUNK
    S = T // B
    NP = S // PAIR

    def body(u_ref, w_ref, qd_ref, kd_ref, gl_ref, at_ref, o_ref, st_ref, s_ref):
        @pl.when(pl.program_id(0) == 0)
        def _():
            s_ref[...] = jnp.zeros_like(s_ref)

        zeros = jnp.zeros((C, Dk), F32)
        for sub in range(2):
            rs = slice(sub * C, (sub + 1) * C)
            for h in range(NH):
                sl = slice(h * Dk, (h + 1) * Dk)
                for b in range(B):
                    s = s_ref[b, h]
                    st_ref[b, sub, h] = s
                    vnew = u_ref[b, rs, sl] - _bdot(w_ref[b, rs, sl], s, NN)
                    vext = jnp.concatenate([vnew, zeros] if sub == 0 else [zeros, vnew], axis=0)
                    o_ref[b, rs, sl] = _bdot(qd_ref[b, rs, sl], s, NN) + _bdot(at_ref[h, b, rs, :], vext, NN)
                    s_ref[b, h] = s * gl_ref[b, sub * C:sub * C + 1, sl] + _bdot(kd_ref[b, rs, sl], vnew, TN)

    big = pl.BlockSpec((B, PAIR, DN), lambda n: (0, n, 0))
    o, st = pl.pallas_call(
        body, name=name, grid=(NP,),
        in_specs=[big] * 5 + [pl.BlockSpec((NH, B, PAIR, PAIR), lambda n: (0, 0, n, 0))],
        out_specs=[big, pl.BlockSpec((B, 2, NH, Dk, Dk), lambda n: (0, n, 0, 0, 0))],
        out_shape=[jax.ShapeDtypeStruct((B, S, DN), F32), jax.ShapeDtypeStruct((B, S // C, NH, Dk, Dk), F32)],
        scratch_shapes=[pltpu.VMEM((B, NH, Dk, Dk), F32)],
        compiler_params=_params(("arbitrary",)),
    )(*[t.reshape(B, S, DN) for t in (u, w, qd, kd, gl)], attn.reshape(NH, B, S, PAIR))
    return o.reshape(T, DN), st.reshape(T // C, NH, Dk, Dk)


def _dn_scan_bwd(name, do, u, w, qd, kd, gl, attn, st, *, NH, B, comm=None):
    T, DN = u.shape
    Dk = DN // NH
    C = CHUNK
    S = T // B
    NP = S // PAIR

    def body(do_ref, u_ref, w_ref, qd_ref, kd_ref, gl_ref, at_ref, st_ref,
             du_ref, dw_ref, dqd_ref, dkd_ref, dgl_ref, dat_ref, ds_ref):
        @pl.when(pl.program_id(0) == 0)
        def _():
            ds_ref[...] = jnp.zeros_like(ds_ref)

        row0 = lax.broadcasted_iota(jnp.int32, (C, Dk), 0) == 0
        zeros = jnp.zeros((C, Dk), F32)
        for sub in (1, 0):
            rs = slice(sub * C, (sub + 1) * C)
            for h in range(NH):
                sl = slice(h * Dk, (h + 1) * Dk)
                for b in range(B):
                    s, ds, g = st_ref[b, sub, h], ds_ref[b, h], do_ref[b, rs, sl]
                    wv, at, kdv = w_ref[b, rs, sl], at_ref[h, b, rs, :], kd_ref[b, rs, sl]
                    vnew = u_ref[b, rs, sl] - _bdot(wv, s, NN)
                    vext = jnp.concatenate([vnew, zeros] if sub == 0 else [zeros, vnew], axis=0)
                    dvnew = _bdot(at, g, TN)[rs] + _bdot(kdv, ds, NN)
                    dat_ref[h, b, rs, :] = _bdot(g, vext, NT)
                    dqd_ref[b, rs, sl] = _bdot(g, s, NT)
                    dkd_ref[b, rs, sl] = _bdot(vnew, ds, NT)
                    dgl_ref[b, rs, sl] = jnp.where(row0, jnp.sum(s * ds, axis=0, keepdims=True), 0.0)
                    du_ref[b, rs, sl] = dvnew
                    dw_ref[b, rs, sl] = -_bdot(dvnew, s, NT)
                    ds_ref[b, h] = (_bdot(qd_ref[b, rs, sl], g, TN) + ds * gl_ref[b, sub * C:sub * C + 1, sl]
                                    - _bdot(wv, dvnew, TN))

    big = pl.BlockSpec((B, PAIR, DN), lambda n: (0, NP - 1 - n, 0))
    att = pl.BlockSpec((NH, B, PAIR, PAIR), lambda n: (0, 0, NP - 1 - n, 0))
    kw = dict(in_specs=[big] * 6 + [att, pl.BlockSpec((B, 2, NH, Dk, Dk), lambda n: (0, NP - 1 - n, 0, 0, 0))],
              out_specs=[big] * 5 + [att],
              out_shape=[jax.ShapeDtypeStruct((B, S, DN), F32)] * 5 + [jax.ShapeDtypeStruct((NH, B, S, PAIR), F32)],
              scratch_shapes=[pltpu.VMEM((B, NH, Dk, Dk), F32)])
    args = [t.reshape(B, S, DN) for t in (do, u, w, qd, kd, gl)] + [attn.reshape(NH, B, S, PAIR),
                                                                   st.reshape(B, S // C, NH, Dk, Dk)]
    body, kw, args = _attach(comm, body, kw, args, lambda: (pl.program_id(0) == 0, pl.program_id(0) == NP - 1))
    res = pl.pallas_call(body, name=name, grid=(NP,), compiler_params=_params(("arbitrary",)), **kw)(*args)
    return [t.reshape(T, DN) for t in res[:5]] + [res[5].reshape(NH, T, PAIR)], res[6:]


def _sum0(name, a):
    n, r, ln = a.shape
    tr = _pick(r, 64, SUB)

    def body(a_ref, o_ref):
        acc = a_ref[0]
        for k in range(1, n):
            acc = acc + a_ref[k]
        o_ref[...] = acc

    return pl.pallas_call(
        body, name=name, grid=(r // tr,),
        in_specs=[pl.BlockSpec((n, tr, ln), lambda i: (0, i, 0))],
        out_specs=pl.BlockSpec((tr, ln), lambda i: (i, 0)),
        out_shape=jax.ShapeDtypeStruct((r, ln), F32),
        compiler_params=_params(("arbitrary",)),
    )(a)


def _adamw(name, w, g, m, v):
    Lw, R, Cc = w.shape
    tr = _pick(R, max(SUB, (1 << 18) // Cc // SUB * SUB), SUB)
    c1 = 1.0 - ADAM_B1 ** ADAM_STEP
    c2 = 1.0 - ADAM_B2 ** ADAM_STEP

    def body(w_ref, g_ref, m_ref, v_ref, d_ref, mo_ref, vo_ref):
        gv = g_ref[...]
        mn = ADAM_B1 * m_ref[...] + (1.0 - ADAM_B1) * gv
        vn = ADAM_B2 * v_ref[...] + (1.0 - ADAM_B2) * (gv * gv)
        mo_ref[...] = mn
        vo_ref[...] = vn
        d_ref[...] = -ADAM_LR * ((mn / c1) / (jnp.sqrt(vn / c2) + ADAM_EPS) + ADAM_WD * w_ref[...])

    spec = pl.BlockSpec((1, tr, Cc), lambda l, i: (l, i, 0))
    return pl.pallas_call(
        body, name=name, grid=(Lw, R // tr), in_specs=[spec] * 4, out_specs=[spec] * 3,
        out_shape=[jax.ShapeDtypeStruct((Lw, R, Cc), F32)] * 3,
        compiler_params=_params(("arbitrary", "arbitrary")),
    )(w, g, m, v)


def _pack(arrs):
    flat = jnp.concatenate([a.reshape(-1).astype(F32) for a in arrs])
    pad = (-flat.shape[0]) % (SUB * PACK_LANES)
    return jnp.pad(flat, (0, pad)).reshape(-1, PACK_LANES)


def _unpack(flat, shapes):
    out, pos = [], 0
    for shp in shapes:
        n = math.prod(shp)
        out.append(flat[..., pos:pos + n].reshape(flat.shape[:-1] + tuple(shp)))
        pos += n
    return out


def _remote(src, dst, ssem, rsem, dev):
    return pltpu.make_async_remote_copy(src_ref=src, dst_ref=dst, send_sem=ssem, recv_sem=rsem, device_id=dev,
                                        device_id_type=MESH)


def _place():
    return lax.axis_index("x"), lax.axis_index("y"), lax.axis_index("c")


def _all_gather8(name, a):
    m, n = a.shape

    def body(x_ref, out_ref, send_sems, recv_sems, local_sem):
        x, y, c = _place()
        me, sibling = (x, y, c), (x, y, 1 - c)
        chips = [(1 - x, y), (x, 1 - y), (1 - x, 1 - y)]

        def rows(px, py, pc):
            return out_ref.at[pl.ds((4 * px + 2 * py + pc) * m, m), :]

        def copy(k, block, to, src=None):
            return _remote(rows(*block) if src is None else src, rows(*block), send_sems.at[k], recv_sems.at[k], to)

        mine = pltpu.make_async_copy(x_ref, rows(*me), local_sem)
        mine.start()
        first = [copy(0, me, sibling, src=x_ref)]
        first += [copy(1 + j, me, (*chip, c), src=x_ref) for j, chip in enumerate(chips)]
        for cp in first:
            cp.start()
        passed = [copy(4 + j, (*chip, c), sibling) for j, chip in enumerate(chips)]
        for j, chip in enumerate(chips):
            copy(1 + j, (*chip, c), me).wait_recv()
            passed[j].start()
        copy(0, sibling, me).wait_recv()
        for j, chip in enumerate(chips):
            copy(4 + j, (*chip, 1 - c), me).wait_recv()
        for cp in first + passed:
            cp.wait_send()
        mine.wait()

    return pl.pallas_call(
        body, name=name,
        out_shape=jax.ShapeDtypeStruct((8 * m, n), a.dtype),
        in_specs=[pl.BlockSpec(memory_space=pltpu.VMEM)],
        out_specs=pl.BlockSpec(memory_space=pltpu.VMEM),
        scratch_shapes=[pltpu.SemaphoreType.DMA((7,)), pltpu.SemaphoreType.DMA((7,)), pltpu.SemaphoreType.DMA],
        compiler_params=pltpu.CompilerParams(vmem_limit_bytes=VMEM_LIMIT),
    )(a)


def _chip_peers(x, y):
    return [(1 - x, y), (x, 1 - y), (1 - x, 1 - y)]


def _sds(a):
    return jax.ShapeDtypeStruct(a.shape, a.dtype)


def _cast_into_slot(name, idx, a, layer, dtype):
    _, R, Cc = a.shape
    tr = _pick(R, 256, 16)

    def body(i_ref, a_ref, o_ref):
        o_ref[0] = a_ref[0].astype(dtype)

    return pl.pallas_call(
        body, name=name,
        grid_spec=pltpu.PrefetchScalarGridSpec(
            num_scalar_prefetch=1, grid=(R // tr,),
            in_specs=[pl.BlockSpec((1, tr, Cc), lambda i, ix: (layer, i, 0))],
            out_specs=pl.BlockSpec((1, tr, Cc), lambda i, ix: (ix[0], i, 0))),
        out_shape=jax.ShapeDtypeStruct((4, R, Cc), dtype),
        compiler_params=_params(("arbitrary",)),
    )(idx, a)


def _gather_ici(bufs):
    n = len(bufs)

    def plan(ins, outs, ssem, rsem, starting):
        x, y, c = _place()
        sends, recvs = [], []
        for a in range(n):
            for k, (px, py) in enumerate(_chip_peers(x, y)):
                mine, got = outs[a].at[2 * x + y, c], outs[a].at[2 * px + py, c]
                sends.append(_remote(mine, mine, ssem.at[3 * a + k], rsem.at[3 * a + k], (px, py, c)))
                if not starting:
                    recvs.append(_remote(got, got, ssem.at[3 * a + k], rsem.at[3 * a + k], (px, py, c)))
        return sends if starting else (sends, recvs)

    return _Comm(list(bufs), [_sds(b) for b in bufs], {a: a for a in range(n)}, 3 * n, plan)


def _gather_d2d(bufs):
    n = len(bufs)

    def plan(ins, outs, ssem, rsem, starting):
        x, y, c = _place()
        sends, recvs = [], []
        for a in range(n):
            for k, (px, py) in enumerate(_chip_peers(x, y)):
                got, other = outs[a].at[2 * px + py, c], outs[a].at[2 * px + py, 1 - c]
                sends.append(_remote(got, got, ssem.at[3 * a + k], rsem.at[3 * a + k], (x, y, 1 - c)))
                if not starting:
                    recvs.append(_remote(other, other, ssem.at[3 * a + k], rsem.at[3 * a + k], (x, y, 1 - c)))
        return sends if starting else (sends, recvs)

    return _Comm(list(bufs), [_sds(b) for b in bufs], {a: a for a in range(n)}, 3 * n, plan)


def _swap_halves(grads):
    n = len(grads)

    def plan(ins, outs, ssem, rsem, starting):
        x, y, c = _place()
        cps = [_remote(ins[a].at[:, 1 - c], outs[a], ssem.at[a], rsem.at[a], (x, y, 1 - c)) for a in range(n)]
        return cps if starting else (cps, cps)

    return _Comm(list(grads), [jax.ShapeDtypeStruct((4,) + g.shape[2:], g.dtype) for g in grads], {}, n, plan)


def _scatter_chips(halves):
    n = len(halves)

    def plan(ins, outs, ssem, rsem, starting):
        x, y, c = _place()
        cps = []
        for a in range(n):
            for k, (px, py) in enumerate(_chip_peers(x, y)):
                cps.append(_remote(ins[a].at[2 * px + py], outs[a].at[k], ssem.at[3 * a + k], rsem.at[3 * a + k], (px, py, c)))
        return cps if starting else (cps, cps)

    return _Comm(list(halves), [jax.ShapeDtypeStruct((3,) + h.shape[1:], h.dtype) for h in halves], {}, 3 * n, plan)


def _join_halves(parts, layer):
    n = len(parts)

    def plan(ins, outs, ssem, rsem, starting):
        x, y, c = _place()
        sends = [_remote(outs[a].at[layer, c], outs[a].at[layer, c], ssem.at[a], rsem.at[a], (x, y, 1 - c)) for a in range(n)]
        if starting:
            return sends
        return sends, [_remote(outs[a].at[layer, 1 - c], outs[a].at[layer, 1 - c], ssem.at[a], rsem.at[a], (x, y, 1 - c))
                       for a in range(n)]

    return _Comm(list(parts), [_sds(p) for p in parts], {a: a for a in range(n)}, n, plan)


def _add_half(name, idx, g, r):
    _, _, Rh, Cc = g.shape
    tr = _pick(Rh, 512, 16)

    def body(i_ref, g_ref, r_ref, o_ref):
        o_ref[...] = (g_ref[0].astype(F32) + r_ref[...].astype(F32)).astype(o_ref.dtype)

    return pl.pallas_call(
        body, name=name,
        grid_spec=pltpu.PrefetchScalarGridSpec(
            num_scalar_prefetch=1, grid=(4, Rh // tr),
            in_specs=[pl.BlockSpec((1, 1, tr, Cc), lambda s, i, ix: (s, ix[1], i, 0)),
                      pl.BlockSpec((1, tr, Cc), lambda s, i, ix: (s, i, 0))],
            out_specs=pl.BlockSpec((1, tr, Cc), lambda s, i, ix: (s, i, 0))),
        out_shape=jax.ShapeDtypeStruct((4, Rh, Cc), g.dtype),
        compiler_params=_params(("arbitrary", "arbitrary")),
    )(idx, g, r)


def _add_chips(name, idx, h, r, layer, nlayers, acc):
    _, Rh, Cc = h.shape
    tr = _pick(Rh, 256, 16)

    def body(i_ref, h_ref, r0, r1, r2, *rest):
        rest[-1][0, 0] = ((h_ref[0].astype(F32) + r0[0].astype(F32)) + r1[0].astype(F32)) + r2[0].astype(F32)

    taken = [] if acc is None else [acc]
    return pl.pallas_call(
        body, name=name,
        grid_spec=pltpu.PrefetchScalarGridSpec(
            num_scalar_prefetch=1, grid=(Rh // tr,),
            in_specs=[pl.BlockSpec((1, tr, Cc), lambda i, ix: (ix[0], i, 0))]
            + [pl.BlockSpec((1, tr, Cc), functools.partial(lambda i, ix, k: (k, i, 0), k=k)) for k in range(3)]
            + [ANY] * len(taken),
            out_specs=pl.BlockSpec((1, 1, tr, Cc), lambda i, ix: (layer, ix[1], i, 0))),
        out_shape=jax.ShapeDtypeStruct((nlayers, 2, Rh, Cc), F32),
        input_output_aliases={5: 0} if taken else {},
        compiler_params=_params(("arbitrary",)),
    )(idx, h, r, r, r, *taken)


def kernel(x, c, w_ada, b_ada, norm1_w, w_in, conv_dw_w, conv_dw_b, conv_ln_w, conv_ln_b, w_pw2, conv_out_norm_w, qkv_conv_w, a_log, dt_bias, dn_norm_w, w_out, norm2_w, w_up, w_down, final_ada_w, final_ada_b, final_norm_w, loss_target, m_w_ada, m_b_ada, m_norm1_w, m_w_in, m_conv_dw_w, m_conv_dw_b, m_conv_ln_w, m_conv_ln_b, m_w_pw2, m_conv_out_norm_w, m_qkv_conv_w, m_a_log, m_dt_bias, m_dn_norm_w, m_w_out, m_norm2_w, m_w_up, m_w_down, m_final_ada_w, m_final_ada_b, m_final_norm_w, v_w_ada, v_b_ada, v_norm1_w, v_w_in, v_conv_dw_w, v_conv_dw_b, v_conv_ln_w, v_conv_ln_b, v_w_pw2, v_conv_out_norm_w, v_qkv_conv_w, v_a_log, v_dt_bias, v_dn_norm_w, v_w_out, v_norm2_w, v_w_up, v_w_down, v_final_ada_w, v_final_ada_b, v_final_norm_w):
    names = ["w_ada", "b_ada", "norm1_w", "w_in", "conv_dw_w", "conv_dw_b", "conv_ln_w", "conv_ln_b", "w_pw2",
             "conv_out_norm_w", "qkv_conv_w", "a_log", "dt_bias", "dn_norm_w", "w_out", "norm2_w", "w_up", "w_down",
             "final_ada_w", "final_ada_b", "final_norm_w"]
    weights = dict(zip(names, [w_ada, b_ada, norm1_w, w_in, conv_dw_w, conv_dw_b, conv_ln_w, conv_ln_b, w_pw2,
                               conv_out_norm_w, qkv_conv_w, a_log, dt_bias, dn_norm_w, w_out, norm2_w, w_up, w_down,
                               final_ada_w, final_ada_b, final_norm_w]))
    mom1 = dict(zip(names, [m_w_ada, m_b_ada, m_norm1_w, m_w_in, m_conv_dw_w, m_conv_dw_b, m_conv_ln_w, m_conv_ln_b,
                            m_w_pw2, m_conv_out_norm_w, m_qkv_conv_w, m_a_log, m_dt_bias, m_dn_norm_w, m_w_out,
                            m_norm2_w, m_w_up, m_w_down, m_final_ada_w, m_final_ada_b, m_final_norm_w]))
    mom2 = dict(zip(names, [v_w_ada, v_b_ada, v_norm1_w, v_w_in, v_conv_dw_w, v_conv_dw_b, v_conv_ln_w, v_conv_ln_b,
                            v_w_pw2, v_conv_out_norm_w, v_qkv_conv_w, v_a_log, v_dt_bias, v_dn_norm_w, v_w_out,
                            v_norm2_w, v_w_up, v_w_down, v_final_ada_w, v_final_ada_b, v_final_norm_w]))

    B, S, D = x.shape
    T = B * S
    L = w_in.shape[0]
    C1 = conv_ln_w.shape[-1]
    NH, DH = a_log.shape[-1], dn_norm_w.shape[-1]
    DN = NH * DH
    FF = w_down.shape[1] * 4
    IN = w_in.shape[-1] * 4
    INP = 6 * C1 + LANE
    KC, KQ = conv_dw_w.shape[1], qkv_conv_w.shape[1]
    NW, NF = w_ada.shape[-1], final_ada_w.shape[-1]
    assert DH == LANE and DN == C1 and IN == 6 * C1 + 2 * NH and S % PAIR == 0
    xi, yi, ci = _place()
    s_me, me = 2 * xi + yi, 4 * xi + 2 * yi + ci
    idx = jnp.stack([s_me, ci]).astype(jnp.int32)
    tmf, tmb = _pick(S, 256, SUB), _pick(S, 128, SUB)
    ts, tsq = _pick(S, 256, _halo(KC)), _pick(S, 512, _halo(KQ))

    shapes1 = [(B, D), conv_dw_w.shape, qkv_conv_w.shape]
    g1 = _pack([c, conv_dw_w, qkv_conv_w])
    g1 = _all_gather8("gather_cond", g1).reshape(8, -1)
    c_all, cw_all, qw_all = _unpack(g1, shapes1)
    c_all = c_all.reshape(8 * B, D)
    conv_w_full = jnp.moveaxis(cw_all[0::2], 0, 2).reshape(L, KC, C1)
    qkv_w_full = jnp.moveaxis(qw_all[0::2], 0, 2).reshape(L, KQ, 3 * DN)
    (c_act,) = _rows_fwd("cond_silu", _silu_row, [_whole(c_all)], [], [], [(D, F32)], tm=8 * B)

    mods = []
    for l in range(L):
        bsh = lax.dynamic_slice(b_ada[l], (s_me * NW,), (NW,)).reshape(1, NW)
        mods.append(_mm(f"mod_{l}", c_act, w_ada, "nn", bias=bsh, b_layer=l))
    bsh = lax.dynamic_slice(final_ada_b, (s_me * NF,), (NF,)).reshape(1, NF)
    mods.append(_mm("mod_final", c_act, final_ada_w, "nn", bias=bsh))
    shapes2 = [(8 * B, NW)] * L + [(8 * B, NF)]
    g2 = _all_gather8("gather_mod", _pack(mods)).reshape(8, -1)[0::2]
    mod_all = [jnp.moveaxis(t, 0, 1).reshape(8 * B, -1) for t in _unpack(g2, shapes2)]
    mod_me = [lax.dynamic_slice(t, (B * me, 0), (B, t.shape[1])) for t in mod_all]

    def split_mod(t, n):
        return [t[:, k * D:(k + 1) * D].reshape(B, 1, D) for k in range(n)]

    def cast_weights(l):
        bufs = [_cast_into_slot(f"cast_{l}_{k}", idx, a, l, MXU_DTYPE) for k, a in enumerate([w_in, w_pw2, w_out, w_up, w_down])]
        return [b.reshape(4, 2, b.shape[1] // 2, b.shape[2]) for b in bufs]

    order = ["w_in", "w_pw2", "w_out", "w_up", "w_down"]
    to_natural = dict(
        w_in=lambda g: jnp.pad(jnp.moveaxis(g.reshape(4, D, IN // 4), 0, 1).reshape(D, IN), ((0, 0), (0, INP - IN))),
        w_pw2=lambda g: g.reshape(C1, C1), w_out=lambda g: g.reshape(D, D),
        w_up=lambda g: jnp.moveaxis(g.reshape(4, D, FF // 4), 0, 1).reshape(D, FF), w_down=lambda g: g.reshape(FF, D))

    def natural(got, keys=order):
        return {k: to_natural[k](g) for k, g in zip(keys, got)}

    first = cast_weights(0)
    late = first[3:]
    first = _run_comm("gather_w_ici_0", _gather_ici(first[:3]))
    wfull = {0: natural(_run_comm("gather_w_d2d_0", _gather_d2d(first)), order[:3])}

    pad_row = lambda v: jnp.pad(v.reshape(1, -1), ((0, 0), (NH, LANE - 2 * NH)))
    row = lambda v: v.reshape(1, -1)

    saved = []
    xcur = x.reshape(T, D)
    ycur = jnp.zeros((T, D), F32)
    gate_prev = jnp.zeros((B, 1, D), F32)
    for l in range(L):
        sh1, sc1, g1_, sh2, sc2, g2_ = split_mod(mod_me[l], 6)
        sv = dict(x_in=xcur, y_in=ycur, gate_in=gate_prev, sh1=sh1, sc1=sc1, g1=g1_, sh2=sh2, sc2=sc2, g2=g2_)
        junction1 = functools.partial(_rows_fwd, f"junction1_{l}", _junction, [_whole(xcur), _whole(ycur)],
                                      [gate_prev, sh1, sc1], [row(norm1_w[l])], [(D, F32), (D, MXU_DTYPE)],
                                      tm=tmf, tpe=S // tmf)
        if l == 0:
            x0, h1 = junction1()
        else:
            (x0, h1), got = junction1(comm=_gather_d2d(travelling))
            wfull[l] = natural(got)
        W = wfull[l]
        nxt = cast_weights(l + 1) if l + 1 < L else None
        if nxt is None:
            proj = _mm(f"proj_{l}", h1, W["w_in"], "nn", tn=896)
        else:
            proj, got_in = _mm(f"proj_{l}", h1, W["w_in"], "nn", tn=896, comm=_gather_ici(nxt[:1]))
        (u0,) = _rows_fwd(f"glu_{l}", _glu, [(proj, C1, 0), (proj, C1, 1)], [], [], [(C1, F32)], tm=tmf)
        if l > 0:
            u1 = _conv_fwd(f"conv_{l}", u0, 0, conv_w_full[l], row(conv_dw_b[l]), ts=ts, tpe=S // ts)
        else:
            u1, late_up = _conv_fwd(f"conv_{l}", u0, 0, conv_w_full[l], row(conv_dw_b[l]), ts=ts, tpe=S // ts,
                                    comm=_gather_ici(late[:1]))
        (u2,) = _rows_fwd(f"ln_silu_{l}", _ln_silu, [_whole(u1)], [], [row(conv_ln_w[l]), row(conv_ln_b[l])],
                          [(C1, MXU_DTYPE)], tm=tmf)
        u3 = _mm(f"pw2_{l}", u2, W["w_pw2"], "nn")
        (y_conv,) = _rows_fwd(f"conv_out_norm_{l}", _rms, [_whole(u3)], [], [row(conv_out_norm_w[l])],
                              [(C1, MXU_DTYPE)], tm=tmf)
        if l > 0:
            qkv = _conv_fwd(f"qkv_conv_{l}", proj, 2 * C1, qkv_w_full[l], None, ts=tsq, tpe=S // tsq)
        else:
            qkv, late_down = _conv_fwd(f"qkv_conv_{l}", proj, 2 * C1, qkv_w_full[l], None, ts=tsq, tpe=S // tsq,
                                       comm=_gather_ici(late[1:]))
        al, dtb = pad_row(a_log[l]), pad_row(dt_bias[l])
        (u, w, qd, kd, gl, attn), got_po = _dn_prep_fwd(f"dn_prep_{l}", qkv, proj, 6 * C1, al, dtb, NH=NH,
                                                        comm=None if nxt is None else _gather_ici(nxt[1:3]))
        o, st = _dn_scan_fwd(f"dn_scan_{l}", u, w, qd, kd, gl, attn, NH=NH, B=B)
        (y_dn,) = _rows_fwd(f"dn_out_norm_{l}", _gated_rms, [_whole(o), (proj, DN, 5)], [], [row(dn_norm_w[l])],
                            [(DN, MXU_DTYPE)], tm=tmf, ngroups=NH)
        ycat = jnp.concatenate([y_conv, y_dn], axis=1)
        if l > 0:
            y = _mm(f"out_{l}", ycat, W["w_out"], "nn")
        else:
            y, got = _mm(f"out_{l}", ycat, W["w_out"], "nn", comm=_gather_d2d(list(late_up) + list(late_down)))
            W.update(natural(got, order[3:]))
        x1, h2 = _rows_fwd(f"junction2_{l}", _junction, [_whole(x0), _whole(y)], [g1_, sh2, sc2],
                           [row(norm2_w[l])], [(D, F32), (D, MXU_DTYPE)], tm=tmf, tpe=S // tmf)
        relu2 = (lambda t: (t, _relu2(t)[0]), [], [F32, MXU_DTYPE])
        if nxt is not None:
            (up, act), got_up = _mm(f"up_{l}", h2, W["w_up"], "nn", post=relu2, comm=_gather_ici(nxt[3:4]))
            mlp, got_down = _mm(f"down_{l}", act, W["w_down"], "nn", comm=_gather_ici(nxt[4:]))
            travelling = list(got_in) + list(got_po) + list(got_up) + list(got_down)
        else:
            up, act = _mm(f"up_{l}", h2, W["w_up"], "nn", post=relu2)
            mlp = _mm(f"down_{l}", act, W["w_down"], "nn")
        sv.update(x0=x0, h1=h1, proj=proj, u0=u0, u1=u1, u2=u2, u3=u3, qkv=qkv, al=al, dtb=dtb, u=u, w=w, qd=qd, kd=kd,
                  gl=gl, attn=attn, o=o, st=st, ycat=ycat, y=y, x1=x1, h2=h2, up=up, act=act)
        saved.append(sv)
        xcur, ycur, gate_prev = x1, mlp, g2_

    shf, scf = split_mod(mod_me[L], 2)
    tgt = loss_target.reshape(T, D)
    rowloss, dx, dy, dgate, dshf, dscf, dfinal_norm = _rows_vjp(
        "loss_head", _final_loss, [_whole(xcur), _whole(ycur), _whole(tgt)], [gate_prev, shf, scf], [row(final_norm_w)],
        None, [F32, MXU_DTYPE, None], tm=tmb, tpe=S // tmb, primal=[(1, F32)])
    loss = lax.psum(jnp.sum(rowloss), ("x", "y", "c"))

    dmods = [None] * L
    small = [None] * L
    stacked = [None] * 5
    pending = None
    for l in reversed(range(L)):
        W, sv = wfull[l], saved[l]
        d_relu2 = (lambda t, u_: (t * (2.0 * jnp.maximum(u_, 0.0)),), [sv["up"]], [MXU_DTYPE])
        if pending is None:
            dup = _mm(f"d_down_x_{l}", dy, W["w_down"], "nt", post=d_relu2)
            gw_down = _mm(f"d_down_w_{l}", sv["act"], dy, "tn", COMM_DTYPE)
            dh2 = _mm(f"d_up_x_{l}", dup, W["w_up"], "nt")
            gw_up = _mm(f"d_up_w_{l}", sv["h2"], dup, "tn", COMM_DTYPE)
        else:
            dup, got = _mm(f"d_down_x_{l}", dy, W["w_down"], "nt", post=d_relu2, comm=_swap_halves(pending))
            halves = [_add_half(f"rs_add_half_{l + 1}_{a}", idx, g, r) for a, (g, r) in enumerate(zip(pending, got))]
            gw_down, got_down = _mm(f"d_down_w_{l}", sv["act"], dy, "tn", COMM_DTYPE, comm=_scatter_chips(halves[4:]))
            dh2, got_in = _mm(f"d_up_x_{l}", dup, W["w_up"], "nt", comm=_scatter_chips(halves[:1]))
            gw_up, got_po = _mm(f"d_up_w_{l}", sv["h2"], dup, "tn", COMM_DTYPE, comm=_scatter_chips(halves[1:3]))
        shard4 = lambda g: g.reshape(4, 2, g.shape[1] // 2, g.shape[2])
        d_junction2 = functools.partial(
            _rows_vjp, f"d_junction2_{l}", _junction, [_whole(sv["x0"]), _whole(sv["y"])], [sv["g1"], sv["sh2"], sv["sc2"]],
            [row(norm2_w[l])], [_whole(dx), _whole(dh2)], [F32, MXU_DTYPE], tm=tmb, tpe=S // tmb)
        early = None
        if l > 0:
            dx0, dyo, dg1, dsh2, dsc2, dn2 = d_junction2()
        else:
            early = [shard4(jnp.moveaxis(gw_up.reshape(D, 4, FF // 4), 1, 0)), shard4(gw_down.reshape(4, FF // 4, D))]
            (dx0, dyo, dg1, dsh2, dsc2, dn2), got = d_junction2(comm=_swap_halves(early))
            halves_e = [_add_half(f"rs_add_half_0_{3 + a}", idx, g, r) for a, (g, r) in enumerate(zip(early, got))]
        dycat = _mm(f"d_out_x_{l}", dyo, W["w_out"], "nt")
        gw_out = _mm(f"d_out_w_{l}", sv["ycat"], dyo, "tn", COMM_DTYPE)
        proj = sv["proj"]
        do, dz, ddn = _rows_vjp(f"d_dn_out_norm_{l}", _gated_rms, [_whole(sv["o"]), (proj, DN, 5)], [],
                                [row(dn_norm_w[l])], [(dycat, DN, 1)], [F32, MXU_DTYPE], tm=tmb, ngroups=NH)
        cts, got_eu = _dn_scan_bwd(f"d_dn_scan_{l}", do, sv["u"], sv["w"], sv["qd"], sv["kd"], sv["gl"], sv["attn"], sv["st"],
                                   NH=NH, B=B, comm=None if early is None else _scatter_chips(halves_e[:1]))
        (dq, dk, dv, dba, dal, ddt), got_up = _dn_prep_bwd(
            f"d_dn_prep_{l}", sv["qkv"], proj, 6 * C1, sv["al"], sv["dtb"], list(cts), F32, NH=NH,
            comm=None if pending is None else _scatter_chips(halves[3:4]))
        if pending is not None:
            got = list(got_in) + list(got_po) + list(got_up) + list(got_down)
            parts = [_add_chips(f"rs_add_chips_{l + 1}_{a}", idx, h, r, l + 1, L, stacked[a])
                     for a, (h, r) in enumerate(zip(halves, got))]
        dqkv, gqw = [], []
        for k, dpart in enumerate((dq, dk, dv)):
            dxp, dwp, _ = _conv_bwd(f"d_qkv_conv_{l}_{k}", proj, (2 + k) * C1, dpart,
                                    qkv_w_full[l][:, k * DN:(k + 1) * DN], MXU_DTYPE, ts=tsq, tpe=S // tsq)
            dqkv.append(dxp)
            gqw.append(dwp)
        (du3, dcon) = _rows_vjp(f"d_conv_out_norm_{l}", _rms, [_whole(sv["u3"])], [], [row(conv_out_norm_w[l])],
                                [(dycat, C1, 0)], [MXU_DTYPE], tm=tmb)
        du2 = _mm(f"d_pw2_x_{l}", du3, W["w_pw2"], "nt")
        gw_pw2 = _mm(f"d_pw2_w_{l}", sv["u2"], du3, "tn", COMM_DTYPE)
        du1, dlnw, dlnb = _rows_vjp(f"d_ln_silu_{l}", _ln_silu, [_whole(sv["u1"])], [],
                                    [row(conv_ln_w[l]), row(conv_ln_b[l])], [_whole(du2)], [F32], tm=tmb)
        if early is None:
            du0, gcw, gcb = _conv_bwd(f"d_conv_{l}", sv["u0"], 0, du1, conv_w_full[l], F32, ts=ts, tpe=S // ts)
        else:
            (du0, gcw, gcb), got_ed = _conv_bwd(f"d_conv_{l}", sv["u0"], 0, du1, conv_w_full[l], F32, ts=ts, tpe=S // ts,
                                                comm=_scatter_chips(halves_e[1:]))
        dval, dgate_c = _rows_vjp(f"d_glu_{l}", _glu, [(proj, C1, 0), (proj, C1, 1)], [], [], [_whole(du0)],
                                  [MXU_DTYPE, MXU_DTYPE], tm=tmb)
        dproj = jnp.concatenate([dval, dgate_c] + dqkv + [dz, dba.astype(MXU_DTYPE)], axis=1)
        if pending is None:
            dh1 = _mm(f"d_proj_x_{l}", dproj, W["w_in"], "nt", tk=896)
        else:
            dh1, stacked = _mm(f"d_proj_x_{l}", dproj, W["w_in"], "nt", tk=896, comm=_join_halves(parts, l + 1))
        gw_in = _mm(f"d_proj_w_{l}", sv["h1"], dproj, "tn", COMM_DTYPE, tn=896)[:, :IN]
        d_junction1 = functools.partial(
            _rows_vjp, f"d_junction1_{l}", _junction, [_whole(sv["x_in"]), _whole(sv["y_in"])],
            [sv["gate_in"], sv["sh1"], sv["sc1"]], [row(norm1_w[l])], [_whole(dx0), _whole(dh1)], [F32, MXU_DTYPE],
            tm=tmb, tpe=S // tmb)
        if early is None:
            dxn, dyn, dg2p, dsh1, dsc1, dn1 = d_junction1()
        else:
            parts_e = [_add_chips(f"rs_add_chips_0_{3 + a}", idx, h, r, 0, L, stacked[3 + a])
                       for a, (h, r) in enumerate(zip(halves_e, list(got_eu) + list(got_ed)))]
            (dxn, dyn, dg2p, dsh1, dsc1, dn1), joined = d_junction1(comm=_join_halves(parts_e, 0))
            stacked = list(stacked[:3]) + list(joined)
        dmods[l] = [dsh1, dsc1, dg1, dsh2, dsc2, dgate]
        small[l] = dict(norm1_w=dn1, conv_dw_w=gcw, conv_dw_b=gcb, conv_ln_w=dlnw, conv_ln_b=dlnb, conv_out_norm_w=dcon,
                        qkv_conv_w=jnp.concatenate(gqw, axis=1), a_log=dal[:, NH:2 * NH], dt_bias=ddt[:, NH:2 * NH],
                        dn_norm_w=ddn, norm2_w=dn2)
        pending = [shard4(jnp.moveaxis(gw_in.reshape(D, 4, IN // 4), 1, 0)), shard4(gw_pw2.reshape(4, C1 // 4, C1)),
                   shard4(gw_out.reshape(4, D // 4, D))]
        if early is None:
            pending += [shard4(jnp.moveaxis(gw_up.reshape(D, 4, FF // 4), 1, 0)), shard4(gw_down.reshape(4, FF // 4, D))]
        dx, dy, dgate = dxn, dyn, dg2p

    got = _run_comm("rs_swap_0", _swap_halves(pending))
    halves = [_add_half(f"rs_add_half_0_{a}", idx, g, r) for a, (g, r) in enumerate(zip(pending, got))]
    got = _run_comm("rs_scatter_0", _scatter_chips(halves))
    parts = [_add_chips(f"rs_add_chips_0_{a}", idx, h, r, 0, L, stacked[a]) for a, (h, r) in enumerate(zip(halves, got))]
    stacked = list(_run_comm("rs_join_0", _join_halves(parts, 0))) + list(stacked[3:])

    grad_x = dx.reshape(B, S, D)

    small_names = ["norm1_w", "conv_dw_w", "conv_dw_b", "conv_ln_w", "conv_ln_b", "conv_out_norm_w", "qkv_conv_w", "a_log",
                   "dt_bias", "dn_norm_w", "norm2_w"]
    dmod_flat = jnp.concatenate([jnp.concatenate([t.reshape(B, D) for t in dmods[l]], axis=1) for l in range(L)]
                                + [dshf.reshape(B, D), dscf.reshape(B, D)], axis=1)
    small_list = [small[l][n] for l in range(L) for n in small_names] + [dfinal_norm]
    shapes3 = [dmod_flat.shape] + [t.shape for t in small_list]
    g3 = _all_gather8("gather_grads", _pack([dmod_flat] + small_list))
    rows3 = g3.shape[0] // 8
    g3 = g3.reshape(8, rows3, PACK_LANES)
    summed = _unpack(_sum0("sum_small_grads", g3).reshape(-1), shapes3)[1:]
    dmod_all = _unpack(g3.reshape(8, -1), shapes3[:1])[0].reshape(8 * B, -1)
    nm = dmod_all.shape[1]
    grad_b_all = _sum0("sum_mod_grads", dmod_all.reshape(8 * B, nm // PACK_LANES, PACK_LANES)).reshape(-1)
    grads = {}
    dm = jnp.concatenate([lax.dynamic_slice(dmod_all, (0, l * 6 * D + s_me * NW), (8 * B, NW)) for l in range(L)], axis=1)
    grads["w_ada"] = _mm("d_ada_w", c_act, dm, "tn", stack=L)
    grads["b_ada"] = grad_b_all[:L * 6 * D].reshape(L, 6 * D)
    dm = lax.dynamic_slice(dmod_all, (0, L * 6 * D + s_me * NF), (8 * B, NF))
    grads["final_ada_w"] = _mm("d_final_ada_w", c_act, dm, "tn")
    grads["final_ada_b"] = grad_b_all[L * 6 * D:]
    per_layer = {n: [] for n in small_names}
    for l in range(L):
        for k, n in enumerate(small_names):
            per_layer[n].append(summed[l * len(small_names) + k])
    for n in small_names:
        t = jnp.stack(per_layer[n])
        if n == "conv_dw_w":
            t = lax.dynamic_slice(t, (0, 0, s_me * (C1 // 4)), (L, KC, C1 // 4))
        elif n == "qkv_conv_w":
            t = lax.dynamic_slice(t, (0, 0, s_me * (3 * DN // 4)), (L, KQ, 3 * DN // 4))
        grads[n] = t.reshape(weights[n].shape)
    grads["final_norm_w"] = summed[-1].reshape(final_norm_w.shape)
    for k, n in enumerate(["w_in", "w_pw2", "w_out", "w_up", "w_down"]):
        grads[n] = stacked[k].reshape(weights[n].shape)

    delta, new_m, new_v = {}, {}, {}
    big_names = ["w_ada", "w_in", "w_pw2", "w_out", "w_up", "w_down", "final_ada_w"]
    for n in big_names:
        shp = weights[n].shape
        three = lambda t: t.reshape((-1,) + shp[-2:])
        d_, m_, v_ = _adamw(f"adamw_{n}", three(weights[n]), three(grads[n]), three(mom1[n]), three(mom2[n]))
        delta[n], new_m[n], new_v[n] = d_.reshape(shp), m_.reshape(shp), v_.reshape(shp)
    rest = [n for n in names if n not in big_names]
    rshapes = [weights[n].shape for n in rest]
    packed = [_pack([d[n] for n in rest])[None] for d in (weights, grads, mom1, mom2)]
    outs = _adamw("adamw_small", *packed)
    for dst, arr in zip((delta, new_m, new_v), outs):
        for n, t in zip(rest, _unpack(arr.reshape(-1), rshapes)):
            dst[n] = t

    return (loss, grad_x, *[grads[n] for n in names], *[delta[n] for n in names], *[new_m[n] for n in names],
            *[new_v[n] for n in names])
```

```python
import functools
import math
import typing

import jax
import jax.numpy as jnp
from jax import lax
from jax.experimental import pallas as pl
from jax.experimental.pallas import tpu as pltpu

F32 = jnp.float32
MXU_DTYPE = jnp.bfloat16
COMM_DTYPE = jnp.bfloat16
HI = lax.Precision.HIGHEST
CHUNK = 64
PAIR = 2 * CHUNK
EPS = 1e-6
LANE = 128
SUB = 8
PACK_LANES = 1024
VMEM_LIMIT = 56 * 1024 * 1024
ADAM_LR, ADAM_B1, ADAM_B2, ADAM_EPS, ADAM_WD, ADAM_STEP = 0.001, 0.9, 0.999, 1e-08, 0.01, 10
MESH = pl.DeviceIdType.MESH
ANY = pl.BlockSpec(memory_space=pl.ANY)
NN, NT, TN = ((1,), (0,)), ((1,), (1,)), ((0,), (0,))


def _pick(dim, pref, mult):
    for t in range(min(pref, dim), 0, -1):
        if dim % t == 0 and t % mult == 0:
            return t
    return dim


def _params(sem):
    return pltpu.CompilerParams(dimension_semantics=sem, vmem_limit_bytes=VMEM_LIMIT)


def _sigmoid(v):
    return 1.0 / (1.0 + jnp.exp(-v))


def _silu(v):
    return v * _sigmoid(v)


def _row_specs(rows, exps, gls, tm, tpe):
    specs = [pl.BlockSpec((tm, w), functools.partial(lambda i, j: (i, j), j=cb)) for _, w, cb in rows]
    specs += [pl.BlockSpec((1, 1, e.shape[-1]), lambda i: (i // tpe, 0, 0)) for e in exps]
    specs += [pl.BlockSpec((1, g.shape[-1]), lambda i: (0, 0)) for g in gls]
    return specs


def _rows_fwd(name, fn, rows, exps, gls, outs, *, tm, tpe=1, ngroups=1, comm=None):
    T = rows[0][0].shape[0]
    nr, ne, ng = len(rows), len(exps), len(gls)
    nsteps = T // tm

    def body(*refs):
        r, e, g, o = refs[:nr], refs[nr:nr + ne], refs[nr + ne:nr + ne + ng], refs[nr + ne + ng:]
        ev = [t[0].astype(F32) for t in e]
        gv = [t[...].astype(F32) for t in g]
        for k in range(ngroups):
            rv = [t[:, k * (w // ngroups):(k + 1) * (w // ngroups)].astype(F32) for t, (_, w, _) in zip(r, rows)]
            res = fn(*rv, *ev, *gv)
            for oref, val, (w, dt) in zip(o, res, outs):
                gw = w // ngroups
                oref[:, k * gw:(k + 1) * gw] = val.astype(dt)

    kw = dict(in_specs=_row_specs(rows, exps, gls, tm, tpe),
              out_specs=[pl.BlockSpec((tm, w), lambda i: (i, 0)) for w, _ in outs],
              out_shape=[jax.ShapeDtypeStruct((T, w), dt) for w, dt in outs], scratch_shapes=[])
    body, kw, args = _attach(comm, body, kw, [a for a, _, _ in rows] + list(exps) + list(gls),
                             lambda: (pl.program_id(0) == 0, pl.program_id(0) == nsteps - 1))
    res = pl.pallas_call(body, name=name, grid=(nsteps,), compiler_params=_params(("arbitrary",)), **kw)(*args)
    return res if comm is None else (res[:len(outs)], res[len(outs):])


def _rows_vjp(name, fn, rows, exps, gls, cts, row_dtypes, *, tm, tpe=1, ngroups=1, primal=None, comm=None):
    T = rows[0][0].shape[0]
    nr, ne, ng = len(rows), len(exps), len(gls)
    nc = 0 if cts is None else len(cts)
    keep = [k for k, dt in enumerate(row_dtypes) if dt is not None]
    npr = 0 if primal is None else len(primal)

    def body(*refs):
        r, e, g = refs[:nr], refs[nr:nr + ne], refs[nr + ne:nr + ne + ng]
        c = refs[nr + ne + ng:nr + ne + ng + nc]
        o = refs[nr + ne + ng + nc:]
        po, ro, eo, go = o[:npr], o[npr:npr + len(keep)], o[npr + len(keep):npr + len(keep) + ne], o[npr + len(keep) + ne:]
        i = pl.program_id(0)
        ev = [t[0].astype(F32) for t in e]
        gv = [t[...].astype(F32) for t in g]
        esum = [jnp.zeros_like(v) for v in ev]
        gsum = [jnp.zeros_like(v) for v in gv]
        for k in range(ngroups):
            rv = [t[:, k * (w // ngroups):(k + 1) * (w // ngroups)].astype(F32) for t, (_, w, _) in zip(r, rows)]
            res, pull = jax.vjp(fn, *rv, *ev, *gv)
            if cts is None:
                ct = tuple(jnp.ones_like(v) for v in res)
            else:
                ct = tuple(t[:, k * (w // ngroups):(k + 1) * (w // ngroups)].astype(F32) for t, (_, w, _) in zip(c, cts))
            grads = pull(ct)
            for oref, val, (w, dt) in zip(po, res, primal or ()):
                gw = w // ngroups
                oref[:, k * gw:(k + 1) * gw] = val.astype(dt)
            for oref, idx in zip(ro, keep):
                gw = rows[idx][1] // ngroups
                oref[:, k * gw:(k + 1) * gw] = grads[idx].astype(row_dtypes[idx])
            esum = [s + d for s, d in zip(esum, grads[nr:nr + ne])]
            gsum = [s + d for s, d in zip(gsum, grads[nr + ne:])]

        if ne:
            @pl.when(i % tpe == 0)
            def _():
                for oref in eo:
                    oref[...] = jnp.zeros_like(oref)
            for oref, s in zip(eo, esum):
                oref[0] += s
        if ng:
            @pl.when(i == 0)
            def _():
                for oref in go:
                    oref[...] = jnp.zeros_like(oref)
            for oref, s in zip(go, gsum):
                oref[...] += s

    out_specs = [pl.BlockSpec((tm, w), lambda i: (i, 0)) for w, _ in (primal or ())]
    out_shape = [jax.ShapeDtypeStruct((T, w), dt) for w, dt in (primal or ())]
    out_specs += [pl.BlockSpec((tm, rows[k][1]), lambda i: (i, 0)) for k in keep]
    out_shape += [jax.ShapeDtypeStruct((T, rows[k][1]), row_dtypes[k]) for k in keep]
    out_specs += [pl.BlockSpec((1, 1, e.shape[-1]), lambda i: (i // tpe, 0, 0)) for e in exps]
    out_shape += [jax.ShapeDtypeStruct(e.shape, F32) for e in exps]
    out_specs += [pl.BlockSpec((1, g.shape[-1]), lambda i: (0, 0)) for g in gls]
    out_shape += [jax.ShapeDtypeStruct(g.shape, F32) for g in gls]
    ct_specs = [] if cts is None else [pl.BlockSpec((tm, w), functools.partial(lambda i, j: (i, j), j=cb)) for _, w, cb in cts]
    ct_arrs = [] if cts is None else [a for a, _, _ in cts]
    nsteps, nown = T // tm, len(out_shape)
    kw = dict(in_specs=_row_specs(rows, exps, gls, tm, tpe) + ct_specs, out_specs=out_specs, out_shape=out_shape,
              scratch_shapes=[])
    body, kw, args = _attach(comm, body, kw, [a for a, _, _ in rows] + list(exps) + list(gls) + ct_arrs,
                             lambda: (pl.program_id(0) == 0, pl.program_id(0) == nsteps - 1))
    res = pl.pallas_call(body, name=name, grid=(nsteps,), compiler_params=_params(("arbitrary",)), **kw)(*args)
    return res if comm is None else (res[:nown], res[nown:])


def _whole(a):
    return (a, a.shape[-1], 0)


def _junction(x, y, gate, shift, scale, w):
    xn = x + gate * y
    r = lax.rsqrt(jnp.mean(xn * xn, axis=-1, keepdims=True) + EPS)
    return xn, (xn * r * w) * (1.0 + scale) + shift


def _final_loss(x, y, tgt, gate, shift, scale, w):
    _, out = _junction(x, y, gate, shift, scale, w)
    err = out - tgt
    return (0.5 * jnp.mean(err * err, axis=-1, keepdims=True),)


def _glu(val, gate):
    return (val * _sigmoid(gate),)


def _ln_silu(u, w, b):
    xc = u - jnp.mean(u, axis=-1, keepdims=True)
    y = xc * lax.rsqrt(jnp.mean(xc * xc, axis=-1, keepdims=True) + EPS) * w + b
    return (_silu(y),)


def _rms(u, w):
    return (u * lax.rsqrt(jnp.mean(u * u, axis=-1, keepdims=True) + EPS) * w,)


def _gated_rms(o, z, w):
    return (o * lax.rsqrt(jnp.mean(o * o, axis=-1, keepdims=True) + EPS) * w * _silu(z),)


def _relu2(u):
    r = jnp.maximum(u, 0.0)
    return (r * r,)


def _silu_row(u):
    return (_silu(u),)


class _Comm(typing.NamedTuple):
    ins: list
    outs: list
    aliases: dict
    nsem: int
    plan: typing.Callable


def _attach(comm, body, kw, args, first_last):
    if comm is None:
        return body, kw, args
    ni0, no0, ns0 = len(kw["in_specs"]), len(kw["out_specs"]), len(kw["scratch_shapes"])
    ni, no = len(comm.ins), len(comm.outs)
    kw = dict(kw, in_specs=kw["in_specs"] + [ANY] * ni, out_specs=kw["out_specs"] + [ANY] * no,
              out_shape=kw["out_shape"] + comm.outs,
              scratch_shapes=kw["scratch_shapes"] + [pltpu.SemaphoreType.DMA((comm.nsem,)), pltpu.SemaphoreType.DMA((comm.nsem,))],
              input_output_aliases={ni0 + k: no0 + v for k, v in comm.aliases.items()})

    def carrying(*refs):
        own_in, c_in = refs[:ni0], refs[ni0:ni0 + ni]
        own_out, c_out = refs[ni0 + ni:ni0 + ni + no0], refs[ni0 + ni + no0:ni0 + ni + no0 + no]
        scratch = refs[ni0 + ni + no0 + no:]
        ssem, rsem = scratch[ns0], scratch[ns0 + 1]
        first, last = first_last()

        @pl.when(first)
        def _():
            for cp in comm.plan(c_in, c_out, ssem, rsem, True):
                cp.start()

        body(*own_in, *own_out, *scratch[:ns0])

        @pl.when(last)
        def _():
            sends, recvs = comm.plan(c_in, c_out, ssem, rsem, False)
            for cp in sends:
                cp.wait_send()
            for cp in recvs:
                cp.wait_recv()

    return carrying, kw, args + list(comm.ins)


def _run_comm(name, comm):
    ni, no = len(comm.ins), len(comm.outs)

    def body(*refs):
        for cp in comm.plan(refs[:ni], refs[ni:ni + no], refs[-2], refs[-1], True):
            cp.start()
        sends, recvs = comm.plan(refs[:ni], refs[ni:ni + no], refs[-2], refs[-1], False)
        for cp in sends:
            cp.wait_send()
        for cp in recvs:
            cp.wait_recv()

    return pl.pallas_call(
        body, name=name, in_specs=[ANY] * ni, out_specs=[ANY] * no, out_shape=comm.outs,
        input_output_aliases=comm.aliases,
        scratch_shapes=[pltpu.SemaphoreType.DMA((comm.nsem,)), pltpu.SemaphoreType.DMA((comm.nsem,))],
    )(*comm.ins)


def _mm(name, a, b, mode, out_dtype=F32, bias=None, tm=1024, tn=1024, tk=2048, comm=None, post=None, b_layer=None, stack=1):
    bshape = b.shape[1:] if b_layer is not None else b.shape
    if mode == "nn":
        (M, K), N = a.shape, bshape[1]
    elif mode == "nt":
        (M, K), N = a.shape, bshape[0]
    else:
        (K, M), N = a.shape, bshape[1] // stack
    tm, tn, tk = _pick(M, tm, LANE), _pick(N, tn, LANE), _pick(K, tk, LANE)
    nk, nj = K // tk, N // tn
    grid = (M // tm, stack * nj, nk)
    dn = {"nn": NN, "nt": NT, "tn": TN}[mode]
    a_spec = pl.BlockSpec((tk, tm), lambda i, j, k: (k, i)) if mode == "tn" else pl.BlockSpec((tm, tk), lambda i, j, k: (i, k))
    b_spec = pl.BlockSpec((tn, tk), lambda i, j, k: (j, k)) if mode == "nt" else pl.BlockSpec((tk, tn), lambda i, j, k: (k, j))
    if b_layer is not None:
        b_spec = pl.BlockSpec((1, tk, tn), lambda i, j, k: (b_layer, k, j))
    specs, args = [a_spec, b_spec], [a, b]
    if bias is not None:
        specs.append(pl.BlockSpec((1, tn), lambda i, j, k: (0, j)))
        args.append(bias)

    o_spec = pl.BlockSpec((tm, tn), lambda i, j, k: (i, j))
    if stack > 1:
        o_spec = pl.BlockSpec((1, tm, tn), lambda i, j, k: (j // nj, i, j % nj))
    fn, extra, out_dtypes = post if post is not None else (lambda t: (t,), [], [out_dtype])
    specs += [o_spec] * len(extra)
    args += list(extra)
    nin, nout = len(args), len(out_dtypes)

    def body(*refs):
        a_ref, b_ref = refs[0], refs[1]
        outs, acc = refs[nin:nin + nout], refs[-1]
        k = pl.program_id(2)

        @pl.when(k == 0)
        def _():
            acc[...] = jnp.zeros_like(acc)

        b_tile = b_ref[...] if b_layer is None else b_ref[0]
        acc[...] += lax.dot_general(a_ref[...].astype(MXU_DTYPE), b_tile.astype(MXU_DTYPE), (dn, ((), ())),
                                    preferred_element_type=F32)

        @pl.when(k == nk - 1)
        def _():
            res = acc[...]
            if bias is not None:
                res = res + refs[2][...]
            for o_ref, val in zip(outs, fn(res, *[r[...] for r in refs[nin - len(extra):nin]])):
                o_ref[...] = val.astype(o_ref.dtype).reshape(o_ref.shape)

    def first_last():
        at = [pl.program_id(d) for d in range(3)]
        first = jnp.logical_and(jnp.logical_and(at[0] == 0, at[1] == 0), at[2] == 0)
        last = jnp.logical_and(jnp.logical_and(at[0] == grid[0] - 1, at[1] == grid[1] - 1), at[2] == grid[2] - 1)
        return first, last

    oshape = (M, N) if stack == 1 else (stack, M, N)
    kw = dict(in_specs=specs, out_specs=[o_spec] * nout, out_shape=[jax.ShapeDtypeStruct(oshape, dt) for dt in out_dtypes],
              scratch_shapes=[pltpu.VMEM((tm, tn), F32)])
    body, kw, args = _attach(comm, body, kw, args, first_last)
    sem = ("arbitrary",) * 3 if comm is not None else ("parallel", "parallel", "arbitrary")
    res = pl.pallas_call(body, name=name, grid=grid, compiler_params=_params(sem), **kw)(*args)
    main = res[0] if nout == 1 else res[:nout]
    return main if comm is None else (main, res[nout:])


def _halo(K):
    return SUB * -(-(K - 1) // SUB)


def _conv_cols(K, C, col0):
    return _pick(math.gcd(C, col0), 512 if K <= SUB else 256, LANE)


def _conv_fwd(name, x, col0, w, bias, *, ts, tpe, comm=None):
    T = x.shape[0]
    K, C = w.shape
    H = _halo(K)
    cb = _conv_cols(K, C, col0)
    rb = _pick(ts, 64, SUB)
    off = col0 // cb
    specs = [pl.BlockSpec((ts, cb), lambda j, i: (i, off + j)),
             pl.BlockSpec((H, cb), lambda j, i: (jnp.maximum(i * (ts // H) - 1, 0), off + j)),
             pl.BlockSpec((K, cb), lambda j, i: (0, j))]
    args = [x, x, w]
    if bias is not None:
        specs.append(pl.BlockSpec((1, cb), lambda j, i: (0, j)))
        args.append(bias)

    def body(*refs):
        cur, halo, w_ref = refs[:3]
        o_ref, xp = refs[-2], refs[-1]
        i = pl.program_id(1)
        xp[0:H, :] = jnp.where(i % tpe == 0, 0.0, halo[...].astype(F32))
        xp[H:H + ts, :] = cur[...].astype(F32)
        for r0 in range(0, ts, rb):
            acc = jnp.zeros((rb, cb), F32) if bias is None else jnp.zeros((rb, cb), F32) + refs[3][...]
            for j in range(K):
                lo = H - (K - 1) + j + r0
                acc = acc + w_ref[j:j + 1, :] * xp[lo:lo + rb, :]
            o_ref[r0:r0 + rb, :] = acc

    nc, nt = C // cb, T // ts
    kw = dict(in_specs=specs, out_specs=[pl.BlockSpec((ts, cb), lambda j, i: (i, j))],
              out_shape=[jax.ShapeDtypeStruct((T, C), F32)], scratch_shapes=[pltpu.VMEM((H + ts, cb), F32)])

    def first_last():
        j, i = pl.program_id(0), pl.program_id(1)
        return jnp.logical_and(j == 0, i == 0), jnp.logical_and(j == nc - 1, i == nt - 1)

    body, kw, args = _attach(comm, body, kw, args, first_last)
    res = pl.pallas_call(body, name=name, grid=(nc, nt), compiler_params=_params(("arbitrary", "arbitrary")), **kw)(*args)
    return res[0] if comm is None else (res[0], res[1:])


def _conv_bwd(name, x, col0, dy, w, out_dtype, *, ts, tpe, comm=None):
    T = x.shape[0]
    K, C = w.shape
    H = _halo(K)
    cb = _conv_cols(K, C, col0)
    rb = _pick(ts, 64, SUB)
    off = col0 // cb
    nt = T // ts

    def body(cur, halo, dyc, dyn, w_ref, dx_ref, dw_ref, db_ref, xp, dyp):
        i = pl.program_id(1)
        xp[0:H, :] = jnp.where(i % tpe == 0, 0.0, halo[...].astype(F32))
        xp[H:H + ts, :] = cur[...].astype(F32)
        dyp[0:ts, :] = dyc[...]
        dyp[ts:ts + H, :] = jnp.where(i % tpe == tpe - 1, 0.0, dyn[...])
        for r0 in range(0, ts, rb):
            acc = jnp.zeros((rb, cb), F32)
            for j in range(K):
                lo = K - 1 - j + r0
                acc = acc + w_ref[j:j + 1, :] * dyp[lo:lo + rb, :]
            dx_ref[r0:r0 + rb, :] = acc.astype(out_dtype)

        @pl.when(i == 0)
        def _():
            dw_ref[...] = jnp.zeros_like(dw_ref)
            db_ref[...] = jnp.zeros_like(db_ref)

        for j in range(K):
            part = jnp.zeros((1, cb), F32)
            for r0 in range(0, ts, rb):
                lo = H - (K - 1) + j + r0
                part = part + jnp.sum(dyp[r0:r0 + rb, :] * xp[lo:lo + rb, :], axis=0, keepdims=True)
            dw_ref[j:j + 1, :] += part
        db_ref[...] += jnp.sum(dyc[...], axis=0, keepdims=True)

    nc = C // cb
    kw = dict(in_specs=[pl.BlockSpec((ts, cb), lambda j, i: (i, off + j)),
                        pl.BlockSpec((H, cb), lambda j, i: (jnp.maximum(i * (ts // H) - 1, 0), off + j)),
                        pl.BlockSpec((ts, cb), lambda j, i: (i, j)),
                        pl.BlockSpec((H, cb), lambda j, i: (jnp.minimum((i + 1) * (ts // H), T // H - 1), j)),
                        pl.BlockSpec((K, cb), lambda j, i: (0, j))],
              out_specs=[pl.BlockSpec((ts, cb), lambda j, i: (i, j)),
                         pl.BlockSpec((K, cb), lambda j, i: (0, j)),
                         pl.BlockSpec((1, cb), lambda j, i: (0, j))],
              out_shape=[jax.ShapeDtypeStruct((T, C), out_dtype), jax.ShapeDtypeStruct((K, C), F32),
                         jax.ShapeDtypeStruct((1, C), F32)],
              scratch_shapes=[pltpu.VMEM((H + ts, cb), F32), pltpu.VMEM((ts + H, cb), F32)])

    def first_last():
        j, i = pl.program_id(0), pl.program_id(1)
        return jnp.logical_and(j == 0, i == 0), jnp.logical_and(j == nc - 1, i == nt - 1)

    body, kw, args = _attach(comm, body, kw, [x, x, dy, dy, w], first_last)
    res = pl.pallas_call(body, name=name, grid=(nc, nt), compiler_params=_params(("arbitrary", "arbitrary")), **kw)(*args)
    return res if comm is None else (res[:3], res[3:])


def _hdot(a, b, dn):
    return lax.dot_general(a, b, (dn, ((), ())), precision=HI, preferred_element_type=F32)


def _bdot(a, b, dn):
    return lax.dot_general(a.astype(MXU_DTYPE), b.astype(MXU_DTYPE), (dn, ((), ())), preferred_element_type=F32)


def _split(a):
    hi = a.astype(MXU_DTYPE)
    return hi, (a - hi.astype(F32)).astype(MXU_DTYPE)


def _dot3_raw(a, b, dn):
    if MXU_DTYPE == F32:
        return _hdot(a, b, dn)
    (ah, al), (bh, bl) = _split(a), _split(b)
    d = lambda p, q: lax.dot_general(p, q, (dn, ((), ())), preferred_element_type=F32)
    return d(ah, bh) + (d(ah, bl) + d(al, bh))


def _with_vjp(raw):
    dn = {"nn": NN, "nt": NT, "tn": TN}

    @functools.partial(jax.custom_vjp, nondiff_argnums=(2,))
    def dot(a, b, mode):
        return raw(a, b, dn[mode])

    def fwd(a, b, mode):
        return raw(a, b, dn[mode]), (a, b)

    def bwd(mode, res, ct):
        a, b = res
        if mode == "nn":
            return raw(ct, b, NT), raw(a, ct, TN)
        if mode == "nt":
            return raw(ct, b, NN), raw(ct, a, TN)
        return raw(b, ct, NT), raw(a, ct, NN)

    dot.defvjp(fwd, bwd)
    return dot


_dot_exact = _with_vjp(_hdot)
_dot3 = _with_vjp(_dot3_raw)
_dot1 = _with_vjp(_bdot)


@jax.custom_vjp
def _tri_inv(ns):
    C = ns[0].shape[0]
    eye = (lax.broadcasted_iota(jnp.int32, (C, C), 0) == lax.broadcasted_iota(jnp.int32, (C, C), 1)).astype(F32)
    ts = [eye + n for n in ns]
    ps = list(ns)
    for _ in range(int(math.log2(CHUNK)) - 1):
        ps = [_dot3_raw(p, p, NN) for p in ps]
        ts = [t + _dot3_raw(t, p, NN) for t, p in zip(ts, ps)]
    return tuple(ts)


def _tri_inv_fwd(ns):
    ts = _tri_inv(ns)
    return ts, ts


def _tri_inv_bwd(ts, cts):
    xs = [_dot3_raw(t, ct, TN) for t, ct in zip(ts, cts)]
    return (tuple(_dot3_raw(x, t, NT) for x, t in zip(xs, ts)),)


_tri_inv.defvjp(_tri_inv_fwd, _tri_inv_bwd)


def _dn_prep(qs, ks, vs, ba, alog, dtb, *, nh):
    C2, Dk = qs[0].shape
    z = ba + dtb
    g_all = -jnp.exp(alog) * (jnp.maximum(z, 0.0) + jnp.log(1.0 + jnp.exp(-jnp.abs(z))))
    ri = lax.broadcasted_iota(jnp.int32, (C2, C2), 0)
    cj = lax.broadcasted_iota(jnp.int32, (C2, C2), 1)
    same = jnp.logical_not(jnp.logical_xor(ri >= CHUNK, cj >= CHUNK))
    causal, strict = jnp.logical_and(ri >= cj, same), jnp.logical_and(ri > cj, same)
    gc_all = _dot_exact(causal.astype(F32), g_all, "nn")
    gc_rows = _dot_exact(g_all, jnp.logical_and(ri <= cj, same).astype(F32), "tn")
    gl_all = _dot_exact(same.astype(F32), g_all, "nn")
    lane = lax.broadcasted_iota(jnp.int32, (1, ba.shape[1]), 1)
    subl = lax.broadcasted_iota(jnp.int32, (ba.shape[1], 1), 0)
    heads = range(nh)
    sel = [(lane == nh + h).astype(F32) for h in heads]
    gc = [jnp.sum(gc_all * s, axis=1, keepdims=True) for s in sel]
    gl = [jnp.sum(gl_all * s, axis=1, keepdims=True) for s in sel]
    gcr = [jnp.sum(gc_rows * (subl == nh + h).astype(F32), axis=0, keepdims=True) for h in heads]
    decay = [jnp.where(causal, jnp.exp(jnp.where(causal, a - b, 0.0)), 0.0) for a, b in zip(gc, gcr)]
    beta = [_sigmoid(jnp.sum(ba * (lane == h).astype(F32), axis=1, keepdims=True)) for h in heads]
    q = [_silu(t) for t in qs]
    q = [t * lax.rsqrt(jnp.sum(t * t, axis=-1, keepdims=True) + EPS) * (Dk ** -0.5) for t in q]
    k = [_silu(t) for t in ks]
    k = [t * lax.rsqrt(jnp.sum(t * t, axis=-1, keepdims=True) + EPS) for t in k]
    kb = [a * b for a, b in zip(k, beta)]
    vb = [_silu(a) * b for a, b in zip(vs, beta)]
    kk = [_dot1(a, b, "nt") for a, b in zip(kb, k)]
    qk = [_dot1(a, b, "nt") for a, b in zip(q, k)]
    t = _tri_inv(tuple(-jnp.where(strict, a * d, 0.0) for a, d in zip(kk, decay)))
    egc = [jnp.exp(a) for a in gc]
    u = [_dot3(a, b, "nn") for a, b in zip(t, vb)]
    w = [_dot3(a, b * e, "nn") for a, b, e in zip(t, kb, egc)]
    attn = [jnp.where(causal, a * d, 0.0) for a, d in zip(qk, decay)]
    qd = [a * e for a, e in zip(q, egc)]
    kd = [a * jnp.exp(b - c) for a, b, c in zip(k, gl, gc)]
    glb = [jnp.exp(a) * jnp.ones((1, Dk), F32) for a in gl]
    return tuple(u), tuple(w), tuple(qd), tuple(kd), tuple(attn), tuple(glb)


def _dn_prep_specs(DN, col_q):
    qs = [pl.BlockSpec((PAIR, DN), functools.partial(lambda i, j: (i, j), j=j)) for j in range(3)]
    return qs + [pl.BlockSpec((PAIR, LANE), lambda i: (i, col_q)), pl.BlockSpec((1, LANE), lambda i: (0, 0)),
                 pl.BlockSpec((1, LANE), lambda i: (0, 0))]


def _heads(ref, NH, Dk):
    return tuple(ref[:, h * Dk:(h + 1) * Dk] for h in range(NH))


def _dn_prep_fwd(name, qkv, proj, ba_col, alog, dtb, *, NH, comm=None):
    T = qkv.shape[0]
    DN = qkv.shape[1] // 3
    Dk = DN // NH
    nsteps = T // PAIR

    def body(q_ref, k_ref, v_ref, ba_ref, al_ref, dt_ref, u_ref, w_ref, qd_ref, kd_ref, gl_ref, at_ref):
        res = _dn_prep(_heads(q_ref, NH, Dk), _heads(k_ref, NH, Dk), _heads(v_ref, NH, Dk), ba_ref[...], al_ref[...],
                       dt_ref[...], nh=NH)
        for h in range(NH):
            sl = slice(h * Dk, (h + 1) * Dk)
            u_ref[:, sl], w_ref[:, sl], qd_ref[:, sl], kd_ref[:, sl] = res[0][h], res[1][h], res[2][h], res[3][h]
            at_ref[h] = res[4][h]
            gl_ref[:, sl] = res[5][h]

    big = pl.BlockSpec((PAIR, DN), lambda i: (i, 0))
    kw = dict(in_specs=_dn_prep_specs(DN, ba_col // LANE),
              out_specs=[big] * 5 + [pl.BlockSpec((NH, PAIR, PAIR), lambda i: (0, i, 0))],
              out_shape=[jax.ShapeDtypeStruct((T, DN), F32)] * 5 + [jax.ShapeDtypeStruct((NH, T, PAIR), F32)],
              scratch_shapes=[])
    body, kw, args = _attach(comm, body, kw, [qkv, qkv, qkv, proj, alog, dtb],
                             lambda: (pl.program_id(0) == 0, pl.program_id(0) == nsteps - 1))
    res = pl.pallas_call(body, name=name, grid=(nsteps,), compiler_params=_params(("arbitrary",)), **kw)(*args)
    return res[:6], res[6:]


def _dn_prep_bwd(name, qkv, proj, ba_col, alog, dtb, cts, out_dtype, *, NH, comm=None):
    T = qkv.shape[0]
    DN = qkv.shape[1] // 3
    Dk = DN // NH
    nsteps = T // PAIR

    def body(q_ref, k_ref, v_ref, ba_ref, al_ref, dt_ref, du, dw, dqd, dkd, dgl, dat,
             dq_ref, dk_ref, dv_ref, dba_ref, dal_ref, ddt_ref):
        i = pl.program_id(0)
        _, pull = jax.vjp(functools.partial(_dn_prep, nh=NH), _heads(q_ref, NH, Dk), _heads(k_ref, NH, Dk),
                          _heads(v_ref, NH, Dk), ba_ref[...], al_ref[...], dt_ref[...])
        gq, gk, gv, gba, gal, gdt = pull((_heads(du, NH, Dk), _heads(dw, NH, Dk), _heads(dqd, NH, Dk),
                                          _heads(dkd, NH, Dk), tuple(dat[h] for h in range(NH)), _heads(dgl, NH, Dk)))
        for h in range(NH):
            sl = slice(h * Dk, (h + 1) * Dk)
            dq_ref[:, sl], dk_ref[:, sl], dv_ref[:, sl] = gq[h].astype(out_dtype), gk[h].astype(out_dtype), gv[h].astype(out_dtype)
        dba_ref[...] = gba.astype(out_dtype)

        @pl.when(i == 0)
        def _():
            dal_ref[...] = jnp.zeros_like(dal_ref)
            ddt_ref[...] = jnp.zeros_like(ddt_ref)

        dal_ref[...] += gal
        ddt_ref[...] += gdt

    big = pl.BlockSpec((PAIR, DN), lambda i: (i, 0))
    row = pl.BlockSpec((1, LANE), lambda i: (0, 0))
    kw = dict(in_specs=_dn_prep_specs(DN, ba_col // LANE) + [big] * 5 + [pl.BlockSpec((NH, PAIR, PAIR), lambda i: (0, i, 0))],
              out_specs=[big] * 3 + [pl.BlockSpec((PAIR, LANE), lambda i: (i, 0)), row, row],
              out_shape=[jax.ShapeDtypeStruct((T, DN), out_dtype)] * 3 + [jax.ShapeDtypeStruct((T, LANE), out_dtype),
                                                                           jax.ShapeDtypeStruct((1, LANE), F32),
                                                                           jax.ShapeDtypeStruct((1, LANE), F32)],
              scratch_shapes=[])
    body, kw, args = _attach(comm, body, kw, [qkv, qkv, qkv, proj, alog, dtb, *cts],
                             lambda: (pl.program_id(0) == 0, pl.program_id(0) == nsteps - 1))
    res = pl.pallas_call(body, name=name, grid=(nsteps,), compiler_params=_params(("arbitrary",)), **kw)(*args)
    return res[:6], res[6:]


def _dn_scan_fwd(name, u, w, qd, kd, gl, attn, *, NH, B):
    T, DN = u.shape
    Dk = DN // NH
    C = CHUNK
    S = T // B
    NP = S // PAIR

    def body(u_ref, w_ref, qd_ref, kd_ref, gl_ref, at_ref, o_ref, st_ref, s_ref):
        @pl.when(pl.program_id(0) == 0)
        def _():
            s_ref[...] = jnp.zeros_like(s_ref)

        zeros = jnp.zeros((C, Dk), F32)
        for sub in range(2):
            rs = slice(sub * C, (sub + 1) * C)
            for h in range(NH):
                sl = slice(h * Dk, (h + 1) * Dk)
                for b in range(B):
                    s = s_ref[b, h]
                    st_ref[b, sub, h] = s
                    vnew = u_ref[b, rs, sl] - _bdot(w_ref[b, rs, sl], s, NN)
                    vext = jnp.concatenate([vnew, zeros] if sub == 0 else [zeros, vnew], axis=0)
                    o_ref[b, rs, sl] = _bdot(qd_ref[b, rs, sl], s, NN) + _bdot(at_ref[h, b, rs, :], vext, NN)
                    s_ref[b, h] = s * gl_ref[b, sub * C:sub * C + 1, sl] + _bdot(kd_ref[b, rs, sl], vnew, TN)

    big = pl.BlockSpec((B, PAIR, DN), lambda n: (0, n, 0))
    o, st = pl.pallas_call(
        body, name=name, grid=(NP,),
        in_specs=[big] * 5 + [pl.BlockSpec((NH, B, PAIR, PAIR), lambda n: (0, 0, n, 0))],
        out_specs=[big, pl.BlockSpec((B, 2, NH, Dk, Dk), lambda n: (0, n, 0, 0, 0))],
        out_shape=[jax.ShapeDtypeStruct((B, S, DN), F32), jax.ShapeDtypeStruct((B, S // C, NH, Dk, Dk), F32)],
        scratch_shapes=[pltpu.VMEM((B, NH, Dk, Dk), F32)],
        compiler_params=_params(("arbitrary",)),
    )(*[t.reshape(B, S, DN) for t in (u, w, qd, kd, gl)], attn.reshape(NH, B, S, PAIR))
    return o.reshape(T, DN), st.reshape(T // C, NH, Dk, Dk)


def _dn_scan_bwd(name, do, u, w, qd, kd, gl, attn, st, *, NH, B, comm=None):
    T, DN = u.shape
    Dk = DN // NH
    C = CHUNK
    S = T // B
    NP = S // PAIR

    def body(do_ref, u_ref, w_ref, qd_ref, kd_ref, gl_ref, at_ref, st_ref,
             du_ref, dw_ref, dqd_ref, dkd_ref, dgl_ref, dat_ref, ds_ref):
        @pl.when(pl.program_id(0) == 0)
        def _():
            ds_ref[...] = jnp.zeros_like(ds_ref)

        row0 = lax.broadcasted_iota(jnp.int32, (C, Dk), 0) == 0
        zeros = jnp.zeros((C, Dk), F32)
        for sub in (1, 0):
            rs = slice(sub * C, (sub + 1) * C)
            for h in range(NH):
                sl = slice(h * Dk, (h + 1) * Dk)
                for b in range(B):
                    s, ds, g = st_ref[b, sub, h], ds_ref[b, h], do_ref[b, rs, sl]
                    wv, at, kdv = w_ref[b, rs, sl], at_ref[h, b, rs, :], kd_ref[b, rs, sl]
                    vnew = u_ref[b, rs, sl] - _bdot(wv, s, NN)
                    vext = jnp.concatenate([vnew, zeros] if sub == 0 else [zeros, vnew], axis=0)
                    dvnew = _bdot(at, g, TN)[rs] + _bdot(kdv, ds, NN)
                    dat_ref[h, b, rs, :] = _bdot(g, vext, NT)
                    dqd_ref[b, rs, sl] = _bdot(g, s, NT)
                    dkd_ref[b, rs, sl] = _bdot(vnew, ds, NT)
                    dgl_ref[b, rs, sl] = jnp.where(row0, jnp.sum(s * ds, axis=0, keepdims=True), 0.0)
                    du_ref[b, rs, sl] = dvnew
                    dw_ref[b, rs, sl] = -_bdot(dvnew, s, NT)
                    ds_ref[b, h] = (_bdot(qd_ref[b, rs, sl], g, TN) + ds * gl_ref[b, sub * C:sub * C + 1, sl]
                                    - _bdot(wv, dvnew, TN))

    big = pl.BlockSpec((B, PAIR, DN), lambda n: (0, NP - 1 - n, 0))
    att = pl.BlockSpec((NH, B, PAIR, PAIR), lambda n: (0, 0, NP - 1 - n, 0))
    kw = dict(in_specs=[big] * 6 + [att, pl.BlockSpec((B, 2, NH, Dk, Dk), lambda n: (0, NP - 1 - n, 0, 0, 0))],
              out_specs=[big] * 5 + [att],
              out_shape=[jax.ShapeDtypeStruct((B, S, DN), F32)] * 5 + [jax.ShapeDtypeStruct((NH, B, S, PAIR), F32)],
              scratch_shapes=[pltpu.VMEM((B, NH, Dk, Dk), F32)])
    args = [t.reshape(B, S, DN) for t in (do, u, w, qd, kd, gl)] + [attn.reshape(NH, B, S, PAIR),
                                                                   st.reshape(B, S // C, NH, Dk, Dk)]
    body, kw, args = _attach(comm, body, kw, args, lambda: (pl.program_id(0) == 0, pl.program_id(0) == NP - 1))
    res = pl.pallas_call(body, name=name, grid=(NP,), compiler_params=_params(("arbitrary",)), **kw)(*args)
    return [t.reshape(T, DN) for t in res[:5]] + [res[5].reshape(NH, T, PAIR)], res[6:]


def _sum0(name, a):
    n, r, ln = a.shape
    tr = _pick(r, 64, SUB)

    def body(a_ref, o_ref):
        acc = a_ref[0]
        for k in range(1, n):
            acc = acc + a_ref[k]
        o_ref[...] = acc

    return pl.pallas_call(
        body, name=name, grid=(r // tr,),
        in_specs=[pl.BlockSpec((n, tr, ln), lambda i: (0, i, 0))],
        out_specs=pl.BlockSpec((tr, ln), lambda i: (i, 0)),
        out_shape=jax.ShapeDtypeStruct((r, ln), F32),
        compiler_params=_params(("arbitrary",)),
    )(a)


def _adamw(name, w, g, m, v):
    Lw, R, Cc = w.shape
    tr = _pick(R, max(SUB, (1 << 18) // Cc // SUB * SUB), SUB)
    c1 = 1.0 - ADAM_B1 ** ADAM_STEP
    c2 = 1.0 - ADAM_B2 ** ADAM_STEP

    def body(w_ref, g_ref, m_ref, v_ref, d_ref, mo_ref, vo_ref):
        gv = g_ref[...]
        mn = ADAM_B1 * m_ref[...] + (1.0 - ADAM_B1) * gv
        vn = ADAM_B2 * v_ref[...] + (1.0 - ADAM_B2) * (gv * gv)
        mo_ref[...] = mn
        vo_ref[...] = vn
        d_ref[...] = -ADAM_LR * ((mn / c1) / (jnp.sqrt(vn / c2) + ADAM_EPS) + ADAM_WD * w_ref[...])

    spec = pl.BlockSpec((1, tr, Cc), lambda l, i: (l, i, 0))
    return pl.pallas_call(
        body, name=name, grid=(Lw, R // tr), in_specs=[spec] * 4, out_specs=[spec] * 3,
        out_shape=[jax.ShapeDtypeStruct((Lw, R, Cc), F32)] * 3,
        compiler_params=_params(("arbitrary", "arbitrary")),
    )(w, g, m, v)


def _pack(arrs):
    flat = jnp.concatenate([a.reshape(-1).astype(F32) for a in arrs])
    pad = (-flat.shape[0]) % (SUB * PACK_LANES)
    return jnp.pad(flat, (0, pad)).reshape(-1, PACK_LANES)


def _unpack(flat, shapes):
    out, pos = [], 0
    for shp in shapes:
        n = math.prod(shp)
        out.append(flat[..., pos:pos + n].reshape(flat.shape[:-1] + tuple(shp)))
        pos += n
    return out


def _remote(src, dst, ssem, rsem, dev):
    return pltpu.make_async_remote_copy(src_ref=src, dst_ref=dst, send_sem=ssem, recv_sem=rsem, device_id=dev,
                                        device_id_type=MESH)


def _place():
    return lax.axis_index("x"), lax.axis_index("y"), lax.axis_index("c")


def _all_gather8(name, a):
    m, n = a.shape

    def body(x_ref, out_ref, send_sems, recv_sems, local_sem):
        x, y, c = _place()
        me, sibling = (x, y, c), (x, y, 1 - c)
        chips = [(1 - x, y), (x, 1 - y), (1 - x, 1 - y)]

        def rows(px, py, pc):
            return out_ref.at[pl.ds((4 * px + 2 * py + pc) * m, m), :]

        def copy(k, block, to, src=None):
            return _remote(rows(*block) if src is None else src, rows(*block), send_sems.at[k], recv_sems.at[k], to)

        mine = pltpu.make_async_copy(x_ref, rows(*me), local_sem)
        mine.start()
        first = [copy(0, me, sibling, src=x_ref)]
        first += [copy(1 + j, me, (*chip, c), src=x_ref) for j, chip in enumerate(chips)]
        for cp in first:
            cp.start()
        passed = [copy(4 + j, (*chip, c), sibling) for j, chip in enumerate(chips)]
        for j, chip in enumerate(chips):
            copy(1 + j, (*chip, c), me).wait_recv()
            passed[j].start()
        copy(0, sibling, me).wait_recv()
        for j, chip in enumerate(chips):
            copy(4 + j, (*chip, 1 - c), me).wait_recv()
        for cp in first + passed:
            cp.wait_send()
        mine.wait()

    return pl.pallas_call(
        body, name=name,
        out_shape=jax.ShapeDtypeStruct((8 * m, n), a.dtype),
        in_specs=[pl.BlockSpec(memory_space=pltpu.VMEM)],
        out_specs=pl.BlockSpec(memory_space=pltpu.VMEM),
        scratch_shapes=[pltpu.SemaphoreType.DMA((7,)), pltpu.SemaphoreType.DMA((7,)), pltpu.SemaphoreType.DMA],
        compiler_params=pltpu.CompilerParams(vmem_limit_bytes=VMEM_LIMIT),
    )(a)


def _chip_peers(x, y):
    return [(1 - x, y), (x, 1 - y), (1 - x, 1 - y)]


def _sds(a):
    return jax.ShapeDtypeStruct(a.shape, a.dtype)


def _cast_into_slot(name, idx, a, layer, dtype):
    _, R, Cc = a.shape
    tr = _pick(R, 512, 16)

    def body(i_ref, a_ref, o_ref):
        o_ref[0] = a_ref[0].astype(dtype)

    return pl.pallas_call(
        body, name=name,
        grid_spec=pltpu.PrefetchScalarGridSpec(
            num_scalar_prefetch=1, grid=(R // tr,),
            in_specs=[pl.BlockSpec((1, tr, Cc), lambda i, ix: (layer, i, 0))],
            out_specs=pl.BlockSpec((1, tr, Cc), lambda i, ix: (ix[0], i, 0))),
        out_shape=jax.ShapeDtypeStruct((4, R, Cc), dtype),
        compiler_params=_params(("arbitrary",)),
    )(idx, a)


def _gather_ici(bufs):
    n = len(bufs)

    def plan(ins, outs, ssem, rsem, starting):
        x, y, c = _place()
        sends, recvs = [], []
        for a in range(n):
            for k, (px, py) in enumerate(_chip_peers(x, y)):
                mine, got = outs[a].at[2 * x + y, c], outs[a].at[2 * px + py, c]
                sends.append(_remote(mine, mine, ssem.at[3 * a + k], rsem.at[3 * a + k], (px, py, c)))
                if not starting:
                    recvs.append(_remote(got, got, ssem.at[3 * a + k], rsem.at[3 * a + k], (px, py, c)))
        return sends if starting else (sends, recvs)

    return _Comm(list(bufs), [_sds(b) for b in bufs], {a: a for a in range(n)}, 3 * n, plan)


def _gather_d2d(bufs):
    n = len(bufs)

    def plan(ins, outs, ssem, rsem, starting):
        x, y, c = _place()
        sends, recvs = [], []
        for a in range(n):
            for k, (px, py) in enumerate(_chip_peers(x, y)):
                got, other = outs[a].at[2 * px + py, c], outs[a].at[2 * px + py, 1 - c]
                sends.append(_remote(got, got, ssem.at[3 * a + k], rsem.at[3 * a + k], (x, y, 1 - c)))
                if not starting:
                    recvs.append(_remote(other, other, ssem.at[3 * a + k], rsem.at[3 * a + k], (x, y, 1 - c)))
        return sends if starting else (sends, recvs)

    return _Comm(list(bufs), [_sds(b) for b in bufs], {a: a for a in range(n)}, 3 * n, plan)


def _swap_halves(grads):
    n = len(grads)

    def plan(ins, outs, ssem, rsem, starting):
        x, y, c = _place()
        cps = [_remote(ins[a].at[:, 1 - c], outs[a], ssem.at[a], rsem.at[a], (x, y, 1 - c)) for a in range(n)]
        return cps if starting else (cps, cps)

    return _Comm(list(grads), [jax.ShapeDtypeStruct((4,) + g.shape[2:], g.dtype) for g in grads], {}, n, plan)


def _scatter_chips(halves):
    n = len(halves)

    def plan(ins, outs, ssem, rsem, starting):
        x, y, c = _place()
        cps = []
        for a in range(n):
            for k, (px, py) in enumerate(_chip_peers(x, y)):
                cps.append(_remote(ins[a].at[2 * px + py], outs[a].at[k], ssem.at[3 * a + k], rsem.at[3 * a + k], (px, py, c)))
        return cps if starting else (cps, cps)

    return _Comm(list(halves), [jax.ShapeDtypeStruct((3,) + h.shape[1:], h.dtype) for h in halves], {}, 3 * n, plan)


def _join_halves(parts, layer):
    n = len(parts)

    def plan(ins, outs, ssem, rsem, starting):
        x, y, c = _place()
        sends = [_remote(outs[a].at[layer, c], outs[a].at[layer, c], ssem.at[a], rsem.at[a], (x, y, 1 - c)) for a in range(n)]
        if starting:
            return sends
        return sends, [_remote(outs[a].at[layer, 1 - c], outs[a].at[layer, 1 - c], ssem.at[a], rsem.at[a], (x, y, 1 - c))
                       for a in range(n)]

    return _Comm(list(parts), [_sds(p) for p in parts], {a: a for a in range(n)}, n, plan)


def _add_half(name, idx, g, r):
    _, _, Rh, Cc = g.shape
    tr = _pick(Rh, 512, 16)

    def body(i_ref, g_ref, r_ref, o_ref):
        o_ref[...] = (g_ref[0].astype(F32) + r_ref[...].astype(F32)).astype(o_ref.dtype)

    return pl.pallas_call(
        body, name=name,
        grid_spec=pltpu.PrefetchScalarGridSpec(
            num_scalar_prefetch=1, grid=(4, Rh // tr),
            in_specs=[pl.BlockSpec((1, 1, tr, Cc), lambda s, i, ix: (s, ix[1], i, 0)),
                      pl.BlockSpec((1, tr, Cc), lambda s, i, ix: (s, i, 0))],
            out_specs=pl.BlockSpec((1, tr, Cc), lambda s, i, ix: (s, i, 0))),
        out_shape=jax.ShapeDtypeStruct((4, Rh, Cc), g.dtype),
        compiler_params=_params(("arbitrary", "arbitrary")),
    )(idx, g, r)


def _add_chips(name, idx, h, r, layer, nlayers, acc):
    _, Rh, Cc = h.shape
    tr = _pick(Rh, 256, 16)

    def body(i_ref, h_ref, r0, r1, r2, *rest):
        rest[-1][0, 0] = ((h_ref[0].astype(F32) + r0[0].astype(F32)) + r1[0].astype(F32)) + r2[0].astype(F32)

    taken = [] if acc is None else [acc]
    return pl.pallas_call(
        body, name=name,
        grid_spec=pltpu.PrefetchScalarGridSpec(
            num_scalar_prefetch=1, grid=(Rh // tr,),
            in_specs=[pl.BlockSpec((1, tr, Cc), lambda i, ix: (ix[0], i, 0))]
            + [pl.BlockSpec((1, tr, Cc), functools.partial(lambda i, ix, k: (k, i, 0), k=k)) for k in range(3)]
            + [ANY] * len(taken),
            out_specs=pl.BlockSpec((1, 1, tr, Cc), lambda i, ix: (layer, ix[1], i, 0))),
        out_shape=jax.ShapeDtypeStruct((nlayers, 2, Rh, Cc), F32),
        input_output_aliases={5: 0} if taken else {},
        compiler_params=_params(("arbitrary",)),
    )(idx, h, r, r, r, *taken)


def kernel(x, c, w_ada, b_ada, norm1_w, w_in, conv_dw_w, conv_dw_b, conv_ln_w, conv_ln_b, w_pw2, conv_out_norm_w, qkv_conv_w, a_log, dt_bias, dn_norm_w, w_out, norm2_w, w_up, w_down, final_ada_w, final_ada_b, final_norm_w, loss_target, m_w_ada, m_b_ada, m_norm1_w, m_w_in, m_conv_dw_w, m_conv_dw_b, m_conv_ln_w, m_conv_ln_b, m_w_pw2, m_conv_out_norm_w, m_qkv_conv_w, m_a_log, m_dt_bias, m_dn_norm_w, m_w_out, m_norm2_w, m_w_up, m_w_down, m_final_ada_w, m_final_ada_b, m_final_norm_w, v_w_ada, v_b_ada, v_norm1_w, v_w_in, v_conv_dw_w, v_conv_dw_b, v_conv_ln_w, v_conv_ln_b, v_w_pw2, v_conv_out_norm_w, v_qkv_conv_w, v_a_log, v_dt_bias, v_dn_norm_w, v_w_out, v_norm2_w, v_w_up, v_w_down, v_final_ada_w, v_final_ada_b, v_final_norm_w):
    names = ["w_ada", "b_ada", "norm1_w", "w_in", "conv_dw_w", "conv_dw_b", "conv_ln_w", "conv_ln_b", "w_pw2",
             "conv_out_norm_w", "qkv_conv_w", "a_log", "dt_bias", "dn_norm_w", "w_out", "norm2_w", "w_up", "w_down",
             "final_ada_w", "final_ada_b", "final_norm_w"]
    weights = dict(zip(names, [w_ada, b_ada, norm1_w, w_in, conv_dw_w, conv_dw_b, conv_ln_w, conv_ln_b, w_pw2,
                               conv_out_norm_w, qkv_conv_w, a_log, dt_bias, dn_norm_w, w_out, norm2_w, w_up, w_down,
                               final_ada_w, final_ada_b, final_norm_w]))
    mom1 = dict(zip(names, [m_w_ada, m_b_ada, m_norm1_w, m_w_in, m_conv_dw_w, m_conv_dw_b, m_conv_ln_w, m_conv_ln_b,
                            m_w_pw2, m_conv_out_norm_w, m_qkv_conv_w, m_a_log, m_dt_bias, m_dn_norm_w, m_w_out,
                            m_norm2_w, m_w_up, m_w_down, m_final_ada_w, m_final_ada_b, m_final_norm_w]))
    mom2 = dict(zip(names, [v_w_ada, v_b_ada, v_norm1_w, v_w_in, v_conv_dw_w, v_conv_dw_b, v_conv_ln_w, v_conv_ln_b,
                            v_w_pw2, v_conv_out_norm_w, v_qkv_conv_w, v_a_log, v_dt_bias, v_dn_norm_w, v_w_out,
                            v_norm2_w, v_w_up, v_w_down, v_final_ada_w, v_final_ada_b, v_final_norm_w]))

    B, S, D = x.shape
    T = B * S
    L = w_in.shape[0]
    C1 = conv_ln_w.shape[-1]
    NH, DH = a_log.shape[-1], dn_norm_w.shape[-1]
    DN = NH * DH
    FF = w_down.shape[1] * 4
    IN = w_in.shape[-1] * 4
    INP = 6 * C1 + LANE
    KC, KQ = conv_dw_w.shape[1], qkv_conv_w.shape[1]
    NW, NF = w_ada.shape[-1], final_ada_w.shape[-1]
    assert DH == LANE and DN == C1 and IN == 6 * C1 + 2 * NH and S % PAIR == 0
    xi, yi, ci = _place()
    s_me, me = 2 * xi + yi, 4 * xi + 2 * yi + ci
    idx = jnp.stack([s_me, ci]).astype(jnp.int32)
    tmf, tmb = _pick(S, 256, SUB), _pick(S, 128, SUB)
    ts, tsq = _pick(S, 256, _halo(KC)), _pick(S, 512, _halo(KQ))

    shapes1 = [(B, D), conv_dw_w.shape, qkv_conv_w.shape]
    g1 = _pack([c, conv_dw_w, qkv_conv_w])
    g1 = _all_gather8("gather_cond", g1).reshape(8, -1)
    c_all, cw_all, qw_all = _unpack(g1, shapes1)
    c_all = c_all.reshape(8 * B, D)
    conv_w_full = jnp.moveaxis(cw_all[0::2], 0, 2).reshape(L, KC, C1)
    qkv_w_full = jnp.moveaxis(qw_all[0::2], 0, 2).reshape(L, KQ, 3 * DN)
    (c_act,) = _rows_fwd("cond_silu", _silu_row, [_whole(c_all)], [], [], [(D, F32)], tm=8 * B)

    mods = []
    for l in range(L):
        bsh = lax.dynamic_slice(b_ada[l], (s_me * NW,), (NW,)).reshape(1, NW)
        mods.append(_mm(f"mod_{l}", c_act, w_ada, "nn", bias=bsh, b_layer=l))
    bsh = lax.dynamic_slice(final_ada_b, (s_me * NF,), (NF,)).reshape(1, NF)
    mods.append(_mm("mod_final", c_act, final_ada_w, "nn", bias=bsh))
    shapes2 = [(8 * B, NW)] * L + [(8 * B, NF)]
    g2 = _all_gather8("gather_mod", _pack(mods)).reshape(8, -1)[0::2]
    mod_all = [jnp.moveaxis(t, 0, 1).reshape(8 * B, -1) for t in _unpack(g2, shapes2)]
    mod_me = [lax.dynamic_slice(t, (B * me, 0), (B, t.shape[1])) for t in mod_all]

    def split_mod(t, n):
        return [t[:, k * D:(k + 1) * D].reshape(B, 1, D) for k in range(n)]

    def cast_weights(l):
        bufs = [_cast_into_slot(f"cast_{l}_{k}", idx, a, l, MXU_DTYPE) for k, a in enumerate([w_in, w_pw2, w_out, w_up, w_down])]
        return [b.reshape(4, 2, b.shape[1] // 2, b.shape[2]) for b in bufs]

    order = ["w_in", "w_pw2", "w_out", "w_up", "w_down"]
    to_natural = dict(
        w_in=lambda g: jnp.pad(jnp.moveaxis(g.reshape(4, D, IN // 4), 0, 1).reshape(D, IN), ((0, 0), (0, INP - IN))),
        w_pw2=lambda g: g.reshape(C1, C1), w_out=lambda g: g.reshape(D, D),
        w_up=lambda g: jnp.moveaxis(g.reshape(4, D, FF // 4), 0, 1).reshape(D, FF), w_down=lambda g: g.reshape(FF, D))

    def natural(got, keys=order):
        return {k: to_natural[k](g) for k, g in zip(keys, got)}

    first = cast_weights(0)
    late = first[3:]
    first = _run_comm("gather_w_ici_0", _gather_ici(first[:3]))
    wfull = {0: natural(_run_comm("gather_w_d2d_0", _gather_d2d(first)), order[:3])}

    pad_row = lambda v: jnp.pad(v.reshape(1, -1), ((0, 0), (NH, LANE - 2 * NH)))
    row = lambda v: v.reshape(1, -1)

    saved = []
    xcur = x.reshape(T, D)
    ycur = jnp.zeros((T, D), F32)
    gate_prev = jnp.zeros((B, 1, D), F32)
    for l in range(L):
        sh1, sc1, g1_, sh2, sc2, g2_ = split_mod(mod_me[l], 6)
        sv = dict(x_in=xcur, y_in=ycur, gate_in=gate_prev, sh1=sh1, sc1=sc1, g1=g1_, sh2=sh2, sc2=sc2, g2=g2_)
        junction1 = functools.partial(_rows_fwd, f"junction1_{l}", _junction, [_whole(xcur), _whole(ycur)],
                                      [gate_prev, sh1, sc1], [row(norm1_w[l])], [(D, F32), (D, MXU_DTYPE)],
                                      tm=tmf, tpe=S // tmf)
        if l == 0:
            x0, h1 = junction1()
        else:
            (x0, h1), got = junction1(comm=_gather_d2d(travelling))
            wfull[l] = natural(list(passed) + list(got))
        W = wfull[l]
        nxt = cast_weights(l + 1) if l + 1 < L else None
        if nxt is None:
            proj = _mm(f"proj_{l}", h1, W["w_in"], "nn", tn=896)
        else:
            proj, got_in = _mm(f"proj_{l}", h1, W["w_in"], "nn", tn=896, comm=_gather_ici(nxt[:1]))
        (u0,) = _rows_fwd(f"glu_{l}", _glu, [(proj, C1, 0), (proj, C1, 1)], [], [], [(C1, F32)], tm=tmf)
        if l > 0:
            u1 = _conv_fwd(f"conv_{l}", u0, 0, conv_w_full[l], row(conv_dw_b[l]), ts=ts, tpe=S // ts)
        else:
            u1, late_up = _conv_fwd(f"conv_{l}", u0, 0, conv_w_full[l], row(conv_dw_b[l]), ts=ts, tpe=S // ts,
                                    comm=_gather_ici(late[:1]))
        (u2,) = _rows_fwd(f"ln_silu_{l}", _ln_silu, [_whole(u1)], [], [row(conv_ln_w[l]), row(conv_ln_b[l])],
                          [(C1, MXU_DTYPE)], tm=tmf)
        u3 = _mm(f"pw2_{l}", u2, W["w_pw2"], "nn")
        (y_conv,) = _rows_fwd(f"conv_out_norm_{l}", _rms, [_whole(u3)], [], [row(conv_out_norm_w[l])],
                              [(C1, MXU_DTYPE)], tm=tmf)
        if l > 0:
            qkv = _conv_fwd(f"qkv_conv_{l}", proj, 2 * C1, qkv_w_full[l], None, ts=tsq, tpe=S // tsq)
        else:
            qkv, late_down = _conv_fwd(f"qkv_conv_{l}", proj, 2 * C1, qkv_w_full[l], None, ts=tsq, tpe=S // tsq,
                                       comm=_gather_ici(late[1:]))
        al, dtb = pad_row(a_log[l]), pad_row(dt_bias[l])
        (u, w, qd, kd, gl, attn), got_po = _dn_prep_fwd(f"dn_prep_{l}", qkv, proj, 6 * C1, al, dtb, NH=NH,
                                                        comm=None if nxt is None else _gather_ici(nxt[1:3]))
        o, st = _dn_scan_fwd(f"dn_scan_{l}", u, w, qd, kd, gl, attn, NH=NH, B=B)
        (y_dn,) = _rows_fwd(f"dn_out_norm_{l}", _gated_rms, [_whole(o), (proj, DN, 5)], [], [row(dn_norm_w[l])],
                            [(DN, MXU_DTYPE)], tm=tmf, ngroups=NH)
        ycat = jnp.concatenate([y_conv, y_dn], axis=1)
        passed = []
        if l > 0 and nxt is not None:
            y, passed = _mm(f"out_{l}", ycat, W["w_out"], "nn", comm=_gather_d2d(list(got_in) + list(got_po)))
        elif l > 0:
            y = _mm(f"out_{l}", ycat, W["w_out"], "nn")
        else:
            y, got = _mm(f"out_{l}", ycat, W["w_out"], "nn", comm=_gather_d2d(list(late_up) + list(late_down)))
            W.update(natural(got, order[3:]))
        x1, h2 = _rows_fwd(f"junction2_{l}", _junction, [_whole(x0), _whole(y)], [g1_, sh2, sc2],
                           [row(norm2_w[l])], [(D, F32), (D, MXU_DTYPE)], tm=tmf, tpe=S // tmf)
        relu2 = (lambda t: (t, _relu2(t)[0]), [], [F32, MXU_DTYPE])
        if nxt is not None:
            (up, act), got_up = _mm(f"up_{l}", h2, W["w_up"], "nn", post=relu2, comm=_gather_ici(nxt[3:4]))
            mlp, got_down = _mm(f"down_{l}", act, W["w_down"], "nn", comm=_gather_ici(nxt[4:]))
            travelling = ([] if passed else list(got_in) + list(got_po)) + list(got_up) + list(got_down)
        else:
            up, act = _mm(f"up_{l}", h2, W["w_up"], "nn", post=relu2)
            mlp = _mm(f"down_{l}", act, W["w_down"], "nn")
        sv.update(x0=x0, h1=h1, proj=proj, u0=u0, u1=u1, u2=u2, u3=u3, qkv=qkv, al=al, dtb=dtb, u=u, w=w, qd=qd, kd=kd,
                  gl=gl, attn=attn, o=o, st=st, ycat=ycat, y=y, x1=x1, h2=h2, up=up, act=act)
        saved.append(sv)
        xcur, ycur, gate_prev = x1, mlp, g2_

    shf, scf = split_mod(mod_me[L], 2)
    tgt = loss_target.reshape(T, D)
    rowloss, dx, dy, dgate, dshf, dscf, dfinal_norm = _rows_vjp(
        "loss_head", _final_loss, [_whole(xcur), _whole(ycur), _whole(tgt)], [gate_prev, shf, scf], [row(final_norm_w)],
        None, [F32, MXU_DTYPE, None], tm=tmb, tpe=S // tmb, primal=[(1, F32)])
    loss = lax.psum(jnp.sum(rowloss), ("x", "y", "c"))

    dmods = [None] * L
    small = [None] * L
    stacked = [None] * 5
    pending = None
    for l in reversed(range(L)):
        W, sv = wfull[l], saved[l]
        d_relu2 = (lambda t, u_: (t * (2.0 * jnp.maximum(u_, 0.0)),), [sv["up"]], [MXU_DTYPE])
        if pending is None:
            dup = _mm(f"d_down_x_{l}", dy, W["w_down"], "nt", post=d_relu2)
            gw_down = _mm(f"d_down_w_{l}", sv["act"], dy, "tn", COMM_DTYPE)
            dh2 = _mm(f"d_up_x_{l}", dup, W["w_up"], "nt")
            gw_up = _mm(f"d_up_w_{l}", sv["h2"], dup, "tn", COMM_DTYPE)
        else:
            dup, got = _mm(f"d_down_x_{l}", dy, W["w_down"], "nt", post=d_relu2, comm=_swap_halves(pending))
            halves = [_add_half(f"rs_add_half_{l + 1}_{a}", idx, g, r) for a, (g, r) in enumerate(zip(pending, got))]
            gw_down, got_down = _mm(f"d_down_w_{l}", sv["act"], dy, "tn", COMM_DTYPE, comm=_scatter_chips(halves[4:]))
            dh2, got_in = _mm(f"d_up_x_{l}", dup, W["w_up"], "nt", comm=_scatter_chips(halves[:1]))
            gw_up, got_po = _mm(f"d_up_w_{l}", sv["h2"], dup, "tn", COMM_DTYPE, comm=_scatter_chips(halves[1:3]))
        shard4 = lambda g: g.reshape(4, 2, g.shape[1] // 2, g.shape[2])
        d_junction2 = functools.partial(
            _rows_vjp, f"d_junction2_{l}", _junction, [_whole(sv["x0"]), _whole(sv["y"])], [sv["g1"], sv["sh2"], sv["sc2"]],
            [row(norm2_w[l])], [_whole(dx), _whole(dh2)], [F32, MXU_DTYPE], tm=tmb, tpe=S // tmb)
        early = None
        if l > 0:
            dx0, dyo, dg1, dsh2, dsc2, dn2 = d_junction2()
        else:
            early = [shard4(jnp.moveaxis(gw_up.reshape(D, 4, FF // 4), 1, 0)), shard4(gw_down.reshape(4, FF // 4, D))]
            (dx0, dyo, dg1, dsh2, dsc2, dn2), got = d_junction2(comm=_swap_halves(early))
            halves_e = [_add_half(f"rs_add_half_0_{3 + a}", idx, g, r) for a, (g, r) in enumerate(zip(early, got))]
        dycat = _mm(f"d_out_x_{l}", dyo, W["w_out"], "nt")
        gw_out = _mm(f"d_out_w_{l}", sv["ycat"], dyo, "tn", COMM_DTYPE)
        proj = sv["proj"]
        do, dz, ddn = _rows_vjp(f"d_dn_out_norm_{l}", _gated_rms, [_whole(sv["o"]), (proj, DN, 5)], [],
                                [row(dn_norm_w[l])], [(dycat, DN, 1)], [F32, MXU_DTYPE], tm=tmb, ngroups=NH)
        cts, got_eu = _dn_scan_bwd(f"d_dn_scan_{l}", do, sv["u"], sv["w"], sv["qd"], sv["kd"], sv["gl"], sv["attn"], sv["st"],
                                   NH=NH, B=B, comm=None if early is None else _scatter_chips(halves_e[:1]))
        (dq, dk, dv, dba, dal, ddt), got_up = _dn_prep_bwd(
            f"d_dn_prep_{l}", sv["qkv"], proj, 6 * C1, sv["al"], sv["dtb"], list(cts), F32, NH=NH,
            comm=None if pending is None else _scatter_chips(halves[3:4]))
        if pending is not None:
            got = list(got_in) + list(got_po) + list(got_up) + list(got_down)
            parts = [_add_chips(f"rs_add_chips_{l + 1}_{a}", idx, h, r, l + 1, L, stacked[a])
                     for a, (h, r) in enumerate(zip(halves, got))]
        dqkv, gqw = [], []
        for k, dpart in enumerate((dq, dk, dv)):
            dxp, dwp, _ = _conv_bwd(f"d_qkv_conv_{l}_{k}", proj, (2 + k) * C1, dpart,
                                    qkv_w_full[l][:, k * DN:(k + 1) * DN], MXU_DTYPE, ts=tsq, tpe=S // tsq)
            dqkv.append(dxp)
            gqw.append(dwp)
        (du3, dcon) = _rows_vjp(f"d_conv_out_norm_{l}", _rms, [_whole(sv["u3"])], [], [row(conv_out_norm_w[l])],
                                [(dycat, C1, 0)], [MXU_DTYPE], tm=tmb)
        du2 = _mm(f"d_pw2_x_{l}", du3, W["w_pw2"], "nt")
        gw_pw2 = _mm(f"d_pw2_w_{l}", sv["u2"], du3, "tn", COMM_DTYPE)
        du1, dlnw, dlnb = _rows_vjp(f"d_ln_silu_{l}", _ln_silu, [_whole(sv["u1"])], [],
                                    [row(conv_ln_w[l]), row(conv_ln_b[l])], [_whole(du2)], [F32], tm=tmb)
        if early is None:
            du0, gcw, gcb = _conv_bwd(f"d_conv_{l}", sv["u0"], 0, du1, conv_w_full[l], F32, ts=ts, tpe=S // ts)
        else:
            (du0, gcw, gcb), got_ed = _conv_bwd(f"d_conv_{l}", sv["u0"], 0, du1, conv_w_full[l], F32, ts=ts, tpe=S // ts,
                                                comm=_scatter_chips(halves_e[1:]))
        dval, dgate_c = _rows_vjp(f"d_glu_{l}", _glu, [(proj, C1, 0), (proj, C1, 1)], [], [], [_whole(du0)],
                                  [MXU_DTYPE, MXU_DTYPE], tm=tmb)
        dproj = jnp.concatenate([dval, dgate_c] + dqkv + [dz, dba.astype(MXU_DTYPE)], axis=1)
        if pending is None:
            dh1 = _mm(f"d_proj_x_{l}", dproj, W["w_in"], "nt", tk=896)
        else:
            dh1, stacked = _mm(f"d_proj_x_{l}", dproj, W["w_in"], "nt", tk=896, comm=_join_halves(parts, l + 1))
        gw_in = _mm(f"d_proj_w_{l}", sv["h1"], dproj, "tn", COMM_DTYPE, tn=896)[:, :IN]
        d_junction1 = functools.partial(
            _rows_vjp, f"d_junction1_{l}", _junction, [_whole(sv["x_in"]), _whole(sv["y_in"])],
            [sv["gate_in"], sv["sh1"], sv["sc1"]], [row(norm1_w[l])], [_whole(dx0), _whole(dh1)], [F32, MXU_DTYPE],
            tm=tmb, tpe=S // tmb)
        if early is None:
            dxn, dyn, dg2p, dsh1, dsc1, dn1 = d_junction1()
        else:
            parts_e = [_add_chips(f"rs_add_chips_0_{3 + a}", idx, h, r, 0, L, stacked[3 + a])
                       for a, (h, r) in enumerate(zip(halves_e, list(got_eu) + list(got_ed)))]
            (dxn, dyn, dg2p, dsh1, dsc1, dn1), joined = d_junction1(comm=_join_halves(parts_e, 0))
            stacked = list(stacked[:3]) + list(joined)
        dmods[l] = [dsh1, dsc1, dg1, dsh2, dsc2, dgate]
        small[l] = dict(norm1_w=dn1, conv_dw_w=gcw, conv_dw_b=gcb, conv_ln_w=dlnw, conv_ln_b=dlnb, conv_out_norm_w=dcon,
                        qkv_conv_w=jnp.concatenate(gqw, axis=1), a_log=dal[:, NH:2 * NH], dt_bias=ddt[:, NH:2 * NH],
                        dn_norm_w=ddn, norm2_w=dn2)
        pending = [shard4(jnp.moveaxis(gw_in.reshape(D, 4, IN // 4), 1, 0)), shard4(gw_pw2.reshape(4, C1 // 4, C1)),
                   shard4(gw_out.reshape(4, D // 4, D))]
        if early is None:
            pending += [shard4(jnp.moveaxis(gw_up.reshape(D, 4, FF // 4), 1, 0)), shard4(gw_down.reshape(4, FF // 4, D))]
        dx, dy, dgate = dxn, dyn, dg2p

    got = _run_comm("rs_swap_0", _swap_halves(pending))
    halves = [_add_half(f"rs_add_half_0_{a}", idx, g, r) for a, (g, r) in enumerate(zip(pending, got))]
    got = _run_comm("rs_scatter_0", _scatter_chips(halves))
    parts = [_add_chips(f"rs_add_chips_0_{a}", idx, h, r, 0, L, stacked[a]) for a, (h, r) in enumerate(zip(halves, got))]
    stacked = list(_run_comm("rs_join_0", _join_halves(parts, 0))) + list(stacked[3:])

    grad_x = dx.reshape(B, S, D)

    small_names = ["norm1_w", "conv_dw_w", "conv_dw_b", "conv_ln_w", "conv_ln_b", "conv_out_norm_w", "qkv_conv_w", "a_log",
                   "dt_bias", "dn_norm_w", "norm2_w"]
    dmod_flat = jnp.concatenate([jnp.concatenate([t.reshape(B, D) for t in dmods[l]], axis=1) for l in range(L)]
                                + [dshf.reshape(B, D), dscf.reshape(B, D)], axis=1)
    small_list = [small[l][n] for l in range(L) for n in small_names] + [dfinal_norm]
    shapes3 = [dmod_flat.shape] + [t.shape for t in small_list]
    g3 = _all_gather8("gather_grads", _pack([dmod_flat] + small_list))
    rows3 = g3.shape[0] // 8
    g3 = g3.reshape(8, rows3, PACK_LANES)
    summed = _unpack(_sum0("sum_small_grads", g3).reshape(-1), shapes3)[1:]
    dmod_all = _unpack(g3.reshape(8, -1), shapes3[:1])[0].reshape(8 * B, -1)
    nm = dmod_all.shape[1]
    grad_b_all = _sum0("sum_mod_grads", dmod_all.reshape(8 * B, nm // PACK_LANES, PACK_LANES)).reshape(-1)
    grads = {}
    dm = jnp.concatenate([lax.dynamic_slice(dmod_all, (0, l * 6 * D + s_me * NW), (8 * B, NW)) for l in range(L)], axis=1)
    grads["w_ada"] = _mm("d_ada_w", c_act, dm, "tn", stack=L)
    grads["b_ada"] = grad_b_all[:L * 6 * D].reshape(L, 6 * D)
    dm = lax.dynamic_slice(dmod_all, (0, L * 6 * D + s_me * NF), (8 * B, NF))
    grads["final_ada_w"] = _mm("d_final_ada_w", c_act, dm, "tn")
    grads["final_ada_b"] = grad_b_all[L * 6 * D:]
    per_layer = {n: [] for n in small_names}
    for l in range(L):
        for k, n in enumerate(small_names):
            per_layer[n].append(summed[l * len(small_names) + k])
    for n in small_names:
        t = jnp.stack(per_layer[n])
        if n == "conv_dw_w":
            t = lax.dynamic_slice(t, (0, 0, s_me * (C1 // 4)), (L, KC, C1 // 4))
        elif n == "qkv_conv_w":
            t = lax.dynamic_slice(t, (0, 0, s_me * (3 * DN // 4)), (L, KQ, 3 * DN // 4))
        grads[n] = t.reshape(weights[n].shape)
    grads["final_norm_w"] = summed[-1].reshape(final_norm_w.shape)
    for k, n in enumerate(["w_in", "w_pw2", "w_out", "w_up", "w_down"]):
        grads[n] = stacked[k].reshape(weights[n].shape)

    delta, new_m, new_v = {}, {}, {}
    big_names = ["w_ada", "w_in", "w_pw2", "w_out", "w_up", "w_down", "final_ada_w"]
    for n in big_names:
        shp = weights[n].shape
        three = lambda t: t.reshape((-1,) + shp[-2:])
        d_, m_, v_ = _adamw(f"adamw_{n}", three(weights[n]), three(grads[n]), three(mom1[n]), three(mom2[n]))
        delta[n], new_m[n], new_v[n] = d_.reshape(shp), m_.reshape(shp), v_.reshape(shp)
    rest = [n for n in names if n not in big_names]
    rshapes = [weights[n].shape for n in rest]
    packed = [_pack([d[n] for n in rest])[None] for d in (weights, grads, mom1, mom2)]
    outs = _adamw("adamw_small", *packed)
    for dst, arr in zip((delta, new_m, new_v), outs):
        for n, t in zip(rest, _unpack(arr.reshape(-1), rshapes)):
            dst[n] = t

    return (loss, grad_x, *[grads[n] for n in names], *[delta[n] for n in names], *[new_m[n] for n in names],
            *[new_v[n] for n in names])
```
